```python
import jax, jax.numpy as jnp
from jax import lax
import numpy as np

D_MODEL = 1024
BATCH = 8
SEQ = 4096
DEPTH = 1

MEM_LEN = 256
NORM_EPS = 1e-6
HALF_STEP = 0.5
D_FF = 2816

RWKV_HEADS = 8
RWKV_HEAD_DIM = 64
RWKV_WIDTH = RWKV_HEADS * RWKV_HEAD_DIM
DECAY_LORA = 64
AAA_LORA = 64
GATE_LORA = 128
RWKV_GN_EPS = 64e-5
RWKV_SIZES = (RWKV_WIDTH, RWKV_WIDTH, RWKV_WIDTH, DECAY_LORA, AAA_LORA, GATE_LORA)
RWKV_COLS = 3 * RWKV_WIDTH + DECAY_LORA + AAA_LORA + GATE_LORA

SWA_Q_HEADS = 8
SWA_KV_HEADS = 2
SWA_GROUP = SWA_Q_HEADS // SWA_KV_HEADS
SWA_HEAD_DIM = 64
SWA_SIZES = (SWA_Q_HEADS * SWA_HEAD_DIM, SWA_KV_HEADS * SWA_HEAD_DIM, SWA_KV_HEADS * SWA_HEAD_DIM)
SWA_COLS = SWA_Q_HEADS * SWA_HEAD_DIM + 2 * SWA_KV_HEADS * SWA_HEAD_DIM
WINDOW = 128
BLOCK = 128
ROPE_THETA = 500000.0
ROPE_DIM = SWA_HEAD_DIM // 4

MEM_HEADS = 4
MEM_HEAD_DIM = 128
MEM_WIDTH = MEM_HEADS * MEM_HEAD_DIM

N_BRANCH = 3
BRANCH_WIDTH = 512
IN_SIZES = (RWKV_COLS, SWA_COLS, MEM_WIDTH, N_BRANCH * D_MODEL)
IN_COLS = RWKV_COLS + SWA_COLS + MEM_WIDTH + N_BRANCH * D_MODEL
NEG_INF = -1e30

kernel_name = 'hybrid_rwkv7_swa_sink_memxattn_macaron'


def _split(t, sizes):
    out, start = [], 0
    for s in sizes:
        out.append(t[..., start:start + s])
        start += s
    return out


def rms_norm(x, g):
    xf = x.astype(jnp.float32)
    y = xf * lax.rsqrt(jnp.mean(xf * xf, axis=-1, keepdims=True) + NORM_EPS)
    return (y * g.astype(jnp.float32)).astype(x.dtype)


def swiglu(u, w_in, w_out):
    gate, up = jnp.split(u @ w_in, 2, axis=-1)
    return (jax.nn.silu(gate) * up) @ w_out


def rope_tables(positions):
    inv_freq = ROPE_THETA ** (-jnp.arange(0, ROPE_DIM, 2, dtype=jnp.float32) / ROPE_DIM)
    ang = positions.astype(jnp.float32)[..., None] * inv_freq
    return jnp.cos(ang)[:, :, None, :], jnp.sin(ang)[:, :, None, :]


def partial_rope(t, cos, sin):
    half = ROPE_DIM // 2
    tf = t[..., :ROPE_DIM].astype(jnp.float32)
    x1, x2 = tf[..., :half], tf[..., half:]
    rot = jnp.concatenate([x1 * cos - x2 * sin, x2 * cos + x1 * sin], axis=-1)
    return jnp.concatenate([rot.astype(t.dtype), t[..., ROPE_DIM:]], axis=-1)


def rwkv7_recurrence(r, decay, k, v, a, b):
    def step(state, inp):
        r_t, w_t, k_t, v_t, a_t, b_t = inp
        sa = jnp.einsum('bhvk,bhk->bhv', state, a_t)
        state = (state * w_t[:, :, None, :] + sa[..., None] * b_t[:, :, None, :]
                 + v_t[..., None] * k_t[:, :, None, :])
        return state, jnp.einsum('bhvk,bhk->bhv', state, r_t)

    xs = (jnp.swapaxes(r, 0, 1), jnp.swapaxes(decay, 0, 1), jnp.swapaxes(k, 0, 1),
          jnp.swapaxes(v, 0, 1), jnp.swapaxes(a, 0, 1), jnp.swapaxes(b, 0, 1))
    B, _, H, N = r.shape
    s0 = jnp.zeros((B, H, N, N), jnp.float32)
    _, y = lax.scan(step, s0, xs)
    return jnp.swapaxes(y, 0, 1)


def rwkv7_mix(p, mu, w0, w_up, a0, a_up, g_up, k_k, k_a, r_k, gn_g, gn_b):
    f32 = jnp.float32
    B, S, _ = p.shape
    H, N = RWKV_HEADS, RWKV_HEAD_DIM
    p_prev = jnp.pad(p, ((0, 0), (1, 0), (0, 0)))[:, :-1]
    p = p + (p_prev - p) * mu
    r, k, v, wd, ad, gd = _split(p, RWKV_SIZES)
    w_log = -jax.nn.softplus(-(w0 + jnp.tanh(wd) @ w_up).astype(f32)) - 0.5
    decay = jnp.exp(-jnp.exp(w_log))
    a = jax.nn.sigmoid((a0 + ad @ a_up).astype(f32))
    g = (jax.nn.sigmoid(gd) @ g_up).astype(f32)
    r = r.astype(f32)
    k = k.astype(f32)
    v = v.astype(f32)
    kk = (k * k_k.astype(f32)).reshape(B, S, H, N)
    kk = kk / jnp.maximum(jnp.sqrt(jnp.sum(kk * kk, axis=-1, keepdims=True)), 1e-12)
    k = k * (1.0 + (a - 1.0) * k_a.astype(f32))
    r = r.reshape(B, S, H, N)
    k = k.reshape(B, S, H, N)
    v = v.reshape(B, S, H, N)
    a = a.reshape(B, S, H, N)
    decay = decay.reshape(B, S, H, N)
    y = rwkv7_recurrence(r, decay, k, v, -kk, kk * a)
    mean = jnp.mean(y, axis=-1, keepdims=True)
    var = jnp.mean(jnp.square(y - mean), axis=-1, keepdims=True)
    y = ((y - mean) * lax.rsqrt(var + RWKV_GN_EPS) * gn_g.astype(f32).reshape(H, N)
         + gn_b.astype(f32).reshape(H, N))
    y = y + jnp.sum(r * k * r_k.astype(f32), axis=-1, keepdims=True) * v
    return (y.reshape(B, S, RWKV_WIDTH) * g).astype(p.dtype)


def sliding_window_gqa_sinks(q, k, v, sinks):
    f32 = jnp.float32
    B, S = q.shape[0], q.shape[1]
    nb = S // BLOCK
    scale = SWA_HEAD_DIM ** -0.5
    qb = q.astype(f32).reshape(B, nb, BLOCK, SWA_KV_HEADS, SWA_GROUP, SWA_HEAD_DIM) * scale

    def banded(t):
        tb = t.astype(f32).reshape(B, nb, BLOCK, SWA_KV_HEADS, SWA_HEAD_DIM)
        prev = jnp.pad(tb, ((0, 0), (1, 0), (0, 0), (0, 0), (0, 0)))[:, :-1]
        return jnp.concatenate([prev, tb], axis=2)

    kw, vw = banded(k), banded(v)
    s = jnp.einsum('bnqhgd,bnkhd->bnhgqk', qb, kw)
    qi = jnp.arange(BLOCK)
    kj = jnp.arange(2 * BLOCK)
    blk = jnp.arange(nb)
    diff = qi[:, None] + BLOCK - kj[None, :]
    valid = (diff >= 0) & (diff < WINDOW)
    key_pos = blk[:, None] * BLOCK - BLOCK + kj[None, :]
    valid = valid[None, :, :] & (key_pos >= 0)[:, None, :]
    s = jnp.where(valid[None, :, None, None], s, NEG_INF)
    sink = sinks.astype(f32).reshape(SWA_KV_HEADS, SWA_GROUP)[None, None, :, :, None, None]
    sink = jnp.broadcast_to(sink, s.shape[:-1] + (1,))
    pr = jax.nn.softmax(jnp.concatenate([s, sink], axis=-1), axis=-1)[..., :-1]
    o = jnp.einsum('bnhgqk,bnkhd->bnqhgd', pr, vw)
    return o.reshape(B, S, SWA_Q_HEADS * SWA_HEAD_DIM).astype(q.dtype)


def memory_cross_attention(q, mem_n, w_kv):
    f32 = jnp.float32
    B, S = q.shape[0], q.shape[1]
    km, vm = jnp.split(mem_n @ w_kv, 2, axis=-1)
    km = km.reshape(B, -1, MEM_HEADS, MEM_HEAD_DIM).astype(f32)
    vm = vm.reshape(B, -1, MEM_HEADS, MEM_HEAD_DIM).astype(f32)
    s = jnp.einsum('bshd,bmhd->bhsm', q.astype(f32) * MEM_HEAD_DIM ** -0.5, km)
    pr = jax.nn.softmax(s, axis=-1)
    o = jnp.einsum('bhsm,bmhd->bshd', pr, vm)
    return o.reshape(B, S, MEM_WIDTH).astype(q.dtype)


def setup_inputs(seed: int = 0) -> dict:
    key = jax.random.key(seed)
    ks = iter(jax.random.split(key, 40))
    f32 = jnp.float32
    L, D = DEPTH, D_MODEL

    def nrm(shape, scale):
        return scale * jax.random.normal(next(ks), shape, f32)

    def gain(n):
        return 1.0 + 0.05 * jax.random.normal(next(ks), (L, n), f32)

    x = jax.random.normal(next(ks), (BATCH, SEQ, D), f32)
    mem = jax.random.normal(next(ks), (BATCH, MEM_LEN, D), f32)
    start = jax.random.randint(next(ks), (BATCH, 1), 0, 1024, dtype=jnp.int32)
    positions = start + jnp.arange(SEQ, dtype=jnp.int32)[None, :]
    return {
        'x': x,
        'mem': mem,
        'positions': positions,
        'ffn1_pre_g': gain(D),
        'ffn1_w_in': nrm((L, D, 2 * D_FF), D ** -0.5),
        'ffn1_w_out': nrm((L, D_FF, D), D_FF ** -0.5),
        'ffn1_post_g': gain(D),
        'mix_pre_g': gain(D),
        'w_in': nrm((L, D, IN_COLS), D ** -0.5),
        'gate_b': nrm((L, N_BRANCH * D), 0.1),
        'rwkv_mu': jax.random.uniform(next(ks), (L, RWKV_COLS), f32),
        'rwkv_w0': jax.random.uniform(next(ks), (L, RWKV_WIDTH), f32, -3.0, 3.0),
        'rwkv_w_up': nrm((L, DECAY_LORA, RWKV_WIDTH), 0.1),
        'rwkv_a0': nrm((L, RWKV_WIDTH), 0.5),
        'rwkv_a_up': nrm((L, AAA_LORA, RWKV_WIDTH), 0.1),
        'rwkv_g_up': nrm((L, GATE_LORA, RWKV_WIDTH), GATE_LORA ** -0.5),
        'rwkv_k_k': 0.85 + nrm((L, RWKV_WIDTH), 0.05),
        'rwkv_k_a': 1.0 + nrm((L, RWKV_WIDTH), 0.05),
        'rwkv_r_k': nrm((L, RWKV_HEADS, RWKV_HEAD_DIM), 0.1),
        'rwkv_gn_g': gain(RWKV_WIDTH),
        'rwkv_gn_b': nrm((L, RWKV_WIDTH), 0.01),
        'swa_sinks': nrm((L, SWA_Q_HEADS), 1.0),
        'mem_norm_g': gain(D),
        'mem_w_kv': nrm((L, D, 2 * MEM_WIDTH), D ** -0.5),
        'w_branch': nrm((L, N_BRANCH, BRANCH_WIDTH, D), BRANCH_WIDTH ** -0.5),
        'w_out': nrm((L, D, D), D ** -0.5),
        'mix_post_g': gain(D),
        'ffn2_pre_g': gain(D),
        'ffn2_w_in': nrm((L, D, 2 * D_FF), D ** -0.5),
        'ffn2_w_out': nrm((L, D_FF, D), D_FF ** -0.5),
        'ffn2_post_g': gain(D),
    }


def reference(x, mem, positions, ffn1_pre_g, ffn1_w_in, ffn1_w_out, ffn1_post_g,
              mix_pre_g, w_in, gate_b, rwkv_mu, rwkv_w0, rwkv_w_up, rwkv_a0, rwkv_a_up,
              rwkv_g_up, rwkv_k_k, rwkv_k_a, rwkv_r_k, rwkv_gn_g, rwkv_gn_b, swa_sinks,
              mem_norm_g, mem_w_kv, w_branch, w_out, mix_post_g,
              ffn2_pre_g, ffn2_w_in, ffn2_w_out, ffn2_post_g):
    B, S, D = x.shape
    cos, sin = rope_tables(positions)
    h = x
    for l in range(DEPTH):
        f1 = swiglu(rms_norm(h, ffn1_pre_g[l]), ffn1_w_in[l], ffn1_w_out[l])
        h = h + HALF_STEP * rms_norm(f1, ffn1_post_g[l])

        u = rms_norm(h, mix_pre_g[l])
        p_rwkv, p_swa, q_mem, gate_logits = _split(u @ w_in[l], IN_SIZES)

        y_a = rwkv7_mix(p_rwkv, rwkv_mu[l], rwkv_w0[l], rwkv_w_up[l], rwkv_a0[l],
                        rwkv_a_up[l], rwkv_g_up[l], rwkv_k_k[l], rwkv_k_a[l], rwkv_r_k[l],
                        rwkv_gn_g[l], rwkv_gn_b[l])

        q_s, k_s, v_s = _split(p_swa, SWA_SIZES)
        q_s = partial_rope(q_s.reshape(B, S, SWA_Q_HEADS, SWA_HEAD_DIM), cos, sin)
        k_s = partial_rope(k_s.reshape(B, S, SWA_KV_HEADS, SWA_HEAD_DIM), cos, sin)
        v_s = v_s.reshape(B, S, SWA_KV_HEADS, SWA_HEAD_DIM)
        y_b = sliding_window_gqa_sinks(q_s, k_s, v_s, swa_sinks[l])

        mem_n = rms_norm(mem, mem_norm_g[l])
        y_c = memory_cross_attention(q_mem.reshape(B, S, MEM_HEADS, MEM_HEAD_DIM),
                                     mem_n, mem_w_kv[l])

        g_a, g_b, g_c = _split(jax.nn.sigmoid(gate_logits + gate_b[l]), (D, D, D))
        merged = (g_a * (y_a @ w_branch[l, 0]) + g_b * (y_b @ w_branch[l, 1])
                  + g_c * (y_c @ w_branch[l, 2]))
        h = h + rms_norm(merged @ w_out[l], mix_post_g[l])

        f2 = swiglu(rms_norm(h, ffn2_pre_g[l]), ffn2_w_in[l], ffn2_w_out[l])
        h = h + HALF_STEP * rms_norm(f2, ffn2_post_g[l])
    return h
```

```python
import functools
import math

import jax
import jax.numpy as jnp
from jax import lax
from jax.experimental import pallas as pl
from jax.experimental.pallas import tpu as pltpu

F32 = jnp.float32
BF16 = jnp.bfloat16

NORM_EPS = 1e-6
HALF_STEP = 0.5

RWKV_HEADS = 8
RWKV_HEAD_DIM = 64
RWKV_WIDTH = RWKV_HEADS * RWKV_HEAD_DIM
DECAY_LORA = 64
AAA_LORA = 64
GATE_LORA = 128
RWKV_COLS = 3 * RWKV_WIDTH + DECAY_LORA + AAA_LORA + GATE_LORA
RWKV_GN_EPS = 64e-5
RWKV_CHUNK = 64

SWA_Q_HEADS = 8
SWA_KV_HEADS = 2
SWA_GROUP = SWA_Q_HEADS // SWA_KV_HEADS
SWA_HEAD_DIM = 64
SWA_Q_WIDTH = SWA_Q_HEADS * SWA_HEAD_DIM
SWA_KV_WIDTH = SWA_KV_HEADS * SWA_HEAD_DIM
SWA_COLS = SWA_Q_WIDTH + 2 * SWA_KV_WIDTH
WINDOW = 128
ROPE_THETA = 500000.0
ROPE_DIM = SWA_HEAD_DIM // 4
NEG_INF = -1e30

MEM_HEADS = 4
MEM_HEAD_DIM = 128
MEM_WIDTH = MEM_HEADS * MEM_HEAD_DIM

N_BRANCH = 3
BRANCH_WIDTH = 512

VMEM_LIMIT_BYTES = 56 * 1024 * 1024


def _dot(a, b):
    return jnp.dot(a.astype(BF16), b.astype(BF16), preferred_element_type=F32)


def _dot_nt(a, b):
    return lax.dot_general(a.astype(BF16), b.astype(BF16), (((1,), (1,)), ((), ())),
                           preferred_element_type=F32)


def _dot_tn(a, b):
    return lax.dot_general(a.astype(BF16), b.astype(BF16), (((0,), (0,)), ((), ())),
                           preferred_element_type=F32)


def _split3(x):
    hi = x.astype(BF16)
    r1 = x - hi.astype(F32)
    mid = r1.astype(BF16)
    lo = (r1 - mid.astype(F32)).astype(BF16)
    return hi, mid, lo


def _rms_norm(x, g):
    return x * lax.rsqrt(jnp.mean(x * x, axis=-1, keepdims=True) + NORM_EPS) * g


def _sigmoid(x):
    return 1.0 / (1.0 + jnp.exp(-x))


def _rwkv_kernel(p_ref, mu_ref, w0_ref, wup_ref, a0_ref, aup_ref, gup_ref, kk_ref, ka_ref,
                 rk_ref, gng_ref, gnb_ref, y_ref, carry_ref, state_ref, q_scr, y0_scr, m_scr,
                 n_scr, yh_scr):
    C, H, N = RWKV_CHUNK, RWKV_HEADS, RWKV_HEAD_DIM
    W = RWKV_WIDTH
    ts = p_ref.shape[1]
    n_chunks = ts // C

    @pl.when(pl.program_id(1) == 0)
    def _():
        carry_ref[...] = jnp.zeros_like(carry_ref)
        state_ref[...] = jnp.zeros_like(state_ref)

    p = p_ref[0].astype(F32)
    row = lax.broadcasted_iota(jnp.int32, p.shape, 0)
    p_prev = jnp.where(row == 0, carry_ref[...], pltpu.roll(p, 1, axis=0))
    carry_ref[...] = p[ts - 1:ts, :]
    xs = p + (p_prev - p) * mu_ref[...]

    r = xs[:, 0:W]
    k = xs[:, W:2 * W]
    v = xs[:, 2 * W:3 * W]
    o = 3 * W
    wd = xs[:, o:o + DECAY_LORA]
    ad = xs[:, o + DECAY_LORA:o + DECAY_LORA + AAA_LORA]
    gd = xs[:, o + DECAY_LORA + AAA_LORA:o + DECAY_LORA + AAA_LORA + GATE_LORA]

    z = w0_ref[...] + _dot(jnp.tanh(wd), wup_ref[...])
    lw = -math.exp(-0.5) * _sigmoid(z)
    a = _sigmoid(a0_ref[...] + _dot(ad, aup_ref[...]))
    g = _dot(_sigmoid(gd), gup_ref[...])
    kk_raw = k * kk_ref[...]
    k2 = k * (1.0 + (a - 1.0) * ka_ref[...])
    rkr = r * k2 * rk_ref[...]

    ci = lax.broadcasted_iota(jnp.int32, (C, C), 0)
    cj = lax.broadcasted_iota(jnp.int32, (C, C), 1)
    tri_incl = (ci >= cj)
    tri_strict = (ci > cj)
    eye = (ci == cj)
    tri_f = tri_incl.astype(BF16)

    for c in range(n_chunks):
        rows = slice(c * C, (c + 1) * C)
        lw_c = lw[rows]
        hi, mid, lo = _split3(lw_c)
        cum = (jnp.dot(tri_f, hi, preferred_element_type=F32)
               + jnp.dot(tri_f, mid, preferred_element_type=F32)
               + jnp.dot(tri_f, lo, preferred_element_type=F32))
        e_pos = jnp.exp(cum)
        e_neg = jnp.exp(-cum)
        e_prev = jnp.exp(cum - lw_c)
        total = cum[C - 1:C, :]
        e_end = jnp.exp(total - cum)
        p_end = jnp.exp(total)
        r_c, k_c, v_c, a_c = r[rows], k2[rows], v[rows], a[rows]
        kkr_c = kk_raw[rows]
        for h in range(H):
            hs = slice(h * N, (h + 1) * N)
            kk_h = kkr_c[:, hs]
            ssq = jnp.sum(kk_h * kk_h, axis=-1, keepdims=True)
            kk_h = kk_h * lax.rsqrt(jnp.maximum(ssq, 1e-24))
            b_h = kk_h * a_c[:, hs]
            a_t = -kk_h * e_prev[:, hs]
            r_t = r_c[:, hs] * e_pos[:, hs]
            b_t = b_h * e_neg[:, hs]
            k_t = k_c[:, hs] * e_neg[:, hs]
            b_e = b_h * e_end[:, hs]
            k_e = k_c[:, hs] * e_end[:, hs]
            v_h = v_c[:, hs]
            a_ab = jnp.where(tri_strict, _dot_nt(a_t, b_t), 0.0)
            a_ak = jnp.where(tri_strict, _dot_nt(a_t, k_t), 0.0)
            a_rb = jnp.where(tri_incl, _dot_nt(r_t, b_t), 0.0)
            a_rk = jnp.where(tri_incl, _dot_nt(r_t, k_t), 0.0)
            x = jnp.where(eye, 1.0, a_ab)
            pw = a_ab
            for _ in range(int(math.log2(C)) - 1):
                pw = _dot(pw, pw)
                x = x + _dot(x, pw)
            w_a = _dot(x, a_t)
            w_v = _dot(x, _dot(a_ak, v_h))
            q_scr[c, h] = r_t + _dot(a_rb, w_a)
            y0_scr[c, h] = _dot(a_rb, w_v) + _dot(a_rk, v_h)
            m_scr[c, h] = jnp.where(eye, p_end[:, hs], 0.0) + _dot_tn(b_e, w_a)
            n_scr[c, h] = _dot_tn(b_e, w_v) + _dot_tn(k_e, v_h)

    for c in range(n_chunks):
        rows = slice(c * C, (c + 1) * C)
        for h in range(H):
            hs = slice(h * N, (h + 1) * N)
            st = state_ref[h]
            yh_scr[rows, hs] = _dot(q_scr[c, h], st) + y0_scr[c, h]
            state_ref[h] = _dot(m_scr[c, h], st) + n_scr[c, h]

    y = yh_scr[...]
    outs = []
    for h in range(H):
        hs = slice(h * N, (h + 1) * N)
        y_h = y[:, hs]
        mean = jnp.mean(y_h, axis=-1, keepdims=True)
        yc = y_h - mean
        var = jnp.mean(yc * yc, axis=-1, keepdims=True)
        bonus = jnp.sum(rkr[:, hs], axis=-1, keepdims=True)
        outs.append(yc * lax.rsqrt(var + RWKV_GN_EPS) * gng_ref[:, hs] + gnb_ref[:, hs]
                    + bonus * v[:, hs])
    y_ref[0] = (jnp.concatenate(outs, axis=-1) * g).astype(y_ref.dtype)


def _rwkv_mix(p, mu, w0, w_up, a0, a_up, g_up, k_k, k_a, r_k, gn_g, gn_b, *, ts):
    B, S, _ = p.shape
    C, H, N, W = RWKV_CHUNK, RWKV_HEADS, RWKV_HEAD_DIM, RWKV_WIDTH
    n_chunks = ts // C
    row = lambda t: t.reshape(1, -1).astype(F32)
    full = lambda shape: pl.BlockSpec(shape, lambda b, i: (0,) * len(shape))
    return pl.pallas_call(
        _rwkv_kernel,
        out_shape=jax.ShapeDtypeStruct((B, S, W), BF16),
        grid=(B, S // ts),
        in_specs=[
            pl.BlockSpec((1, ts, RWKV_COLS), lambda b, i: (b, i, 0)),
            full((1, RWKV_COLS)), full((1, W)), full((DECAY_LORA, W)), full((1, W)),
            full((AAA_LORA, W)), full((GATE_LORA, W)), full((1, W)), full((1, W)),
            full((1, W)), full((1, W)), full((1, W)),
        ],
        out_specs=pl.BlockSpec((1, ts, W), lambda b, i: (b, i, 0)),
        scratch_shapes=[
            pltpu.VMEM((1, RWKV_COLS), F32),
            pltpu.VMEM((H, N, N), F32),
            pltpu.VMEM((n_chunks, H, C, N), F32),
            pltpu.VMEM((n_chunks, H, C, N), F32),
            pltpu.VMEM((n_chunks, H, N, N), F32),
            pltpu.VMEM((n_chunks, H, N, N), F32),
            pltpu.VMEM((ts, W), F32),
        ],
        compiler_params=pltpu.CompilerParams(
            dimension_semantics=("arbitrary", "arbitrary"),
            vmem_limit_bytes=VMEM_LIMIT_BYTES),
        name="rwkv7_mix",
    )(p, row(mu), row(w0), w_up.astype(BF16), row(a0), a_up.astype(BF16),
      g_up.astype(BF16), row(k_k), row(k_a), row(r_k), row(gn_g), row(gn_b))


D_MODEL = 1024
D_FF = 2816
FFN_CHUNK = 256


def _const_spec(shape):
    return pl.BlockSpec(shape, lambda *_: (0,) * len(shape), pipeline_mode=pl.Buffered(1))


def _swiglu_half_step(h, pre_g, w_in_ref, w_out_ref, post_g):
    xn = _rms_norm(h, pre_g).astype(BF16)
    acc = jnp.zeros(h.shape, F32)
    for c in range(D_FF // FFN_CHUNK):
        lo = c * FFN_CHUNK
        gate = jnp.dot(xn, w_in_ref[:, lo:lo + FFN_CHUNK], preferred_element_type=F32)
        up = jnp.dot(xn, w_in_ref[:, D_FF + lo:D_FF + lo + FFN_CHUNK],
                     preferred_element_type=F32)
        act = (gate * _sigmoid(gate) * up).astype(BF16)
        acc = acc + jnp.dot(act, w_out_ref[lo:lo + FFN_CHUNK, :], preferred_element_type=F32)
    return h + HALF_STEP * _rms_norm(acc, post_g)


def _ffn_kernel(h_ref, pre_g_ref, w_in_ref, w_out_ref, post_g_ref, o_ref):
    o_ref[...] = _swiglu_half_step(h_ref[...], pre_g_ref[...], w_in_ref, w_out_ref,
                                   post_g_ref[...])


def _ffn(h, pre_g, w_in, w_out, post_g, *, tm):
    T, D = h.shape
    tile = pl.BlockSpec((tm, D), lambda i: (i, 0))
    return pl.pallas_call(
        _ffn_kernel,
        out_shape=jax.ShapeDtypeStruct((T, D), F32),
        grid=(T // tm,),
        in_specs=[tile, _const_spec((1, D)), _const_spec((D, 2 * D_FF)),
                  _const_spec((D_FF, D)), _const_spec((1, D))],
        out_specs=tile,
        compiler_params=pltpu.CompilerParams(
            dimension_semantics=("arbitrary",), vmem_limit_bytes=VMEM_LIMIT_BYTES),
        name="swiglu_half_step",
    )(h, pre_g.reshape(1, D), w_in.astype(BF16), w_out.astype(BF16), post_g.reshape(1, D))


BRANCH_COLS = RWKV_COLS + SWA_COLS + MEM_WIDTH


def _proj_kernel(h_ref, g_ref, w_ref, p_rwkv_ref, p_swa_ref, q_mem_ref):
    u = _rms_norm(h_ref[...], g_ref[...]).astype(BF16)
    p_rwkv_ref[...] = jnp.dot(u, w_ref[:, 0:RWKV_COLS],
                              preferred_element_type=F32).astype(p_rwkv_ref.dtype)
    p_swa_ref[...] = jnp.dot(u, w_ref[:, RWKV_COLS:RWKV_COLS + SWA_COLS],
                             preferred_element_type=F32).astype(p_swa_ref.dtype)
    q_mem_ref[...] = jnp.dot(u, w_ref[:, RWKV_COLS + SWA_COLS:BRANCH_COLS],
                             preferred_element_type=F32).astype(q_mem_ref.dtype)


def _proj(h, g, w_branches, *, tm):
    T, D = h.shape
    return pl.pallas_call(
        _proj_kernel,
        out_shape=(jax.ShapeDtypeStruct((T, RWKV_COLS), F32),
                   jax.ShapeDtypeStruct((T, SWA_COLS), BF16),
                   jax.ShapeDtypeStruct((T, MEM_WIDTH), BF16)),
        grid=(T // tm,),
        in_specs=[pl.BlockSpec((tm, D), lambda i: (i, 0)), _const_spec((1, D)),
                  _const_spec((D, BRANCH_COLS))],
        out_specs=(pl.BlockSpec((tm, RWKV_COLS), lambda i: (i, 0)),
                   pl.BlockSpec((tm, SWA_COLS), lambda i: (i, 0)),
                   pl.BlockSpec((tm, MEM_WIDTH), lambda i: (i, 0))),
        compiler_params=pltpu.CompilerParams(
            dimension_semantics=("arbitrary",), vmem_limit_bytes=VMEM_LIMIT_BYTES),
        name="mixer_in_proj",
    )(h, g.reshape(1, D), w_branches)


def _rope_tables(pos_col):
    half = ROPE_DIM // 2
    lane = lax.broadcasted_iota(jnp.int32, (1, 128), 1)
    d = lane % SWA_HEAD_DIM
    f = (d % half).astype(F32)
    inv_freq = jnp.exp(f * (-2.0 / ROPE_DIM * math.log(ROPE_THETA)))
    inv_freq = jnp.where(d < ROPE_DIM, inv_freq, 0.0)
    ang = pos_col * inv_freq
    c = jnp.cos(ang)
    s = jnp.sin(ang)
    s_up = jnp.where(d < half, -s, 0.0)
    s_dn = jnp.where((d >= half) & (d < ROPE_DIM), s, 0.0)
    return c, s_up, s_dn


def _apply_rope(x, tables):
    c, s_up, s_dn = tables
    half = ROPE_DIM // 2
    cols = []
    for j in range(x.shape[1] // 128):
        xj = x[:, j * 128:(j + 1) * 128]
        cols.append(xj * c + pltpu.roll(xj, 128 - half, axis=1) * s_up
                    + pltpu.roll(xj, half, axis=1) * s_dn)
    return cols[0] if len(cols) == 1 else jnp.concatenate(cols, axis=1)


def _swa_kernel(sinks_ref, p_ref, pos_ref, o_ref, kprev_ref, vprev_ref):
    tq = p_ref.shape[1]
    blk = WINDOW
    hd = SWA_HEAD_DIM
    first_tile = pl.program_id(1) == 0

    @pl.when(first_tile)
    def _():
        kprev_ref[...] = jnp.zeros_like(kprev_ref)
        vprev_ref[...] = jnp.zeros_like(vprev_ref)

    p = p_ref[0].astype(F32)
    tables = _rope_tables(pos_ref[0].astype(F32))
    q = _apply_rope(p[:, 0:SWA_Q_WIDTH], tables) * (hd ** -0.5)
    k = _apply_rope(p[:, SWA_Q_WIDTH:SWA_Q_WIDTH + SWA_KV_WIDTH], tables)
    v = p[:, SWA_Q_WIDTH + SWA_KV_WIDTH:SWA_COLS]
    q = q.astype(BF16)
    k = k.astype(BF16)
    v = v.astype(BF16)

    rows = SWA_GROUP * blk
    qi = lax.broadcasted_iota(jnp.int32, (rows, 2 * blk), 0) % blk
    kj = lax.broadcasted_iota(jnp.int32, (rows, 2 * blk), 1)
    diff = qi + blk - kj
    band = (diff >= 0) & (diff < WINDOW)
    band_first = band & (kj >= jnp.where(first_tile, blk, 0))

    outs = []
    for j in range(tq // blk):
        rs = slice(j * blk, (j + 1) * blk)
        k_prev = kprev_ref[...] if j == 0 else k[(j - 1) * blk:j * blk]
        v_prev = vprev_ref[...] if j == 0 else v[(j - 1) * blk:j * blk]
        valid = band_first if j == 0 else band
        heads = [None] * SWA_Q_HEADS
        for g in range(SWA_KV_HEADS):
            gs = slice(g * hd, (g + 1) * hd)
            q_g = jnp.concatenate(
                [q[rs, (g * SWA_GROUP + i) * hd:(g * SWA_GROUP + i + 1) * hd]
                 for i in range(SWA_GROUP)], axis=0)
            k_win = jnp.concatenate([k_prev[:, gs], k[rs, gs]], axis=0)
            v_win = jnp.concatenate([v_prev[:, gs], v[rs, gs]], axis=0)
            s = jnp.where(valid, _dot_nt(q_g, k_win), NEG_INF)
            sink = jnp.concatenate(
                [jnp.full((blk, 1), sinks_ref[g * SWA_GROUP + i], F32)
                 for i in range(SWA_GROUP)], axis=0)
            m = jnp.maximum(jnp.max(s, axis=-1, keepdims=True), sink)
            e = jnp.exp(s - m)
            denom = jnp.sum(e, axis=-1, keepdims=True) + jnp.exp(sink - m)
            o = _dot(e, v_win) / denom
            for i in range(SWA_GROUP):
                heads[g * SWA_GROUP + i] = o[i * blk:(i + 1) * blk]
        outs.append(jnp.concatenate(heads, axis=1))
    kprev_ref[...] = k[tq - blk:tq]
    vprev_ref[...] = v[tq - blk:tq]
    o_ref[0] = jnp.concatenate(outs, axis=0).astype(o_ref.dtype)


def _swa(p_swa, positions, sinks, *, tq):
    B, S, _ = p_swa.shape
    return pl.pallas_call(
        _swa_kernel,
        out_shape=jax.ShapeDtypeStruct((B, S, SWA_Q_WIDTH), BF16),
        grid_spec=pltpu.PrefetchScalarGridSpec(
            num_scalar_prefetch=1,
            grid=(B, S // tq),
            in_specs=[pl.BlockSpec((1, tq, SWA_COLS), lambda b, i, *_: (b, i, 0)),
                      pl.BlockSpec((1, tq, 1), lambda b, i, *_: (b, i, 0))],
            out_specs=pl.BlockSpec((1, tq, SWA_Q_WIDTH), lambda b, i, *_: (b, i, 0)),
            scratch_shapes=[pltpu.VMEM((WINDOW, SWA_KV_WIDTH), BF16),
                            pltpu.VMEM((WINDOW, SWA_KV_WIDTH), BF16)]),
        compiler_params=pltpu.CompilerParams(
            dimension_semantics=("arbitrary", "arbitrary"),
            vmem_limit_bytes=VMEM_LIMIT_BYTES),
        name="swa_gqa_sinks",
    )(sinks.astype(F32), p_swa, positions.reshape(B, S, 1))


def _mem_kv_kernel(mem_ref, g_ref, w_ref, km_ref, vm_ref):
    mn = _rms_norm(mem_ref[0], g_ref[...]).astype(BF16)
    km_ref[0] = jnp.dot(mn, w_ref[:, 0:MEM_WIDTH],
                        preferred_element_type=F32).astype(km_ref.dtype)
    vm_ref[0] = jnp.dot(mn, w_ref[:, MEM_WIDTH:2 * MEM_WIDTH],
                        preferred_element_type=F32).astype(vm_ref.dtype)


def _mem_kv(mem, g, w_kv):
    B, M, D = mem.shape
    out = jax.ShapeDtypeStruct((B, M, MEM_WIDTH), BF16)
    return pl.pallas_call(
        _mem_kv_kernel,
        out_shape=(out, out),
        grid=(B,),
        in_specs=[pl.BlockSpec((1, M, D), lambda b: (b, 0, 0)), _const_spec((1, D)),
                  _const_spec((D, 2 * MEM_WIDTH))],
        out_specs=(pl.BlockSpec((1, M, MEM_WIDTH), lambda b: (b, 0, 0)),
                   pl.BlockSpec((1, M, MEM_WIDTH), lambda b: (b, 0, 0))),
        compiler_params=pltpu.CompilerParams(
            dimension_semantics=("arbitrary",), vmem_limit_bytes=VMEM_LIMIT_BYTES),
        name="mem_kv_proj",
    )(mem, g.reshape(1, D), w_kv.astype(BF16))


def _mem_attention(q, km, vm):
    hd = MEM_HEAD_DIM
    q = (q * (hd ** -0.5)).astype(BF16)
    outs = []
    for h in range(MEM_HEADS):
        hs = slice(h * hd, (h + 1) * hd)
        s = _dot_nt(q[:, hs], km[:, hs])
        e = jnp.exp(s - jnp.max(s, axis=-1, keepdims=True))
        outs.append(_dot(e, vm[:, hs]) / jnp.sum(e, axis=-1, keepdims=True))
    return jnp.concatenate(outs, axis=1)


def _mem_attn_kernel(q_ref, km_ref, vm_ref, o_ref):
    o_ref[0] = _mem_attention(q_ref[0].astype(F32), km_ref[0], vm_ref[0]).astype(o_ref.dtype)


def _mem_attn(q_mem, km, vm, *, tq):
    B, S, _ = q_mem.shape
    M = km.shape[1]
    return pl.pallas_call(
        _mem_attn_kernel,
        out_shape=jax.ShapeDtypeStruct((B, S, MEM_WIDTH), BF16),
        grid=(B, S // tq),
        in_specs=[pl.BlockSpec((1, tq, MEM_WIDTH), lambda b, i: (b, i, 0)),
                  pl.BlockSpec((1, M, MEM_WIDTH), lambda b, i: (b, 0, 0)),
                  pl.BlockSpec((1, M, MEM_WIDTH), lambda b, i: (b, 0, 0))],
        out_specs=pl.BlockSpec((1, tq, MEM_WIDTH), lambda b, i: (b, i, 0)),
        compiler_params=pltpu.CompilerParams(
            dimension_semantics=("arbitrary", "arbitrary"),
            vmem_limit_bytes=VMEM_LIMIT_BYTES),
        name="mem_cross_attn",
    )(q_mem, km, vm)


def _merge_kernel(h_ref, ya_ref, yb_ref, yc_ref, pre_g_ref, wg_ref, gb_ref, wb_ref, wo_ref,
                  post_g_ref, o_ref):
    D = D_MODEL
    h = h_ref[...]
    u = _rms_norm(h, pre_g_ref[...]).astype(BF16)
    merged = jnp.zeros(h.shape, F32)
    for i, y_ref in enumerate((ya_ref, yb_ref, yc_ref)):
        logits = jnp.dot(u, wg_ref[:, i * D:(i + 1) * D], preferred_element_type=F32)
        gate = _sigmoid(logits + gb_ref[:, i * D:(i + 1) * D])
        merged = merged + gate * jnp.dot(y_ref[...], wb_ref[i], preferred_element_type=F32)
    out = jnp.dot(merged.astype(BF16), wo_ref[...], preferred_element_type=F32)
    o_ref[...] = h + _rms_norm(out, post_g_ref[...])


def _merge(h, y_a, y_b, y_c, pre_g, w_gate, gate_b, w_branch, w_out, post_g, *, tm):
    T, D = h.shape
    tile = pl.BlockSpec((tm, D), lambda i: (i, 0))
    ytile = pl.BlockSpec((tm, BRANCH_WIDTH), lambda i: (i, 0))
    return pl.pallas_call(
        _merge_kernel,
        out_shape=jax.ShapeDtypeStruct((T, D), F32),
        grid=(T // tm,),
        in_specs=[tile, ytile, ytile, ytile, _const_spec((1, D)),
                  _const_spec((D, N_BRANCH * D)), _const_spec((1, N_BRANCH * D)),
                  _const_spec((N_BRANCH, BRANCH_WIDTH, D)), _const_spec((D, D)),
                  _const_spec((1, D))],
        out_specs=tile,
        compiler_params=pltpu.CompilerParams(
            dimension_semantics=("arbitrary",), vmem_limit_bytes=VMEM_LIMIT_BYTES),
        name="gated_merge",
    )(h, y_a, y_b, y_c, pre_g.reshape(1, D), w_gate, gate_b.reshape(1, N_BRANCH * D),
      w_branch.astype(BF16), w_out.astype(BF16), post_g.reshape(1, D))


def kernel(x, mem, positions, ffn1_pre_g, ffn1_w_in, ffn1_w_out, ffn1_post_g, mix_pre_g, w_in, gate_b, rwkv_mu, rwkv_w0, rwkv_w_up, rwkv_a0, rwkv_a_up, rwkv_g_up, rwkv_k_k, rwkv_k_a, rwkv_r_k, rwkv_gn_g, rwkv_gn_b, swa_sinks, mem_norm_g, mem_w_kv, w_branch, w_out, mix_post_g, ffn2_pre_g, ffn2_w_in, ffn2_w_out, ffn2_post_g):
    B, S, D = x.shape
    T = B * S
    tm = min(512, T)
    ts = min(512, S)
    h = x.reshape(T, D)
    for l in range(ffn1_pre_g.shape[0]):
        h = _ffn(h, ffn1_pre_g[l], ffn1_w_in[l], ffn1_w_out[l], ffn1_post_g[l], tm=tm)

        w_in_bf = w_in[l].astype(BF16)
        p_rwkv, p_swa, q_mem = _proj(h, mix_pre_g[l], w_in_bf[:, :BRANCH_COLS], tm=tm)
        y_a = _rwkv_mix(p_rwkv.reshape(B, S, RWKV_COLS), rwkv_mu[l], rwkv_w0[l], rwkv_w_up[l],
                        rwkv_a0[l], rwkv_a_up[l], rwkv_g_up[l], rwkv_k_k[l], rwkv_k_a[l],
                        rwkv_r_k[l].reshape(-1), rwkv_gn_g[l], rwkv_gn_b[l], ts=ts)
        y_b = _swa(p_swa.reshape(B, S, SWA_COLS), positions, swa_sinks[l], tq=ts)
        km, vm = _mem_kv(mem, mem_norm_g[l], mem_w_kv[l])
        y_c = _mem_attn(q_mem.reshape(B, S, MEM_WIDTH), km, vm, tq=ts)
        h = _merge(h, y_a.reshape(T, -1), y_b.reshape(T, -1), y_c.reshape(T, -1), mix_pre_g[l],
                   w_in_bf[:, BRANCH_COLS:], gate_b[l], w_branch[l], w_out[l], mix_post_g[l],
                   tm=tm)

        h = _ffn(h, ffn2_pre_g[l], ffn2_w_in[l], ffn2_w_out[l], ffn2_post_g[l], tm=tm)
    return h.reshape(B, S, D)
```

```python
import functools
import math

import jax
import jax.numpy as jnp
from jax import lax
from jax.experimental import pallas as pl
from jax.experimental.pallas import tpu as pltpu

F32 = jnp.float32
BF16 = jnp.bfloat16

NORM_EPS = 1e-6
HALF_STEP = 0.5

RWKV_HEADS = 8
RWKV_HEAD_DIM = 64
RWKV_WIDTH = RWKV_HEADS * RWKV_HEAD_DIM
DECAY_LORA = 64
AAA_LORA = 64
GATE_LORA = 128
RWKV_COLS = 3 * RWKV_WIDTH + DECAY_LORA + AAA_LORA + GATE_LORA
RWKV_GN_EPS = 64e-5
RWKV_CHUNK = 64
RWKV_HEAD_GROUP = 4
assert RWKV_CHUNK == RWKV_HEAD_DIM and RWKV_HEADS % RWKV_HEAD_GROUP == 0
RWKV_INTERLEAVE_CHUNKS = 4

SWA_Q_HEADS = 8
SWA_KV_HEADS = 2
SWA_GROUP = SWA_Q_HEADS // SWA_KV_HEADS
SWA_HEAD_DIM = 64
SWA_Q_WIDTH = SWA_Q_HEADS * SWA_HEAD_DIM
SWA_KV_WIDTH = SWA_KV_HEADS * SWA_HEAD_DIM
SWA_COLS = SWA_Q_WIDTH + 2 * SWA_KV_WIDTH
WINDOW = 128
ROPE_THETA = 500000.0
ROPE_DIM = SWA_HEAD_DIM // 4
NEG_INF = -1e30

MEM_HEADS = 4
MEM_HEAD_DIM = 128
MEM_WIDTH = MEM_HEADS * MEM_HEAD_DIM

N_BRANCH = 3
BRANCH_WIDTH = 512

VMEM_LIMIT_BYTES = 56 * 1024 * 1024


def _dot(a, b):
    return jnp.dot(a.astype(BF16), b.astype(BF16), preferred_element_type=F32)


def _dot_nt(a, b):
    return lax.dot_general(a.astype(BF16), b.astype(BF16), (((1,), (1,)), ((), ())),
                           preferred_element_type=F32)


def _dot_tn(a, b):
    return lax.dot_general(a.astype(BF16), b.astype(BF16), (((0,), (0,)), ((), ())),
                           preferred_element_type=F32)


def _split3(x):
    hi = x.astype(BF16)
    r1 = x - hi.astype(F32)
    mid = r1.astype(BF16)
    lo = (r1 - mid.astype(F32)).astype(BF16)
    return hi, mid, lo


def _rms_norm(x, g):
    return x * lax.rsqrt(jnp.mean(x * x, axis=-1, keepdims=True) + NORM_EPS) * g


def _sigmoid(x):
    return 1.0 / (1.0 + jnp.exp(-x))


def _rwkv_kernel(p_ref, mu_ref, w0_ref, wup_ref, a0_ref, aup_ref, gup_ref, kk_ref, ka_ref,
                 rk_ref, gng_ref, gnb_ref, y_ref, carry_ref, state_ref, q_scr, y0_scr, m_scr,
                 n_scr, yh_scr):
    C, H, N, GH = RWKV_CHUNK, RWKV_HEADS, RWKV_HEAD_DIM, RWKV_HEAD_GROUP
    W = RWKV_WIDTH
    ts = p_ref.shape[1]
    n_chunks = ts // C

    @pl.when(pl.program_id(1) == 0)
    def _():
        carry_ref[...] = jnp.zeros_like(carry_ref)
        state_ref[...] = jnp.zeros_like(state_ref)

    p = p_ref[0].astype(F32)
    row = lax.broadcasted_iota(jnp.int32, p.shape, 0)
    p_prev = jnp.where(row == 0, carry_ref[...], pltpu.roll(p, 1, axis=0))
    carry_ref[...] = p[ts - 1:ts, :]
    xs = p + (p_prev - p) * mu_ref[...]

    r = xs[:, 0:W]
    k = xs[:, W:2 * W]
    v = xs[:, 2 * W:3 * W]
    o = 3 * W
    wd = xs[:, o:o + DECAY_LORA]
    ad = xs[:, o + DECAY_LORA:o + DECAY_LORA + AAA_LORA]
    gd = xs[:, o + DECAY_LORA + AAA_LORA:o + DECAY_LORA + AAA_LORA + GATE_LORA]

    z = w0_ref[...] + _dot(jnp.tanh(wd), wup_ref[...])
    lw = -math.exp(-0.5) * _sigmoid(z)
    a = _sigmoid(a0_ref[...] + _dot(ad, aup_ref[...]))
    g = _dot(_sigmoid(gd), gup_ref[...])
    kk_raw = k * kk_ref[...]
    k2 = k * (1.0 + (a - 1.0) * ka_ref[...])
    rkr = r * k2 * rk_ref[...]

    GL = GH * N
    row_blk = lax.broadcasted_iota(jnp.int32, (GL, GL), 0) // N
    col_blk = lax.broadcasted_iota(jnp.int32, (GL, GL), 1) // N
    same_head = (row_blk == col_blk).astype(BF16)
    ti = lax.broadcasted_iota(jnp.int32, (C, GL), 0)
    si = lax.broadcasted_iota(jnp.int32, (C, GL), 1) % N
    cat_incl = ti >= si
    cat_strict = ti > si
    cat_eye = ti == si
    eye_rep = cat_eye.astype(BF16)
    ci = lax.broadcasted_iota(jnp.int32, (C, C), 0)
    cj = lax.broadcasted_iota(jnp.int32, (C, C), 1)
    tri_f = (ci >= cj).astype(BF16)

    def block_diag(x):
        xb = x.astype(BF16)
        return jnp.concatenate([xb] * GH, axis=0) * same_head

    def head_sum(x):
        return jnp.dot(x.astype(BF16), same_head, preferred_element_type=F32)

    kk_parts = []
    for gi in range(H // GH):
        gl = slice(gi * GL, (gi + 1) * GL)
        kk_g = kk_raw[:, gl]
        kk_parts.append(kk_g * lax.rsqrt(jnp.maximum(head_sum(kk_g * kk_g), 1e-24)))
    kk = jnp.concatenate(kk_parts, axis=1)
    bvec = kk * a

    def chunk_terms(c, gi, r_t, a_t, b_t, k_t, b_e, k_e, v_c, p_end):
        bd_b = block_diag(b_t)
        bd_k = block_diag(k_t)
        bd_v = block_diag(v_c)
        gram = _dot_nt(jnp.concatenate([a_t, r_t], axis=0),
                       jnp.concatenate([bd_b, bd_k], axis=0))
        tr = _dot_nt(eye_rep, jnp.concatenate([block_diag(b_e), block_diag(k_e)], axis=0))
        yield
        a_ab = jnp.where(cat_strict, gram[:C, :GL], 0.0)
        a_ak = jnp.where(cat_strict, gram[:C, GL:], 0.0)
        a_rb = jnp.where(cat_incl, gram[C:, :GL], 0.0)
        a_rk = jnp.where(cat_incl, gram[C:, GL:], 0.0)
        x = jnp.where(cat_eye, 1.0, a_ab)
        pw = _dot(a_ab, block_diag(a_ab))
        akv = _dot(a_ak, bd_v)
        y0 = _dot(a_rk, bd_v)
        n0 = _dot(tr[:, GL:], bd_v)
        yield
        for _ in range(int(math.log2(C)) - 2):
            sq = _dot(jnp.concatenate([pw, x], axis=0), block_diag(pw))
            pw = sq[:C]
            x = x + sq[C:]
            yield
        x = x + _dot(x, block_diag(pw))
        yield
        w = _dot(x, jnp.concatenate([block_diag(a_t), block_diag(akv)], axis=1))
        yield
        bd_w = jnp.concatenate([block_diag(w[:, :GL]), block_diag(w[:, GL:])], axis=1)
        qymn = _dot(jnp.concatenate([a_rb, tr[:, :GL]], axis=0), bd_w)
        q_scr[c, gi] = (r_t + qymn[:C, :GL]).astype(BF16)
        y0_scr[c, gi] = qymn[:C, GL:] + y0
        m_scr[c, gi] = (jnp.where(cat_eye, p_end, 0.0) + qymn[C:, :GL]).astype(BF16)
        n_scr[c, gi] = qymn[C:, GL:] + n0

    def run_interleaved(gens):
        while gens:
            alive = []
            for gen in gens:
                if next(gen, True) is None:
                    alive.append(gen)
            gens = alive

    gens = []
    for c in range(n_chunks):
        rows = slice(c * C, (c + 1) * C)
        lw_c = lw[rows]
        hi, mid, lo = _split3(lw_c)
        cum = (jnp.dot(tri_f, hi, preferred_element_type=F32)
               + jnp.dot(tri_f, mid, preferred_element_type=F32)
               + jnp.dot(tri_f, lo, preferred_element_type=F32))
        e_pos = jnp.exp(cum)
        e_neg = jnp.exp(-cum)
        e_prev = jnp.exp(cum - lw_c)
        total = cum[C - 1:C, :]
        e_end = jnp.exp(total - cum)
        p_end = jnp.exp(total)
        r_t = r[rows] * e_pos
        a_t = -kk[rows] * e_prev
        b_t = bvec[rows] * e_neg
        k_t = k2[rows] * e_neg
        b_e = bvec[rows] * e_end
        k_e = k2[rows] * e_end
        v_c = v[rows]
        for gi in range(H // GH):
            gl = slice(gi * GL, (gi + 1) * GL)
            gens.append(chunk_terms(c, gi, r_t[:, gl], a_t[:, gl], b_t[:, gl], k_t[:, gl],
                                    b_e[:, gl], k_e[:, gl], v_c[:, gl], p_end[:, gl]))
        if (c + 1) % RWKV_INTERLEAVE_CHUNKS == 0 or c + 1 == n_chunks:
            run_interleaved(gens)
            gens = []

    states = [state_ref[gi] for gi in range(H // GH)]
    for c in range(n_chunks):
        rows = slice(c * C, (c + 1) * C)
        for gi in range(H // GH):
            gl = slice(gi * GL, (gi + 1) * GL)
            res = _dot(jnp.concatenate([q_scr[c, gi], m_scr[c, gi]], axis=0),
                       block_diag(states[gi]))
            yh_scr[rows, gl] = res[:C] + y0_scr[c, gi]
            states[gi] = res[C:] + n_scr[c, gi]
    for gi in range(H // GH):
        state_ref[gi] = states[gi]

    outs = []
    for gi in range(H // GH):
        gl = slice(gi * GL, (gi + 1) * GL)
        y_g = yh_scr[:, gl]
        yc = y_g - head_sum(y_g) * (1.0 / N)
        var = head_sum(yc * yc) * (1.0 / N)
        bonus = head_sum(rkr[:, gl])
        outs.append(yc * lax.rsqrt(var + RWKV_GN_EPS) * gng_ref[:, gl] + gnb_ref[:, gl]
                    + bonus * v[:, gl])
    y_ref[0] = (jnp.concatenate(outs, axis=-1) * g).astype(y_ref.dtype)


def _rwkv_mix(p, mu, w0, w_up, a0, a_up, g_up, k_k, k_a, r_k, gn_g, gn_b, *, ts):
    B, S, _ = p.shape
    C, H, N, W = RWKV_CHUNK, RWKV_HEADS, RWKV_HEAD_DIM, RWKV_WIDTH
    n_chunks = ts // C
    NG, GL = H // RWKV_HEAD_GROUP, RWKV_HEAD_GROUP * N
    row = lambda t: t.reshape(1, -1).astype(F32)
    full = lambda shape: pl.BlockSpec(shape, lambda b, i: (0,) * len(shape))
    return pl.pallas_call(
        _rwkv_kernel,
        out_shape=jax.ShapeDtypeStruct((B, S, W), BF16),
        grid=(B, S // ts),
        in_specs=[
            pl.BlockSpec((1, ts, RWKV_COLS), lambda b, i: (b, i, 0)),
            full((1, RWKV_COLS)), full((1, W)), full((DECAY_LORA, W)), full((1, W)),
            full((AAA_LORA, W)), full((GATE_LORA, W)), full((1, W)), full((1, W)),
            full((1, W)), full((1, W)), full((1, W)),
        ],
        out_specs=pl.BlockSpec((1, ts, W), lambda b, i: (b, i, 0)),
        scratch_shapes=[
            pltpu.VMEM((1, RWKV_COLS), F32),
            pltpu.VMEM((NG, N, GL), F32),
            pltpu.VMEM((n_chunks, NG, C, GL), BF16),
            pltpu.VMEM((n_chunks, NG, C, GL), F32),
            pltpu.VMEM((n_chunks, NG, N, GL), BF16),
            pltpu.VMEM((n_chunks, NG, N, GL), F32),
            pltpu.VMEM((ts, W), F32),
        ],
        compiler_params=pltpu.CompilerParams(
            dimension_semantics=("arbitrary", "arbitrary"),
            vmem_limit_bytes=VMEM_LIMIT_BYTES),
        name="rwkv7_mix",
    )(p, row(mu), row(w0), w_up.astype(BF16), row(a0), a_up.astype(BF16),
      g_up.astype(BF16), row(k_k), row(k_a), row(r_k), row(gn_g), row(gn_b))


D_MODEL = 1024
D_FF = 2816
FFN_CHUNK = 256


def _const_spec(shape):
    return pl.BlockSpec(shape, lambda *_: (0,) * len(shape), pipeline_mode=pl.Buffered(1))


def _swiglu_half_step(h, pre_g, w_in_ref, w_out_ref, post_g):
    xn = _rms_norm(h, pre_g).astype(BF16)
    acc = jnp.zeros(h.shape, F32)
    for c in range(D_FF // FFN_CHUNK):
        lo = c * FFN_CHUNK
        gate = jnp.dot(xn, w_in_ref[:, lo:lo + FFN_CHUNK], preferred_element_type=F32)
        up = jnp.dot(xn, w_in_ref[:, D_FF + lo:D_FF + lo + FFN_CHUNK],
                     preferred_element_type=F32)
        act = (gate * _sigmoid(gate) * up).astype(BF16)
        acc = acc + jnp.dot(act, w_out_ref[lo:lo + FFN_CHUNK, :], preferred_element_type=F32)
    return h + HALF_STEP * _rms_norm(acc, post_g)


def _ffn_kernel(h_ref, pre_g_ref, w_in_ref, w_out_ref, post_g_ref, o_ref):
    o_ref[...] = _swiglu_half_step(h_ref[...], pre_g_ref[...], w_in_ref, w_out_ref,
                                   post_g_ref[...])


def _ffn(h, pre_g, w_in, w_out, post_g, *, tm):
    T, D = h.shape
    tile = pl.BlockSpec((tm, D), lambda i: (i, 0))
    return pl.pallas_call(
        _ffn_kernel,
        out_shape=jax.ShapeDtypeStruct((T, D), F32),
        grid=(T // tm,),
        in_specs=[tile, _const_spec((1, D)), _const_spec((D, 2 * D_FF)),
                  _const_spec((D_FF, D)), _const_spec((1, D))],
        out_specs=tile,
        compiler_params=pltpu.CompilerParams(
            dimension_semantics=("arbitrary",), vmem_limit_bytes=VMEM_LIMIT_BYTES),
        name="swiglu_half_step",
    )(h, pre_g.reshape(1, D), w_in.astype(BF16), w_out.astype(BF16), post_g.reshape(1, D))


BRANCH_COLS = RWKV_COLS + SWA_COLS + MEM_WIDTH


def _proj_kernel(h_ref, g_ref, w_ref, p_rwkv_ref, p_swa_ref, q_mem_ref):
    u = _rms_norm(h_ref[...], g_ref[...]).astype(BF16)
    p_rwkv_ref[...] = jnp.dot(u, w_ref[:, 0:RWKV_COLS],
                              preferred_element_type=F32).astype(p_rwkv_ref.dtype)
    p_swa_ref[...] = jnp.dot(u, w_ref[:, RWKV_COLS:RWKV_COLS + SWA_COLS],
                             preferred_element_type=F32).astype(p_swa_ref.dtype)
    q_mem_ref[...] = jnp.dot(u, w_ref[:, RWKV_COLS + SWA_COLS:BRANCH_COLS],
                             preferred_element_type=F32).astype(q_mem_ref.dtype)


def _proj(h, g, w_branches, *, tm):
    T, D = h.shape
    return pl.pallas_call(
        _proj_kernel,
        out_shape=(jax.ShapeDtypeStruct((T, RWKV_COLS), F32),
                   jax.ShapeDtypeStruct((T, SWA_COLS), BF16),
                   jax.ShapeDtypeStruct((T, MEM_WIDTH), BF16)),
        grid=(T // tm,),
        in_specs=[pl.BlockSpec((tm, D), lambda i: (i, 0)), _const_spec((1, D)),
                  _const_spec((D, BRANCH_COLS))],
        out_specs=(pl.BlockSpec((tm, RWKV_COLS), lambda i: (i, 0)),
                   pl.BlockSpec((tm, SWA_COLS), lambda i: (i, 0)),
                   pl.BlockSpec((tm, MEM_WIDTH), lambda i: (i, 0))),
        compiler_params=pltpu.CompilerParams(
            dimension_semantics=("arbitrary",), vmem_limit_bytes=VMEM_LIMIT_BYTES),
        name="mixer_in_proj",
    )(h, g.reshape(1, D), w_branches)


def _rope_tables(pos_col):
    half = ROPE_DIM // 2
    lane = lax.broadcasted_iota(jnp.int32, (1, 128), 1)
    d = lane % SWA_HEAD_DIM
    f = (d % half).astype(F32)
    inv_freq = jnp.exp(f * (-2.0 / ROPE_DIM * math.log(ROPE_THETA)))
    inv_freq = jnp.where(d < ROPE_DIM, inv_freq, 0.0)
    ang = pos_col * inv_freq
    c = jnp.cos(ang)
    s = jnp.sin(ang)
    s_up = jnp.where(d < half, -s, 0.0)
    s_dn = jnp.where((d >= half) & (d < ROPE_DIM), s, 0.0)
    return c, s_up, s_dn


def _apply_rope(x, tables):
    c, s_up, s_dn = tables
    half = ROPE_DIM // 2
    cols = []
    for j in range(x.shape[1] // 128):
        xj = x[:, j * 128:(j + 1) * 128]
        cols.append(xj * c + pltpu.roll(xj, 128 - half, axis=1) * s_up
                    + pltpu.roll(xj, half, axis=1) * s_dn)
    return cols[0] if len(cols) == 1 else jnp.concatenate(cols, axis=1)


def _swa_kernel(sinks_ref, p_ref, pos_ref, o_ref, kprev_ref, vprev_ref):
    tq = p_ref.shape[1]
    blk = WINDOW
    hd = SWA_HEAD_DIM
    first_tile = pl.program_id(1) == 0

    @pl.when(first_tile)
    def _():
        kprev_ref[...] = jnp.zeros_like(kprev_ref)
        vprev_ref[...] = jnp.zeros_like(vprev_ref)

    p = p_ref[0].astype(F32)
    tables = _rope_tables(pos_ref[0].astype(F32))
    q = _apply_rope(p[:, 0:SWA_Q_WIDTH], tables) * (hd ** -0.5)
    k = _apply_rope(p[:, SWA_Q_WIDTH:SWA_Q_WIDTH + SWA_KV_WIDTH], tables)
    v = p[:, SWA_Q_WIDTH + SWA_KV_WIDTH:SWA_COLS]
    q = q.astype(BF16)
    k = k.astype(BF16)
    v = v.astype(BF16)

    rows = SWA_GROUP * blk
    qi = lax.broadcasted_iota(jnp.int32, (rows, 2 * blk), 0) % blk
    kj = lax.broadcasted_iota(jnp.int32, (rows, 2 * blk), 1)
    diff = qi + blk - kj
    band = (diff >= 0) & (diff < WINDOW)
    band_first = band & (kj >= jnp.where(first_tile, blk, 0))

    outs = []
    for j in range(tq // blk):
        rs = slice(j * blk, (j + 1) * blk)
        k_prev = kprev_ref[...] if j == 0 else k[(j - 1) * blk:j * blk]
        v_prev = vprev_ref[...] if j == 0 else v[(j - 1) * blk:j * blk]
        valid = band_first if j == 0 else band
        heads = [None] * SWA_Q_HEADS
        for g in range(SWA_KV_HEADS):
            gs = slice(g * hd, (g + 1) * hd)
            q_g = jnp.concatenate(
                [q[rs, (g * SWA_GROUP + i) * hd:(g * SWA_GROUP + i + 1) * hd]
                 for i in range(SWA_GROUP)], axis=0)
            k_win = jnp.concatenate([k_prev[:, gs], k[rs, gs]], axis=0)
            v_win = jnp.concatenate([v_prev[:, gs], v[rs, gs]], axis=0)
            s = jnp.where(valid, _dot_nt(q_g, k_win), NEG_INF)
            sink = jnp.concatenate(
                [jnp.full((blk, 1), sinks_ref[g * SWA_GROUP + i], F32)
                 for i in range(SWA_GROUP)], axis=0)
            m = jnp.maximum(jnp.max(s, axis=-1, keepdims=True), sink)
            e = jnp.exp(s - m)
            denom = jnp.sum(e, axis=-1, keepdims=True) + jnp.exp(sink - m)
            o = _dot(e, v_win) / denom
            for i in range(SWA_GROUP):
                heads[g * SWA_GROUP + i] = o[i * blk:(i + 1) * blk]
        outs.append(jnp.concatenate(heads, axis=1))
    kprev_ref[...] = k[tq - blk:tq]
    vprev_ref[...] = v[tq - blk:tq]
    o_ref[0] = jnp.concatenate(outs, axis=0).astype(o_ref.dtype)


def _swa(p_swa, positions, sinks, *, tq):
    B, S, _ = p_swa.shape
    return pl.pallas_call(
        _swa_kernel,
        out_shape=jax.ShapeDtypeStruct((B, S, SWA_Q_WIDTH), BF16),
        grid_spec=pltpu.PrefetchScalarGridSpec(
            num_scalar_prefetch=1,
            grid=(B, S // tq),
            in_specs=[pl.BlockSpec((1, tq, SWA_COLS), lambda b, i, *_: (b, i, 0)),
                      pl.BlockSpec((1, tq, 1), lambda b, i, *_: (b, i, 0))],
            out_specs=pl.BlockSpec((1, tq, SWA_Q_WIDTH), lambda b, i, *_: (b, i, 0)),
            scratch_shapes=[pltpu.VMEM((WINDOW, SWA_KV_WIDTH), BF16),
                            pltpu.VMEM((WINDOW, SWA_KV_WIDTH), BF16)]),
        compiler_params=pltpu.CompilerParams(
            dimension_semantics=("arbitrary", "arbitrary"),
            vmem_limit_bytes=VMEM_LIMIT_BYTES),
        name="swa_gqa_sinks",
    )(sinks.astype(F32), p_swa, positions.reshape(B, S, 1))


def _mem_kv_kernel(mem_ref, g_ref, w_ref, km_ref, vm_ref):
    mn = _rms_norm(mem_ref[0], g_ref[...]).astype(BF16)
    km_ref[0] = jnp.dot(mn, w_ref[:, 0:MEM_WIDTH],
                        preferred_element_type=F32).astype(km_ref.dtype)
    vm_ref[0] = jnp.dot(mn, w_ref[:, MEM_WIDTH:2 * MEM_WIDTH],
                        preferred_element_type=F32).astype(vm_ref.dtype)


def _mem_kv(mem, g, w_kv):
    B, M, D = mem.shape
    out = jax.ShapeDtypeStruct((B, M, MEM_WIDTH), BF16)
    return pl.pallas_call(
        _mem_kv_kernel,
        out_shape=(out, out),
        grid=(B,),
        in_specs=[pl.BlockSpec((1, M, D), lambda b: (b, 0, 0)), _const_spec((1, D)),
                  _const_spec((D, 2 * MEM_WIDTH))],
        out_specs=(pl.BlockSpec((1, M, MEM_WIDTH), lambda b: (b, 0, 0)),
                   pl.BlockSpec((1, M, MEM_WIDTH), lambda b: (b, 0, 0))),
        compiler_params=pltpu.CompilerParams(
            dimension_semantics=("arbitrary",), vmem_limit_bytes=VMEM_LIMIT_BYTES),
        name="mem_kv_proj",
    )(mem, g.reshape(1, D), w_kv.astype(BF16))


def _mem_attention(q, km, vm):
    hd = MEM_HEAD_DIM
    q = (q * (hd ** -0.5)).astype(BF16)
    outs = []
    for h in range(MEM_HEADS):
        hs = slice(h * hd, (h + 1) * hd)
        s = _dot_nt(q[:, hs], km[:, hs])
        e = jnp.exp(s - jnp.max(s, axis=-1, keepdims=True))
        outs.append(_dot(e, vm[:, hs]) / jnp.sum(e, axis=-1, keepdims=True))
    return jnp.concatenate(outs, axis=1)


def _mem_attn_kernel(q_ref, km_ref, vm_ref, o_ref):
    o_ref[0] = _mem_attention(q_ref[0].astype(F32), km_ref[0], vm_ref[0]).astype(o_ref.dtype)


def _mem_attn(q_mem, km, vm, *, tq):
    B, S, _ = q_mem.shape
    M = km.shape[1]
    return pl.pallas_call(
        _mem_attn_kernel,
        out_shape=jax.ShapeDtypeStruct((B, S, MEM_WIDTH), BF16),
        grid=(B, S // tq),
        in_specs=[pl.BlockSpec((1, tq, MEM_WIDTH), lambda b, i: (b, i, 0)),
                  pl.BlockSpec((1, M, MEM_WIDTH), lambda b, i: (b, 0, 0)),
                  pl.BlockSpec((1, M, MEM_WIDTH), lambda b, i: (b, 0, 0))],
        out_specs=pl.BlockSpec((1, tq, MEM_WIDTH), lambda b, i: (b, i, 0)),
        compiler_params=pltpu.CompilerParams(
            dimension_semantics=("arbitrary", "arbitrary"),
            vmem_limit_bytes=VMEM_LIMIT_BYTES),
        name="mem_cross_attn",
    )(q_mem, km, vm)


def _merge_kernel(h_ref, ya_ref, yb_ref, yc_ref, pre_g_ref, wg_ref, gb_ref, wb_ref, wo_ref,
                  post_g_ref, o_ref):
    D = D_MODEL
    h = h_ref[...]
    u = _rms_norm(h, pre_g_ref[...]).astype(BF16)
    merged = jnp.zeros(h.shape, F32)
    for i, y_ref in enumerate((ya_ref, yb_ref, yc_ref)):
        logits = jnp.dot(u, wg_ref[:, i * D:(i + 1) * D], preferred_element_type=F32)
        gate = _sigmoid(logits + gb_ref[:, i * D:(i + 1) * D])
        merged = merged + gate * jnp.dot(y_ref[...], wb_ref[i], preferred_element_type=F32)
    out = jnp.dot(merged.astype(BF16), wo_ref[...], preferred_element_type=F32)
    o_ref[...] = h + _rms_norm(out, post_g_ref[...])


def _merge(h, y_a, y_b, y_c, pre_g, w_gate, gate_b, w_branch, w_out, post_g, *, tm):
    T, D = h.shape
    tile = pl.BlockSpec((tm, D), lambda i: (i, 0))
    ytile = pl.BlockSpec((tm, BRANCH_WIDTH), lambda i: (i, 0))
    return pl.pallas_call(
        _merge_kernel,
        out_shape=jax.ShapeDtypeStruct((T, D), F32),
        grid=(T // tm,),
        in_specs=[tile, ytile, ytile, ytile, _const_spec((1, D)),
                  _const_spec((D, N_BRANCH * D)), _const_spec((1, N_BRANCH * D)),
                  _const_spec((N_BRANCH, BRANCH_WIDTH, D)), _const_spec((D, D)),
                  _const_spec((1, D))],
        out_specs=tile,
        compiler_params=pltpu.CompilerParams(
            dimension_semantics=("arbitrary",), vmem_limit_bytes=VMEM_LIMIT_BYTES),
        name="gated_merge",
    )(h, y_a, y_b, y_c, pre_g.reshape(1, D), w_gate, gate_b.reshape(1, N_BRANCH * D),
      w_branch.astype(BF16), w_out.astype(BF16), post_g.reshape(1, D))


def kernel(x, mem, positions, ffn1_pre_g, ffn1_w_in, ffn1_w_out, ffn1_post_g, mix_pre_g, w_in, gate_b, rwkv_mu, rwkv_w0, rwkv_w_up, rwkv_a0, rwkv_a_up, rwkv_g_up, rwkv_k_k, rwkv_k_a, rwkv_r_k, rwkv_gn_g, rwkv_gn_b, swa_sinks, mem_norm_g, mem_w_kv, w_branch, w_out, mix_post_g, ffn2_pre_g, ffn2_w_in, ffn2_w_out, ffn2_post_g):
    B, S, D = x.shape
    T = B * S
    tm = min(512, T)
    ts = min(512, S)
    h = x.reshape(T, D)
    for l in range(ffn1_pre_g.shape[0]):
        h = _ffn(h, ffn1_pre_g[l], ffn1_w_in[l], ffn1_w_out[l], ffn1_post_g[l], tm=tm)

        w_in_bf = w_in[l].astype(BF16)
        p_rwkv, p_swa, q_mem = _proj(h, mix_pre_g[l], w_in_bf[:, :BRANCH_COLS], tm=tm)
        y_a = _rwkv_mix(p_rwkv.reshape(B, S, RWKV_COLS), rwkv_mu[l], rwkv_w0[l], rwkv_w_up[l],
                        rwkv_a0[l], rwkv_a_up[l], rwkv_g_up[l], rwkv_k_k[l], rwkv_k_a[l],
                        rwkv_r_k[l].reshape(-1), rwkv_gn_g[l], rwkv_gn_b[l], ts=ts)
        y_b = _swa(p_swa.reshape(B, S, SWA_COLS), positions, swa_sinks[l], tq=ts)
        km, vm = _mem_kv(mem, mem_norm_g[l], mem_w_kv[l])
        y_c = _mem_attn(q_mem.reshape(B, S, MEM_WIDTH), km, vm, tq=ts)
        h = _merge(h, y_a.reshape(T, -1), y_b.reshape(T, -1), y_c.reshape(T, -1), mix_pre_g[l],
                   w_in_bf[:, BRANCH_COLS:], gate_b[l], w_branch[l], w_out[l], mix_post_g[l],
                   tm=tm)

        h = _ffn(h, ffn2_pre_g[l], ffn2_w_in[l], ffn2_w_out[l], ffn2_post_g[l], tm=tm)
    return h.reshape(B, S, D)
```

```python
import functools
import math

import jax
import jax.numpy as jnp
from jax import lax
from jax.experimental import pallas as pl
from jax.experimental.pallas import tpu as pltpu

F32 = jnp.float32
BF16 = jnp.bfloat16

NORM_EPS = 1e-6
HALF_STEP = 0.5

RWKV_HEADS = 8
RWKV_HEAD_DIM = 64
RWKV_WIDTH = RWKV_HEADS * RWKV_HEAD_DIM
DECAY_LORA = 64
AAA_LORA = 64
GATE_LORA = 128
RWKV_COLS = 3 * RWKV_WIDTH + DECAY_LORA + AAA_LORA + GATE_LORA
RWKV_GN_EPS = 64e-5
RWKV_CHUNK = 64
RWKV_HEAD_GROUP = 4
assert RWKV_CHUNK == RWKV_HEAD_DIM and RWKV_HEADS % RWKV_HEAD_GROUP == 0
RWKV_INTERLEAVE_CHUNKS = 4

SWA_Q_HEADS = 8
SWA_KV_HEADS = 2
SWA_GROUP = SWA_Q_HEADS // SWA_KV_HEADS
SWA_HEAD_DIM = 64
SWA_Q_WIDTH = SWA_Q_HEADS * SWA_HEAD_DIM
SWA_KV_WIDTH = SWA_KV_HEADS * SWA_HEAD_DIM
SWA_COLS = SWA_Q_WIDTH + 2 * SWA_KV_WIDTH
WINDOW = 128
ROPE_THETA = 500000.0
ROPE_DIM = SWA_HEAD_DIM // 4
NEG_INF = -1e30

MEM_HEADS = 4
MEM_HEAD_DIM = 128
MEM_WIDTH = MEM_HEADS * MEM_HEAD_DIM

N_BRANCH = 3
BRANCH_WIDTH = 512

VMEM_LIMIT_BYTES = 56 * 1024 * 1024


def _dot(a, b):
    return jnp.dot(a.astype(BF16), b.astype(BF16), preferred_element_type=F32)


def _dot_nt(a, b):
    return lax.dot_general(a.astype(BF16), b.astype(BF16), (((1,), (1,)), ((), ())),
                           preferred_element_type=F32)


def _dot_tn(a, b):
    return lax.dot_general(a.astype(BF16), b.astype(BF16), (((0,), (0,)), ((), ())),
                           preferred_element_type=F32)


def _split3(x):
    hi = x.astype(BF16)
    r1 = x - hi.astype(F32)
    mid = r1.astype(BF16)
    lo = (r1 - mid.astype(F32)).astype(BF16)
    return hi, mid, lo


def _rms_norm(x, g):
    return x * lax.rsqrt(jnp.mean(x * x, axis=-1, keepdims=True) + NORM_EPS) * g


def _sigmoid(x):
    return 1.0 / (1.0 + jnp.exp(-x))


def _run_interleaved(gens):
    while gens:
        alive = []
        for gen in gens:
            if next(gen, True) is None:
                alive.append(gen)
        gens = alive


def _rwkv_kernel(h_ref, pre_g_ref, w_ref, mu_ref, w0_ref, wup_ref, a0_ref, aup_ref, gup_ref,
                 kk_ref, ka_ref, rk_ref, gng_ref, gnb_ref, y_ref, carry_ref, state_ref, q_scr,
                 y0_scr, m_scr, n_scr, yh_scr):
    C, H, N, GH = RWKV_CHUNK, RWKV_HEADS, RWKV_HEAD_DIM, RWKV_HEAD_GROUP
    W = RWKV_WIDTH
    ts = h_ref.shape[1]
    n_chunks = ts // C

    @pl.when(pl.program_id(1) == 0)
    def _():
        carry_ref[...] = jnp.zeros_like(carry_ref)
        state_ref[...] = jnp.zeros_like(state_ref)

    u = _rms_norm(h_ref[0], pre_g_ref[...]).astype(BF16)
    p = jnp.dot(u, w_ref[...], preferred_element_type=F32)
    row = lax.broadcasted_iota(jnp.int32, p.shape, 0)
    p_prev = jnp.where(row == 0, carry_ref[...], pltpu.roll(p, 1, axis=0))
    carry_ref[...] = p[ts - 1:ts, :]
    xs = p + (p_prev - p) * mu_ref[...]

    r = xs[:, 0:W]
    k = xs[:, W:2 * W]
    v = xs[:, 2 * W:3 * W]
    o = 3 * W
    wd = xs[:, o:o + DECAY_LORA]
    ad = xs[:, o + DECAY_LORA:o + DECAY_LORA + AAA_LORA]
    gd = xs[:, o + DECAY_LORA + AAA_LORA:o + DECAY_LORA + AAA_LORA + GATE_LORA]

    z = w0_ref[...] + _dot(jnp.tanh(wd), wup_ref[...])
    lw = -math.exp(-0.5) * _sigmoid(z)
    a = _sigmoid(a0_ref[...] + _dot(ad, aup_ref[...]))
    g = _dot(_sigmoid(gd), gup_ref[...])
    kk_raw = k * kk_ref[...]
    k2 = k * (1.0 + (a - 1.0) * ka_ref[...])
    rkr = r * k2 * rk_ref[...]

    GL = GH * N
    row_blk = lax.broadcasted_iota(jnp.int32, (GL, GL), 0) // N
    col_blk = lax.broadcasted_iota(jnp.int32, (GL, GL), 1) // N
    same_head = (row_blk == col_blk).astype(BF16)
    ti = lax.broadcasted_iota(jnp.int32, (C, GL), 0)
    si = lax.broadcasted_iota(jnp.int32, (C, GL), 1) % N
    cat_incl = ti >= si
    cat_strict = ti > si
    cat_eye = ti == si
    eye_rep = cat_eye.astype(BF16)
    ci = lax.broadcasted_iota(jnp.int32, (C, C), 0)
    cj = lax.broadcasted_iota(jnp.int32, (C, C), 1)
    tri_f = (ci >= cj).astype(BF16)

    def block_diag(x):
        xb = x.astype(BF16)
        return jnp.concatenate([xb] * GH, axis=0) * same_head

    def head_sum(x):
        return jnp.dot(x.astype(BF16), same_head, preferred_element_type=F32)

    kk_parts = []
    for gi in range(H // GH):
        gl = slice(gi * GL, (gi + 1) * GL)
        kk_g = kk_raw[:, gl]
        kk_parts.append(kk_g * lax.rsqrt(jnp.maximum(head_sum(kk_g * kk_g), 1e-24)))
    kk = jnp.concatenate(kk_parts, axis=1)
    bvec = kk * a

    def chunk_terms(c, gi, r_t, a_t, b_t, k_t, b_e, k_e, v_c, p_end):
        bd_b = block_diag(b_t)
        bd_k = block_diag(k_t)
        bd_v = block_diag(v_c)
        gram = _dot_nt(jnp.concatenate([a_t, r_t], axis=0),
                       jnp.concatenate([bd_b, bd_k], axis=0))
        tr = _dot_nt(eye_rep, jnp.concatenate([block_diag(b_e), block_diag(k_e)], axis=0))
        yield
        a_ab = jnp.where(cat_strict, gram[:C, :GL], 0.0)
        a_ak = jnp.where(cat_strict, gram[:C, GL:], 0.0)
        a_rb = jnp.where(cat_incl, gram[C:, :GL], 0.0)
        a_rk = jnp.where(cat_incl, gram[C:, GL:], 0.0)
        x = jnp.where(cat_eye, 1.0, a_ab)
        pw = _dot(a_ab, block_diag(a_ab))
        akv = _dot(a_ak, bd_v)
        y0 = _dot(a_rk, bd_v)
        n0 = _dot(tr[:, GL:], bd_v)
        yield
        for _ in range(int(math.log2(C)) - 2):
            sq = _dot(jnp.concatenate([pw, x], axis=0), block_diag(pw))
            pw = sq[:C]
            x = x + sq[C:]
            yield
        x = x + _dot(x, block_diag(pw))
        yield
        w = _dot(x, jnp.concatenate([block_diag(a_t), block_diag(akv)], axis=1))
        yield
        bd_w = jnp.concatenate([block_diag(w[:, :GL]), block_diag(w[:, GL:])], axis=1)
        qymn = _dot(jnp.concatenate([a_rb, tr[:, :GL]], axis=0), bd_w)
        q_scr[c, gi] = (r_t + qymn[:C, :GL]).astype(BF16)
        y0_scr[c, gi] = qymn[:C, GL:] + y0
        m_scr[c, gi] = (jnp.where(cat_eye, p_end, 0.0) + qymn[C:, :GL]).astype(BF16)
        n_scr[c, gi] = qymn[C:, GL:] + n0

    gens = []
    for c in range(n_chunks):
        rows = slice(c * C, (c + 1) * C)
        lw_c = lw[rows]
        hi, mid, lo = _split3(lw_c)
        cum = (jnp.dot(tri_f, hi, preferred_element_type=F32)
               + jnp.dot(tri_f, mid, preferred_element_type=F32)
               + jnp.dot(tri_f, lo, preferred_element_type=F32))
        e_pos = jnp.exp(cum)
        e_neg = jnp.exp(-cum)
        e_prev = jnp.exp(cum - lw_c)
        total = cum[C - 1:C, :]
        e_end = jnp.exp(total - cum)
        p_end = jnp.exp(total)
        r_t = r[rows] * e_pos
        a_t = -kk[rows] * e_prev
        b_t = bvec[rows] * e_neg
        k_t = k2[rows] * e_neg
        b_e = bvec[rows] * e_end
        k_e = k2[rows] * e_end
        v_c = v[rows]
        for gi in range(H // GH):
            gl = slice(gi * GL, (gi + 1) * GL)
            gens.append(chunk_terms(c, gi, r_t[:, gl], a_t[:, gl], b_t[:, gl], k_t[:, gl],
                                    b_e[:, gl], k_e[:, gl], v_c[:, gl], p_end[:, gl]))
        if (c + 1) % RWKV_INTERLEAVE_CHUNKS == 0 or c + 1 == n_chunks:
            _run_interleaved(gens)
            gens = []

    states = [state_ref[gi] for gi in range(H // GH)]
    for c in range(n_chunks):
        rows = slice(c * C, (c + 1) * C)
        for gi in range(H // GH):
            gl = slice(gi * GL, (gi + 1) * GL)
            res = _dot(jnp.concatenate([q_scr[c, gi], m_scr[c, gi]], axis=0),
                       block_diag(states[gi]))
            yh_scr[rows, gl] = res[:C] + y0_scr[c, gi]
            states[gi] = res[C:] + n_scr[c, gi]
    for gi in range(H // GH):
        state_ref[gi] = states[gi]

    outs = []
    for gi in range(H // GH):
        gl = slice(gi * GL, (gi + 1) * GL)
        y_g = yh_scr[:, gl]
        yc = y_g - head_sum(y_g) * (1.0 / N)
        var = head_sum(yc * yc) * (1.0 / N)
        bonus = head_sum(rkr[:, gl])
        outs.append(yc * lax.rsqrt(var + RWKV_GN_EPS) * gng_ref[:, gl] + gnb_ref[:, gl]
                    + bonus * v[:, gl])
    y_ref[0] = (jnp.concatenate(outs, axis=-1) * g).astype(y_ref.dtype)


def _rwkv_mix(h, pre_g, w_proj, mu, w0, w_up, a0, a_up, g_up, k_k, k_a, r_k, gn_g, gn_b, *,
              ts):
    B, S, D = h.shape
    C, H, N, W = RWKV_CHUNK, RWKV_HEADS, RWKV_HEAD_DIM, RWKV_WIDTH
    n_chunks = ts // C
    NG, GL = H // RWKV_HEAD_GROUP, RWKV_HEAD_GROUP * N
    row = lambda t: t.reshape(1, -1).astype(F32)
    full = lambda shape: pl.BlockSpec(shape, lambda b, i: (0,) * len(shape),
                                      pipeline_mode=pl.Buffered(1))
    return pl.pallas_call(
        _rwkv_kernel,
        out_shape=jax.ShapeDtypeStruct((B, S, W), BF16),
        grid=(B, S // ts),
        in_specs=[
            pl.BlockSpec((1, ts, D), lambda b, i: (b, i, 0)),
            full((1, D)), full((D, RWKV_COLS)),
            full((1, RWKV_COLS)), full((1, W)), full((DECAY_LORA, W)), full((1, W)),
            full((AAA_LORA, W)), full((GATE_LORA, W)), full((1, W)), full((1, W)),
            full((1, W)), full((1, W)), full((1, W)),
        ],
        out_specs=pl.BlockSpec((1, ts, W), lambda b, i: (b, i, 0)),
        scratch_shapes=[
            pltpu.VMEM((1, RWKV_COLS), F32),
            pltpu.VMEM((NG, N, GL), F32),
            pltpu.VMEM((n_chunks, NG, C, GL), BF16),
            pltpu.VMEM((n_chunks, NG, C, GL), F32),
            pltpu.VMEM((n_chunks, NG, N, GL), BF16),
            pltpu.VMEM((n_chunks, NG, N, GL), F32),
            pltpu.VMEM((ts, W), F32),
        ],
        compiler_params=pltpu.CompilerParams(
            dimension_semantics=("arbitrary", "arbitrary"),
            vmem_limit_bytes=VMEM_LIMIT_BYTES),
        name="rwkv7_mix",
    )(h, row(pre_g), w_proj, row(mu), row(w0), w_up.astype(BF16), row(a0), a_up.astype(BF16),
      g_up.astype(BF16), row(k_k), row(k_a), row(r_k), row(gn_g), row(gn_b))


D_MODEL = 1024
D_FF = 2816
FFN_CHUNK = 256


def _const_spec(shape):
    return pl.BlockSpec(shape, lambda *_: (0,) * len(shape), pipeline_mode=pl.Buffered(1))


def _swiglu_half_step(h, pre_g, w_in_ref, w_out_ref, post_g):
    xn = _rms_norm(h, pre_g).astype(BF16)
    acc = jnp.zeros(h.shape, F32)
    for c in range(D_FF // FFN_CHUNK):
        lo = c * FFN_CHUNK
        gate = jnp.dot(xn, w_in_ref[:, lo:lo + FFN_CHUNK], preferred_element_type=F32)
        up = jnp.dot(xn, w_in_ref[:, D_FF + lo:D_FF + lo + FFN_CHUNK],
                     preferred_element_type=F32)
        act = (gate * _sigmoid(gate) * up).astype(BF16)
        acc = acc + jnp.dot(act, w_out_ref[lo:lo + FFN_CHUNK, :], preferred_element_type=F32)
    return h + HALF_STEP * _rms_norm(acc, post_g)


def _ffn_kernel(h_ref, pre_g_ref, w_in_ref, w_out_ref, post_g_ref, o_ref):
    o_ref[...] = _swiglu_half_step(h_ref[...], pre_g_ref[...], w_in_ref, w_out_ref,
                                   post_g_ref[...])


def _ffn(h, pre_g, w_in, w_out, post_g, *, tm):
    T, D = h.shape
    tile = pl.BlockSpec((tm, D), lambda i: (i, 0))
    return pl.pallas_call(
        _ffn_kernel,
        out_shape=jax.ShapeDtypeStruct((T, D), F32),
        grid=(T // tm,),
        in_specs=[tile, _const_spec((1, D)), _const_spec((D, 2 * D_FF)),
                  _const_spec((D_FF, D)), _const_spec((1, D))],
        out_specs=tile,
        compiler_params=pltpu.CompilerParams(
            dimension_semantics=("arbitrary",), vmem_limit_bytes=VMEM_LIMIT_BYTES),
        name="swiglu_half_step",
    )(h, pre_g.reshape(1, D), w_in.astype(BF16), w_out.astype(BF16), post_g.reshape(1, D))


ATTN_COLS = SWA_COLS + MEM_WIDTH
BRANCH_COLS = RWKV_COLS + ATTN_COLS

def _rope_tables(pos_row):
    half = ROPE_DIM // 2
    f = lax.broadcasted_iota(jnp.int32, (half, 1), 0).astype(F32)
    inv_freq = jnp.exp(f * (-2.0 / ROPE_DIM * math.log(ROPE_THETA)))
    ang = inv_freq * pos_row
    cs = jnp.concatenate([jnp.cos(ang), jnp.sin(ang)], axis=0)
    src = lax.broadcasted_iota(jnp.int32, (2 * half, 256), 0)
    col = lax.broadcasted_iota(jnp.int32, (2 * half, 256), 1)
    d = col % SWA_HEAD_DIM
    is_sin = col >= 128
    want = jnp.where(is_sin, half, 0) + d % half
    sign = jnp.where(is_sin & (d < half), -1.0, 1.0)
    sel = jnp.where((src == want) & (d < ROPE_DIM), sign, 0.0).astype(BF16)
    hi = cs.astype(BF16)
    lo = (cs - hi.astype(F32)).astype(BF16)
    tab = _dot_tn(hi, sel) + _dot_tn(lo, sel)
    lane_d = lax.broadcasted_iota(jnp.int32, (1, 128), 1) % SWA_HEAD_DIM
    c = tab[:, :128] + jnp.where(lane_d < ROPE_DIM, 0.0, 1.0)
    return c, tab[:, 128:]


def _apply_rope(x, c, s):
    half = ROPE_DIM // 2
    low = (lax.broadcasted_iota(jnp.int32, (1, 128), 1) % SWA_HEAD_DIM) < half
    cols = []
    for j in range(x.shape[1] // 128):
        xj = x[:, j * 128:(j + 1) * 128]
        partner = jnp.where(low, pltpu.roll(xj, 128 - half, axis=1),
                            pltpu.roll(xj, half, axis=1))
        cols.append(xj * c + partner * s)
    return cols[0] if len(cols) == 1 else jnp.concatenate(cols, axis=1)


def _kv_lane_variants(x):
    hd = SWA_HEAD_DIM
    low = lax.broadcasted_iota(jnp.int32, (1, 128), 1) < hd
    xs = pltpu.roll(x, hd, axis=1)
    return [[jnp.where(low, x, 0.0).astype(BF16), jnp.where(low, 0.0, xs).astype(BF16)],
            [jnp.where(low, xs, 0.0).astype(BF16), jnp.where(low, 0.0, x).astype(BF16)]]


def _swa_attention(q, k, v, kc_ref, vc_ref, sinks_ref, first_tile):
    tq = q.shape[0]
    blk = WINDOW
    nb = tq // blk
    kv = _kv_lane_variants(k)
    vv = _kv_lane_variants(v)
    qi = lax.broadcasted_iota(jnp.int32, (blk, blk), 0)
    kj = lax.broadcasted_iota(jnp.int32, (blk, blk), 1)
    from_prev = kj > qi
    prev_bias = jnp.where(first_tile, NEG_INF, 0.0)

    def window(var, ref, g, e, j):
        prev = ref[2 * g + e] if j == 0 else var[g][e][(j - 1) * blk:j * blk]
        return [prev, var[g][e][j * blk:(j + 1) * blk]]

    pair_outs = [[None] * (2 * SWA_KV_HEADS) for _ in range(nb)]

    def block_group(j, g):
        rs = slice(j * blk, (j + 1) * blk)
        k_rhs = jnp.concatenate(window(kv, kc_ref, g, 0, j) + window(kv, kc_ref, g, 1, j),
                                axis=0)
        v_rhs = jnp.concatenate(window(vv, vc_ref, g, 0, j) + window(vv, vc_ref, g, 1, j),
                                axis=0)
        q_rows = jnp.concatenate([q[rs, (2 * g + r) * 128:(2 * g + r + 1) * 128]
                                  for r in range(2)], axis=0)
        s_all = _dot_nt(q_rows, k_rhs)
        yield
        probs = []
        for r in range(2):
            row_p = []
            for e in range(2):
                sink = sinks_ref[SWA_GROUP * g + 2 * r + e]
                s_prev = s_all[r * blk:(r + 1) * blk, 2 * e * blk:(2 * e + 1) * blk]
                s_cur = s_all[r * blk:(r + 1) * blk, (2 * e + 1) * blk:(2 * e + 2) * blk]
                if j == 0:
                    s_prev = s_prev + prev_bias
                s = jnp.where(from_prev, s_prev, s_cur)
                m = jnp.maximum(jnp.max(s, axis=-1, keepdims=True), sink)
                ex = jnp.exp(s - m)
                denom = jnp.sum(ex, axis=-1, keepdims=True) + jnp.exp(sink - m)
                pn = ex * (1.0 / denom)
                row_p += [jnp.where(from_prev, pn, 0.0).astype(BF16),
                          jnp.where(from_prev, 0.0, pn).astype(BF16)]
            probs.append(jnp.concatenate(row_p, axis=1))
            yield
        o = _dot(jnp.concatenate(probs, axis=0), v_rhs)
        pair_outs[j][2 * g] = o[:blk]
        pair_outs[j][2 * g + 1] = o[blk:]

    _run_interleaved([block_group(j, g) for j in range(nb) for g in range(SWA_KV_HEADS)])
    outs = [jnp.concatenate(po, axis=1) for po in pair_outs]
    for g in range(SWA_KV_HEADS):
        for e in range(2):
            kc_ref[2 * g + e] = kv[g][e][tq - blk:tq]
            vc_ref[2 * g + e] = vv[g][e][tq - blk:tq]
    return jnp.concatenate(outs, axis=0)


def _attn_kernel(sinks_ref, h_ref, pos_ref, g_ref, w_ref, km_ref, vm_ref, yb_ref, yc_ref,
                 kc_ref, vc_ref):
    first_tile = pl.program_id(1) == 0

    @pl.when(first_tile)
    def _():
        kc_ref[...] = jnp.zeros_like(kc_ref)
        vc_ref[...] = jnp.zeros_like(vc_ref)

    u = _rms_norm(h_ref[0], g_ref[...]).astype(BF16)
    p = jnp.dot(u, w_ref[...], preferred_element_type=F32)
    c, s = _rope_tables(pos_ref[0].astype(F32))
    q = (_apply_rope(p[:, 0:SWA_Q_WIDTH], c, s) * (SWA_HEAD_DIM ** -0.5)).astype(BF16)
    k = _apply_rope(p[:, SWA_Q_WIDTH:SWA_Q_WIDTH + SWA_KV_WIDTH], c, s)
    v = p[:, SWA_Q_WIDTH + SWA_KV_WIDTH:SWA_COLS]
    yb_ref[0] = _swa_attention(q, k, v, kc_ref, vc_ref, sinks_ref,
                               first_tile).astype(yb_ref.dtype)
    yc_ref[0] = _mem_attention(p[:, SWA_COLS:ATTN_COLS], km_ref[0],
                               vm_ref[0]).astype(yc_ref.dtype)


def _attn_branches(h, positions, sinks, g, w_attn, km, vm, *, tq):
    B, S, D = h.shape
    M = km.shape[1]
    n_carry = 2 * SWA_KV_HEADS
    const = lambda shape: pl.BlockSpec(shape, lambda b, i, *_: (0,) * len(shape),
                                       pipeline_mode=pl.Buffered(1))
    return pl.pallas_call(
        _attn_kernel,
        out_shape=(jax.ShapeDtypeStruct((B, S, SWA_Q_WIDTH), BF16),
                   jax.ShapeDtypeStruct((B, S, MEM_WIDTH), BF16)),
        grid_spec=pltpu.PrefetchScalarGridSpec(
            num_scalar_prefetch=1,
            grid=(B, S // tq),
            in_specs=[pl.BlockSpec((1, tq, D), lambda b, i, *_: (b, i, 0)),
                      pl.BlockSpec((1, 1, tq), lambda b, i, *_: (b, 0, i)),
                      const((1, D)), const((D, ATTN_COLS)),
                      pl.BlockSpec((1, M, MEM_WIDTH), lambda b, i, *_: (b, 0, 0)),
                      pl.BlockSpec((1, M, MEM_WIDTH), lambda b, i, *_: (b, 0, 0))],
            out_specs=(pl.BlockSpec((1, tq, SWA_Q_WIDTH), lambda b, i, *_: (b, i, 0)),
                       pl.BlockSpec((1, tq, MEM_WIDTH), lambda b, i, *_: (b, i, 0))),
            scratch_shapes=[pltpu.VMEM((n_carry, WINDOW, 128), BF16),
                            pltpu.VMEM((n_carry, WINDOW, 128), BF16)]),
        compiler_params=pltpu.CompilerParams(
            dimension_semantics=("arbitrary", "arbitrary"),
            vmem_limit_bytes=VMEM_LIMIT_BYTES),
        name="attn_branches",
    )(sinks.astype(F32), h, positions.reshape(B, 1, S), g.reshape(1, D), w_attn, km, vm)


def _mem_kv_kernel(mem_ref, g_ref, w_ref, km_ref, vm_ref):
    mn = _rms_norm(mem_ref[0], g_ref[...]).astype(BF16)
    km_ref[0] = jnp.dot(mn, w_ref[:, 0:MEM_WIDTH],
                        preferred_element_type=F32).astype(km_ref.dtype)
    vm_ref[0] = jnp.dot(mn, w_ref[:, MEM_WIDTH:2 * MEM_WIDTH],
                        preferred_element_type=F32).astype(vm_ref.dtype)


def _mem_kv(mem, g, w_kv):
    B, M, D = mem.shape
    out = jax.ShapeDtypeStruct((B, M, MEM_WIDTH), BF16)
    return pl.pallas_call(
        _mem_kv_kernel,
        out_shape=(out, out),
        grid=(B,),
        in_specs=[pl.BlockSpec((1, M, D), lambda b: (b, 0, 0)), _const_spec((1, D)),
                  _const_spec((D, 2 * MEM_WIDTH))],
        out_specs=(pl.BlockSpec((1, M, MEM_WIDTH), lambda b: (b, 0, 0)),
                   pl.BlockSpec((1, M, MEM_WIDTH), lambda b: (b, 0, 0))),
        compiler_params=pltpu.CompilerParams(
            dimension_semantics=("arbitrary",), vmem_limit_bytes=VMEM_LIMIT_BYTES),
        name="mem_kv_proj",
    )(mem, g.reshape(1, D), w_kv.astype(BF16))


def _mem_attention(q, km, vm):
    hd = MEM_HEAD_DIM
    q = (q * (hd ** -0.5)).astype(BF16)
    outs = []
    for h in range(MEM_HEADS):
        hs = slice(h * hd, (h + 1) * hd)
        s = _dot_nt(q[:, hs], km[:, hs])
        e = jnp.exp(s - jnp.max(s, axis=-1, keepdims=True))
        outs.append(_dot(e, vm[:, hs]) / jnp.sum(e, axis=-1, keepdims=True))
    return jnp.concatenate(outs, axis=1)


def _merge_kernel(h_ref, ya_ref, yb_ref, yc_ref, pre_g_ref, wg_ref, gb_ref, wb_ref, wo_ref,
                  post_g_ref, o_ref):
    D = D_MODEL
    h = h_ref[...]
    u = _rms_norm(h, pre_g_ref[...]).astype(BF16)
    merged = jnp.zeros(h.shape, F32)
    for i, y_ref in enumerate((ya_ref, yb_ref, yc_ref)):
        logits = jnp.dot(u, wg_ref[:, i * D:(i + 1) * D], preferred_element_type=F32)
        gate = _sigmoid(logits + gb_ref[:, i * D:(i + 1) * D])
        merged = merged + gate * jnp.dot(y_ref[...], wb_ref[i], preferred_element_type=F32)
    out = jnp.dot(merged.astype(BF16), wo_ref[...], preferred_element_type=F32)
    o_ref[...] = h + _rms_norm(out, post_g_ref[...])


def _merge(h, y_a, y_b, y_c, pre_g, w_gate, gate_b, w_branch, w_out, post_g, *, tm):
    T, D = h.shape
    tile = pl.BlockSpec((tm, D), lambda i: (i, 0))
    ytile = pl.BlockSpec((tm, BRANCH_WIDTH), lambda i: (i, 0))
    return pl.pallas_call(
        _merge_kernel,
        out_shape=jax.ShapeDtypeStruct((T, D), F32),
        grid=(T // tm,),
        in_specs=[tile, ytile, ytile, ytile, _const_spec((1, D)),
                  _const_spec((D, N_BRANCH * D)), _const_spec((1, N_BRANCH * D)),
                  _const_spec((N_BRANCH, BRANCH_WIDTH, D)), _const_spec((D, D)),
                  _const_spec((1, D))],
        out_specs=tile,
        compiler_params=pltpu.CompilerParams(
            dimension_semantics=("arbitrary",), vmem_limit_bytes=VMEM_LIMIT_BYTES),
        name="gated_merge",
    )(h, y_a, y_b, y_c, pre_g.reshape(1, D), w_gate, gate_b.reshape(1, N_BRANCH * D),
      w_branch.astype(BF16), w_out.astype(BF16), post_g.reshape(1, D))


def kernel(x, mem, positions, ffn1_pre_g, ffn1_w_in, ffn1_w_out, ffn1_post_g, mix_pre_g, w_in, gate_b, rwkv_mu, rwkv_w0, rwkv_w_up, rwkv_a0, rwkv_a_up, rwkv_g_up, rwkv_k_k, rwkv_k_a, rwkv_r_k, rwkv_gn_g, rwkv_gn_b, swa_sinks, mem_norm_g, mem_w_kv, w_branch, w_out, mix_post_g, ffn2_pre_g, ffn2_w_in, ffn2_w_out, ffn2_post_g):
    B, S, D = x.shape
    T = B * S
    tm = min(512, T)
    ts = min(512, S)
    h = x.reshape(T, D)
    for l in range(ffn1_pre_g.shape[0]):
        h = _ffn(h, ffn1_pre_g[l], ffn1_w_in[l], ffn1_w_out[l], ffn1_post_g[l], tm=tm)

        w_in_bf = w_in[l].astype(BF16)
        h3 = h.reshape(B, S, D)
        y_a = _rwkv_mix(h3, mix_pre_g[l], w_in_bf[:, :RWKV_COLS], rwkv_mu[l], rwkv_w0[l],
                        rwkv_w_up[l], rwkv_a0[l], rwkv_a_up[l], rwkv_g_up[l], rwkv_k_k[l],
                        rwkv_k_a[l], rwkv_r_k[l].reshape(-1), rwkv_gn_g[l], rwkv_gn_b[l], ts=ts)
        km, vm = _mem_kv(mem, mem_norm_g[l], mem_w_kv[l])
        y_b, y_c = _attn_branches(h3, positions, swa_sinks[l], mix_pre_g[l],
                                  w_in_bf[:, RWKV_COLS:BRANCH_COLS], km, vm, tq=ts)
        h = _merge(h, y_a.reshape(T, -1), y_b.reshape(T, -1), y_c.reshape(T, -1), mix_pre_g[l],
                   w_in_bf[:, BRANCH_COLS:], gate_b[l], w_branch[l], w_out[l], mix_post_g[l],
                   tm=tm)

        h = _ffn(h, ffn2_pre_g[l], ffn2_w_in[l], ffn2_w_out[l], ffn2_post_g[l], tm=tm)
    return h.reshape(B, S, D)
```

```python
import functools
import math

import jax
import jax.numpy as jnp
from jax import lax
from jax.experimental import pallas as pl
from jax.experimental.pallas import tpu as pltpu

F32 = jnp.float32
BF16 = jnp.bfloat16

NORM_EPS = 1e-6
HALF_STEP = 0.5

RWKV_HEADS = 8
RWKV_HEAD_DIM = 64
RWKV_WIDTH = RWKV_HEADS * RWKV_HEAD_DIM
DECAY_LORA = 64
AAA_LORA = 64
GATE_LORA = 128
RWKV_COLS = 3 * RWKV_WIDTH + DECAY_LORA + AAA_LORA + GATE_LORA
RWKV_GN_EPS = 64e-5
RWKV_CHUNK = 64
RWKV_HEAD_GROUP = 4
assert RWKV_CHUNK == RWKV_HEAD_DIM and RWKV_HEADS % RWKV_HEAD_GROUP == 0
RWKV_CHUNK_STAGGER = 2

SWA_Q_HEADS = 8
SWA_KV_HEADS = 2
SWA_GROUP = SWA_Q_HEADS // SWA_KV_HEADS
SWA_HEAD_DIM = 64
SWA_Q_WIDTH = SWA_Q_HEADS * SWA_HEAD_DIM
SWA_KV_WIDTH = SWA_KV_HEADS * SWA_HEAD_DIM
SWA_COLS = SWA_Q_WIDTH + 2 * SWA_KV_WIDTH
WINDOW = 128
ROPE_THETA = 500000.0
ROPE_DIM = SWA_HEAD_DIM // 4
NEG_INF = -1e30

MEM_HEADS = 4
MEM_HEAD_DIM = 128
MEM_WIDTH = MEM_HEADS * MEM_HEAD_DIM

N_BRANCH = 3
BRANCH_WIDTH = 512

VMEM_LIMIT_BYTES = 56 * 1024 * 1024


def _dot(a, b):
    return jnp.dot(a.astype(BF16), b.astype(BF16), preferred_element_type=F32)


def _dot_nt(a, b):
    return lax.dot_general(a.astype(BF16), b.astype(BF16), (((1,), (1,)), ((), ())),
                           preferred_element_type=F32)


def _dot_tn(a, b):
    return lax.dot_general(a.astype(BF16), b.astype(BF16), (((0,), (0,)), ((), ())),
                           preferred_element_type=F32)


def _split3(x):
    hi = x.astype(BF16)
    r1 = x - hi.astype(F32)
    mid = r1.astype(BF16)
    lo = (r1 - mid.astype(F32)).astype(BF16)
    return hi, mid, lo


def _rms_norm(x, g):
    return x * lax.rsqrt(jnp.mean(x * x, axis=-1, keepdims=True) + NORM_EPS) * g


def _sigmoid(x):
    return 1.0 / (1.0 + jnp.exp(-x))


def _run_interleaved(gens, stagger=0):
    waiting = list(gens)
    active = []
    rnd = 0
    while waiting or active:
        if waiting and (stagger == 0 or rnd % stagger == 0):
            n_new = len(waiting) if stagger == 0 else 1
            active += waiting[:n_new]
            waiting = waiting[n_new:]
        active = [gen for gen in active if next(gen, True) is None]
        rnd += 1


def _rwkv_kernel(h_ref, pre_g_ref, w_ref, mu_ref, w0_ref, wup_ref, a0_ref, aup_ref, gup_ref,
                 kk_ref, ka_ref, rk_ref, gng_ref, gnb_ref, y_ref, carry_ref, state_ref, q_scr,
                 y0_scr, m_scr, n_scr, yh_scr):
    C, H, N, GH = RWKV_CHUNK, RWKV_HEADS, RWKV_HEAD_DIM, RWKV_HEAD_GROUP
    W = RWKV_WIDTH
    ts = h_ref.shape[1]
    n_chunks = ts // C

    @pl.when(pl.program_id(1) == 0)
    def _():
        carry_ref[...] = jnp.zeros_like(carry_ref)
        state_ref[...] = jnp.zeros_like(state_ref)

    u = _rms_norm(h_ref[0], pre_g_ref[...]).astype(BF16)
    p = jnp.dot(u, w_ref[...], preferred_element_type=F32)
    row = lax.broadcasted_iota(jnp.int32, p.shape, 0)
    p_prev = jnp.where(row == 0, carry_ref[...], pltpu.roll(p, 1, axis=0))
    carry_ref[...] = p[ts - 1:ts, :]
    xs = p + (p_prev - p) * mu_ref[...]

    r = xs[:, 0:W]
    k = xs[:, W:2 * W]
    v = xs[:, 2 * W:3 * W]
    o = 3 * W
    wd = xs[:, o:o + DECAY_LORA]
    ad = xs[:, o + DECAY_LORA:o + DECAY_LORA + AAA_LORA]
    gd = xs[:, o + DECAY_LORA + AAA_LORA:o + DECAY_LORA + AAA_LORA + GATE_LORA]

    z = w0_ref[...] + _dot(jnp.tanh(wd), wup_ref[...])
    lw = -math.exp(-0.5) * _sigmoid(z)
    a = _sigmoid(a0_ref[...] + _dot(ad, aup_ref[...]))
    g = _dot(_sigmoid(gd), gup_ref[...])
    kk_raw = k * kk_ref[...]
    k2 = k * (1.0 + (a - 1.0) * ka_ref[...])
    rkr = r * k2 * rk_ref[...]

    GL = GH * N
    row_blk = lax.broadcasted_iota(jnp.int32, (GL, GL), 0) // N
    col_blk = lax.broadcasted_iota(jnp.int32, (GL, GL), 1) // N
    same_head = (row_blk == col_blk).astype(BF16)
    ti = lax.broadcasted_iota(jnp.int32, (C, GL), 0)
    si = lax.broadcasted_iota(jnp.int32, (C, GL), 1) % N
    cat_incl = ti >= si
    cat_strict = ti > si
    cat_eye = ti == si
    ci = lax.broadcasted_iota(jnp.int32, (C, C), 0)
    cj = lax.broadcasted_iota(jnp.int32, (C, C), 1)
    tri_f = (ci >= cj).astype(BF16)

    def block_diag(x):
        xb = x.astype(BF16)
        return jnp.concatenate([xb] * GH, axis=0) * same_head

    def head_sum(x):
        return jnp.dot(x.astype(BF16), same_head, preferred_element_type=F32)

    kk_parts = []
    for gi in range(H // GH):
        gl = slice(gi * GL, (gi + 1) * GL)
        kk_g = kk_raw[:, gl]
        kk_parts.append(kk_g * lax.rsqrt(jnp.maximum(head_sum(kk_g * kk_g), 1e-24)))
    kk = jnp.concatenate(kk_parts, axis=1)
    bvec = kk * a

    def chunk_terms(c, gi, r_t, a_t, b_t, k_t, v_c, p_end):
        bd_b = block_diag(b_t)
        bd_k = block_diag(k_t)
        bd_v = block_diag(v_c)
        diag_p = jnp.where(cat_eye, p_end, 0.0)
        gram = _dot_nt(jnp.concatenate([a_t, r_t, diag_p], axis=0),
                       jnp.concatenate([bd_b, bd_k], axis=0))
        yield
        a_ab = jnp.where(cat_strict, gram[:C, :GL], 0.0)
        a_ak = jnp.where(cat_strict, gram[:C, GL:], 0.0)
        a_rb = jnp.where(cat_incl, gram[C:2 * C, :GL], 0.0)
        a_rk = jnp.where(cat_incl, gram[C:2 * C, GL:], 0.0)
        bt_end = gram[2 * C:, :GL]
        kt_end = gram[2 * C:, GL:]
        x = jnp.where(cat_eye, 1.0, a_ab)
        pw = _dot(a_ab, block_diag(a_ab))
        xv = _dot(jnp.concatenate([a_ak, a_rk, kt_end], axis=0), bd_v)
        akv, y0, n0 = xv[:C], xv[C:2 * C], xv[2 * C:]
        yield
        for _ in range(int(math.log2(C)) - 2):
            sq = _dot(jnp.concatenate([pw, x], axis=0), block_diag(pw))
            pw = sq[:C]
            x = x + sq[C:]
            yield
        x = x + _dot(x, block_diag(pw))
        yield
        lt = _dot(jnp.concatenate([a_rb, bt_end], axis=0), block_diag(x))
        yield
        qymn = _dot(lt, jnp.concatenate([block_diag(a_t), block_diag(akv)], axis=1))
        q_scr[c, gi] = (r_t + qymn[:C, :GL]).astype(BF16)
        y0_scr[c, gi] = qymn[:C, GL:] + y0
        m_scr[c, gi] = (diag_p + qymn[C:, :GL]).astype(BF16)
        n_scr[c, gi] = qymn[C:, GL:] + n0

    def chunk_gen(c):
        rows = slice(c * C, (c + 1) * C)
        lw_c = lw[rows]
        hi, mid, lo = _split3(lw_c)
        cum = (jnp.dot(tri_f, hi, preferred_element_type=F32)
               + jnp.dot(tri_f, mid, preferred_element_type=F32)
               + jnp.dot(tri_f, lo, preferred_element_type=F32))
        e_pos = jnp.exp(cum)
        e_neg = jnp.exp(-cum)
        e_prev = jnp.exp(cum - lw_c)
        p_end = jnp.exp(cum[C - 1:C, :])
        r_t = r[rows] * e_pos
        a_t = -kk[rows] * e_prev
        b_t = bvec[rows] * e_neg
        k_t = k2[rows] * e_neg
        v_c = v[rows]
        yield
        subs = []
        for gi in range(H // GH):
            gl = slice(gi * GL, (gi + 1) * GL)
            subs.append(chunk_terms(c, gi, r_t[:, gl], a_t[:, gl], b_t[:, gl], k_t[:, gl],
                                    v_c[:, gl], p_end[:, gl]))
        while subs:
            subs = [sub for sub in subs if next(sub, True) is None]
            yield

    _run_interleaved([chunk_gen(c) for c in range(n_chunks)], stagger=RWKV_CHUNK_STAGGER)

    states = [state_ref[gi] for gi in range(H // GH)]
    for c in range(n_chunks):
        rows = slice(c * C, (c + 1) * C)
        for gi in range(H // GH):
            gl = slice(gi * GL, (gi + 1) * GL)
            res = _dot(jnp.concatenate([q_scr[c, gi], m_scr[c, gi]], axis=0),
                       block_diag(states[gi]))
            yh_scr[rows, gl] = res[:C] + y0_scr[c, gi]
            states[gi] = res[C:] + n_scr[c, gi]
    for gi in range(H // GH):
        state_ref[gi] = states[gi]

    outs = []
    for gi in range(H // GH):
        gl = slice(gi * GL, (gi + 1) * GL)
        y_g = yh_scr[:, gl]
        yc = y_g - head_sum(y_g) * (1.0 / N)
        var = head_sum(yc * yc) * (1.0 / N)
        bonus = head_sum(rkr[:, gl])
        outs.append(yc * lax.rsqrt(var + RWKV_GN_EPS) * gng_ref[:, gl] + gnb_ref[:, gl]
                    + bonus * v[:, gl])
    y_ref[0] = (jnp.concatenate(outs, axis=-1) * g).astype(y_ref.dtype)


def _rwkv_mix(h, pre_g, w_proj, mu, w0, w_up, a0, a_up, g_up, k_k, k_a, r_k, gn_g, gn_b, *,
              ts):
    B, S, D = h.shape
    C, H, N, W = RWKV_CHUNK, RWKV_HEADS, RWKV_HEAD_DIM, RWKV_WIDTH
    n_chunks = ts // C
    NG, GL = H // RWKV_HEAD_GROUP, RWKV_HEAD_GROUP * N
    row = lambda t: t.reshape(1, -1).astype(F32)
    full = lambda shape: pl.BlockSpec(shape, lambda b, i: (0,) * len(shape),
                                      pipeline_mode=pl.Buffered(1))
    return pl.pallas_call(
        _rwkv_kernel,
        out_shape=jax.ShapeDtypeStruct((B, S, W), BF16),
        grid=(B, S // ts),
        in_specs=[
            pl.BlockSpec((1, ts, D), lambda b, i: (b, i, 0)),
            full((1, D)), full((D, RWKV_COLS)),
            full((1, RWKV_COLS)), full((1, W)), full((DECAY_LORA, W)), full((1, W)),
            full((AAA_LORA, W)), full((GATE_LORA, W)), full((1, W)), full((1, W)),
            full((1, W)), full((1, W)), full((1, W)),
        ],
        out_specs=pl.BlockSpec((1, ts, W), lambda b, i: (b, i, 0)),
        scratch_shapes=[
            pltpu.VMEM((1, RWKV_COLS), F32),
            pltpu.VMEM((NG, N, GL), F32),
            pltpu.VMEM((n_chunks, NG, C, GL), BF16),
            pltpu.VMEM((n_chunks, NG, C, GL), F32),
            pltpu.VMEM((n_chunks, NG, N, GL), BF16),
            pltpu.VMEM((n_chunks, NG, N, GL), F32),
            pltpu.VMEM((ts, W), F32),
        ],
        compiler_params=pltpu.CompilerParams(
            dimension_semantics=("arbitrary", "arbitrary"),
            vmem_limit_bytes=VMEM_LIMIT_BYTES),
        name="rwkv7_mix",
    )(h, row(pre_g), w_proj, row(mu), row(w0), w_up.astype(BF16), row(a0), a_up.astype(BF16),
      g_up.astype(BF16), row(k_k), row(k_a), row(r_k), row(gn_g), row(gn_b))


D_MODEL = 1024
D_FF = 2816
FFN_CHUNK = 256


def _const_spec(shape):
    return pl.BlockSpec(shape, lambda *_: (0,) * len(shape), pipeline_mode=pl.Buffered(1))


def _swiglu_half_step(h, pre_g, w_in_ref, w_out_ref, post_g):
    xn = _rms_norm(h, pre_g).astype(BF16)
    acc = jnp.zeros(h.shape, F32)
    for c in range(D_FF // FFN_CHUNK):
        lo = c * FFN_CHUNK
        gate = jnp.dot(xn, w_in_ref[:, lo:lo + FFN_CHUNK], preferred_element_type=F32)
        up = jnp.dot(xn, w_in_ref[:, D_FF + lo:D_FF + lo + FFN_CHUNK],
                     preferred_element_type=F32)
        act = (gate * _sigmoid(gate) * up).astype(BF16)
        acc = acc + jnp.dot(act, w_out_ref[lo:lo + FFN_CHUNK, :], preferred_element_type=F32)
    return h + HALF_STEP * _rms_norm(acc, post_g)


def _ffn_kernel(h_ref, pre_g_ref, w_in_ref, w_out_ref, post_g_ref, o_ref):
    o_ref[...] = _swiglu_half_step(h_ref[...], pre_g_ref[...], w_in_ref, w_out_ref,
                                   post_g_ref[...])


def _ffn(h, pre_g, w_in, w_out, post_g, *, tm):
    T, D = h.shape
    tile = pl.BlockSpec((tm, D), lambda i: (i, 0))
    return pl.pallas_call(
        _ffn_kernel,
        out_shape=jax.ShapeDtypeStruct((T, D), F32),
        grid=(T // tm,),
        in_specs=[tile, _const_spec((1, D)), _const_spec((D, 2 * D_FF)),
                  _const_spec((D_FF, D)), _const_spec((1, D))],
        out_specs=tile,
        compiler_params=pltpu.CompilerParams(
            dimension_semantics=("arbitrary",), vmem_limit_bytes=VMEM_LIMIT_BYTES),
        name="swiglu_half_step",
    )(h, pre_g.reshape(1, D), w_in.astype(BF16), w_out.astype(BF16), post_g.reshape(1, D))


ATTN_COLS = SWA_COLS + MEM_WIDTH
BRANCH_COLS = RWKV_COLS + ATTN_COLS

def _rope_tables(pos_row):
    half = ROPE_DIM // 2
    f = lax.broadcasted_iota(jnp.int32, (half, 1), 0).astype(F32)
    inv_freq = jnp.exp(f * (-2.0 / ROPE_DIM * math.log(ROPE_THETA)))
    ang = inv_freq * pos_row
    cs = jnp.concatenate([jnp.cos(ang), jnp.sin(ang)], axis=0)
    src = lax.broadcasted_iota(jnp.int32, (2 * half, 256), 0)
    col = lax.broadcasted_iota(jnp.int32, (2 * half, 256), 1)
    d = col % SWA_HEAD_DIM
    is_sin = col >= 128
    want = jnp.where(is_sin, half, 0) + d % half
    sign = jnp.where(is_sin & (d < half), -1.0, 1.0)
    sel = jnp.where((src == want) & (d < ROPE_DIM), sign, 0.0).astype(BF16)
    hi = cs.astype(BF16)
    lo = (cs - hi.astype(F32)).astype(BF16)
    tab = _dot_tn(hi, sel) + _dot_tn(lo, sel)
    lane_d = lax.broadcasted_iota(jnp.int32, (1, 128), 1) % SWA_HEAD_DIM
    c = tab[:, :128] + jnp.where(lane_d < ROPE_DIM, 0.0, 1.0)
    return c, tab[:, 128:]


def _apply_rope(x, c, s):
    half = ROPE_DIM // 2
    low = (lax.broadcasted_iota(jnp.int32, (1, 128), 1) % SWA_HEAD_DIM) < half
    cols = []
    for j in range(x.shape[1] // 128):
        xj = x[:, j * 128:(j + 1) * 128]
        partner = jnp.where(low, pltpu.roll(xj, 128 - half, axis=1),
                            pltpu.roll(xj, half, axis=1))
        cols.append(xj * c + partner * s)
    return cols[0] if len(cols) == 1 else jnp.concatenate(cols, axis=1)


def _kv_lane_variants(x):
    hd = SWA_HEAD_DIM
    low = lax.broadcasted_iota(jnp.int32, (1, 128), 1) < hd
    xs = pltpu.roll(x, hd, axis=1)
    return [[jnp.where(low, x, 0.0).astype(BF16), jnp.where(low, 0.0, xs).astype(BF16)],
            [jnp.where(low, xs, 0.0).astype(BF16), jnp.where(low, 0.0, x).astype(BF16)]]


def _swa_attention(q, k, v, kc_ref, vc_ref, sinks_ref, first_tile):
    tq = q.shape[0]
    blk = WINDOW
    nb = tq // blk
    kv = _kv_lane_variants(k)
    vv = _kv_lane_variants(v)
    qi = lax.broadcasted_iota(jnp.int32, (blk, blk), 0)
    kj = lax.broadcasted_iota(jnp.int32, (blk, blk), 1)
    from_prev = kj > qi
    prev_bias = jnp.where(first_tile, NEG_INF, 0.0)

    def window(var, ref, g, e, j):
        prev = ref[2 * g + e] if j == 0 else var[g][e][(j - 1) * blk:j * blk]
        return [prev, var[g][e][j * blk:(j + 1) * blk]]

    pair_outs = [[None] * (2 * SWA_KV_HEADS) for _ in range(nb)]

    def block_group(j, g):
        rs = slice(j * blk, (j + 1) * blk)
        k_rhs = jnp.concatenate(window(kv, kc_ref, g, 0, j) + window(kv, kc_ref, g, 1, j),
                                axis=0)
        v_rhs = jnp.concatenate(window(vv, vc_ref, g, 0, j) + window(vv, vc_ref, g, 1, j),
                                axis=0)
        q_rows = jnp.concatenate([q[rs, (2 * g + r) * 128:(2 * g + r + 1) * 128]
                                  for r in range(2)], axis=0)
        s_all = _dot_nt(q_rows, k_rhs)
        yield
        probs = []
        for r in range(2):
            row_p = []
            for e in range(2):
                sink = sinks_ref[SWA_GROUP * g + 2 * r + e]
                s_prev = s_all[r * blk:(r + 1) * blk, 2 * e * blk:(2 * e + 1) * blk]
                s_cur = s_all[r * blk:(r + 1) * blk, (2 * e + 1) * blk:(2 * e + 2) * blk]
                if j == 0:
                    s_prev = s_prev + prev_bias
                s = jnp.where(from_prev, s_prev, s_cur)
                m = jnp.maximum(jnp.max(s, axis=-1, keepdims=True), sink)
                ex = jnp.exp(s - m)
                denom = jnp.sum(ex, axis=-1, keepdims=True) + jnp.exp(sink - m)
                pn = ex * (1.0 / denom)
                row_p += [jnp.where(from_prev, pn, 0.0).astype(BF16),
                          jnp.where(from_prev, 0.0, pn).astype(BF16)]
            probs.append(jnp.concatenate(row_p, axis=1))
            yield
        o = _dot(jnp.concatenate(probs, axis=0), v_rhs)
        pair_outs[j][2 * g] = o[:blk]
        pair_outs[j][2 * g + 1] = o[blk:]

    _run_interleaved([block_group(j, g) for j in range(nb) for g in range(SWA_KV_HEADS)])
    outs = [jnp.concatenate(po, axis=1) for po in pair_outs]
    for g in range(SWA_KV_HEADS):
        for e in range(2):
            kc_ref[2 * g + e] = kv[g][e][tq - blk:tq]
            vc_ref[2 * g + e] = vv[g][e][tq - blk:tq]
    return jnp.concatenate(outs, axis=0)


def _attn_kernel(sinks_ref, h_ref, pos_ref, g_ref, w_ref, km_ref, vm_ref, yb_ref, yc_ref,
                 kc_ref, vc_ref):
    first_tile = pl.program_id(1) == 0

    @pl.when(first_tile)
    def _():
        kc_ref[...] = jnp.zeros_like(kc_ref)
        vc_ref[...] = jnp.zeros_like(vc_ref)

    u = _rms_norm(h_ref[0], g_ref[...]).astype(BF16)
    p = jnp.dot(u, w_ref[...], preferred_element_type=F32)
    c, s = _rope_tables(pos_ref[0].astype(F32))
    q = (_apply_rope(p[:, 0:SWA_Q_WIDTH], c, s) * (SWA_HEAD_DIM ** -0.5)).astype(BF16)
    k = _apply_rope(p[:, SWA_Q_WIDTH:SWA_Q_WIDTH + SWA_KV_WIDTH], c, s)
    v = p[:, SWA_Q_WIDTH + SWA_KV_WIDTH:SWA_COLS]
    yb_ref[0] = _swa_attention(q, k, v, kc_ref, vc_ref, sinks_ref,
                               first_tile).astype(yb_ref.dtype)
    yc_ref[0] = _mem_attention(p[:, SWA_COLS:ATTN_COLS], km_ref[0],
                               vm_ref[0]).astype(yc_ref.dtype)


def _attn_branches(h, positions, sinks, g, w_attn, km, vm, *, tq):
    B, S, D = h.shape
    M = km.shape[1]
    n_carry = 2 * SWA_KV_HEADS
    const = lambda shape: pl.BlockSpec(shape, lambda b, i, *_: (0,) * len(shape),
                                       pipeline_mode=pl.Buffered(1))
    return pl.pallas_call(
        _attn_kernel,
        out_shape=(jax.ShapeDtypeStruct((B, S, SWA_Q_WIDTH), BF16),
                   jax.ShapeDtypeStruct((B, S, MEM_WIDTH), BF16)),
        grid_spec=pltpu.PrefetchScalarGridSpec(
            num_scalar_prefetch=1,
            grid=(B, S // tq),
            in_specs=[pl.BlockSpec((1, tq, D), lambda b, i, *_: (b, i, 0)),
                      pl.BlockSpec((1, 1, tq), lambda b, i, *_: (b, 0, i)),
                      const((1, D)), const((D, ATTN_COLS)),
                      pl.BlockSpec((1, M, MEM_WIDTH), lambda b, i, *_: (b, 0, 0)),
                      pl.BlockSpec((1, M, MEM_WIDTH), lambda b, i, *_: (b, 0, 0))],
            out_specs=(pl.BlockSpec((1, tq, SWA_Q_WIDTH), lambda b, i, *_: (b, i, 0)),
                       pl.BlockSpec((1, tq, MEM_WIDTH), lambda b, i, *_: (b, i, 0))),
            scratch_shapes=[pltpu.VMEM((n_carry, WINDOW, 128), BF16),
                            pltpu.VMEM((n_carry, WINDOW, 128), BF16)]),
        compiler_params=pltpu.CompilerParams(
            dimension_semantics=("arbitrary", "arbitrary"),
            vmem_limit_bytes=VMEM_LIMIT_BYTES),
        name="attn_branches",
    )(sinks.astype(F32), h, positions.reshape(B, 1, S), g.reshape(1, D), w_attn, km, vm)


def _mem_kv_kernel(mem_ref, g_ref, w_ref, km_ref, vm_ref):
    mn = _rms_norm(mem_ref[0], g_ref[...]).astype(BF16)
    km_ref[0] = jnp.dot(mn, w_ref[:, 0:MEM_WIDTH],
                        preferred_element_type=F32).astype(km_ref.dtype)
    vm_ref[0] = jnp.dot(mn, w_ref[:, MEM_WIDTH:2 * MEM_WIDTH],
                        preferred_element_type=F32).astype(vm_ref.dtype)


def _mem_kv(mem, g, w_kv):
    B, M, D = mem.shape
    out = jax.ShapeDtypeStruct((B, M, MEM_WIDTH), BF16)
    return pl.pallas_call(
        _mem_kv_kernel,
        out_shape=(out, out),
        grid=(B,),
        in_specs=[pl.BlockSpec((1, M, D), lambda b: (b, 0, 0)), _const_spec((1, D)),
                  _const_spec((D, 2 * MEM_WIDTH))],
        out_specs=(pl.BlockSpec((1, M, MEM_WIDTH), lambda b: (b, 0, 0)),
                   pl.BlockSpec((1, M, MEM_WIDTH), lambda b: (b, 0, 0))),
        compiler_params=pltpu.CompilerParams(
            dimension_semantics=("arbitrary",), vmem_limit_bytes=VMEM_LIMIT_BYTES),
        name="mem_kv_proj",
    )(mem, g.reshape(1, D), w_kv.astype(BF16))


def _mem_attention(q, km, vm):
    hd = MEM_HEAD_DIM
    q = (q * (hd ** -0.5)).astype(BF16)
    outs = []
    for h in range(MEM_HEADS):
        hs = slice(h * hd, (h + 1) * hd)
        s = _dot_nt(q[:, hs], km[:, hs])
        e = jnp.exp(s - jnp.max(s, axis=-1, keepdims=True))
        outs.append(_dot(e, vm[:, hs]) / jnp.sum(e, axis=-1, keepdims=True))
    return jnp.concatenate(outs, axis=1)


def _merge_kernel(h_ref, ya_ref, yb_ref, yc_ref, pre_g_ref, wg_ref, gb_ref, wb_ref, wo_ref,
                  post_g_ref, o_ref):
    D = D_MODEL
    h = h_ref[...]
    u = _rms_norm(h, pre_g_ref[...]).astype(BF16)
    merged = jnp.zeros(h.shape, F32)
    for i, y_ref in enumerate((ya_ref, yb_ref, yc_ref)):
        logits = jnp.dot(u, wg_ref[:, i * D:(i + 1) * D], preferred_element_type=F32)
        gate = _sigmoid(logits + gb_ref[:, i * D:(i + 1) * D])
        merged = merged + gate * jnp.dot(y_ref[...], wb_ref[i], preferred_element_type=F32)
    out = jnp.dot(merged.astype(BF16), wo_ref[...], preferred_element_type=F32)
    o_ref[...] = h + _rms_norm(out, post_g_ref[...])


def _merge(h, y_a, y_b, y_c, pre_g, w_gate, gate_b, w_branch, w_out, post_g, *, tm):
    T, D = h.shape
    tile = pl.BlockSpec((tm, D), lambda i: (i, 0))
    ytile = pl.BlockSpec((tm, BRANCH_WIDTH), lambda i: (i, 0))
    return pl.pallas_call(
        _merge_kernel,
        out_shape=jax.ShapeDtypeStruct((T, D), F32),
        grid=(T // tm,),
        in_specs=[tile, ytile, ytile, ytile, _const_spec((1, D)),
                  _const_spec((D, N_BRANCH * D)), _const_spec((1, N_BRANCH * D)),
                  _const_spec((N_BRANCH, BRANCH_WIDTH, D)), _const_spec((D, D)),
                  _const_spec((1, D))],
        out_specs=tile,
        compiler_params=pltpu.CompilerParams(
            dimension_semantics=("arbitrary",), vmem_limit_bytes=VMEM_LIMIT_BYTES),
        name="gated_merge",
    )(h, y_a, y_b, y_c, pre_g.reshape(1, D), w_gate, gate_b.reshape(1, N_BRANCH * D),
      w_branch.astype(BF16), w_out.astype(BF16), post_g.reshape(1, D))


def kernel(x, mem, positions, ffn1_pre_g, ffn1_w_in, ffn1_w_out, ffn1_post_g, mix_pre_g, w_in, gate_b, rwkv_mu, rwkv_w0, rwkv_w_up, rwkv_a0, rwkv_a_up, rwkv_g_up, rwkv_k_k, rwkv_k_a, rwkv_r_k, rwkv_gn_g, rwkv_gn_b, swa_sinks, mem_norm_g, mem_w_kv, w_branch, w_out, mix_post_g, ffn2_pre_g, ffn2_w_in, ffn2_w_out, ffn2_post_g):
    B, S, D = x.shape
    T = B * S
    tm = min(1024, T)
    ts = min(512, S)
    h = x.reshape(T, D)
    for l in range(ffn1_pre_g.shape[0]):
        h = _ffn(h, ffn1_pre_g[l], ffn1_w_in[l], ffn1_w_out[l], ffn1_post_g[l], tm=tm)

        w_in_bf = w_in[l].astype(BF16)
        h3 = h.reshape(B, S, D)
        y_a = _rwkv_mix(h3, mix_pre_g[l], w_in_bf[:, :RWKV_COLS], rwkv_mu[l], rwkv_w0[l],
                        rwkv_w_up[l], rwkv_a0[l], rwkv_a_up[l], rwkv_g_up[l], rwkv_k_k[l],
                        rwkv_k_a[l], rwkv_r_k[l].reshape(-1), rwkv_gn_g[l], rwkv_gn_b[l], ts=ts)
        km, vm = _mem_kv(mem, mem_norm_g[l], mem_w_kv[l])
        y_b, y_c = _attn_branches(h3, positions, swa_sinks[l], mix_pre_g[l],
                                  w_in_bf[:, RWKV_COLS:BRANCH_COLS], km, vm, tq=ts)
        h = _merge(h, y_a.reshape(T, -1), y_b.reshape(T, -1), y_c.reshape(T, -1), mix_pre_g[l],
                   w_in_bf[:, BRANCH_COLS:], gate_b[l], w_branch[l], w_out[l], mix_post_g[l],
                   tm=tm)

        h = _ffn(h, ffn2_pre_g[l], ffn2_w_in[l], ffn2_w_out[l], ffn2_post_g[l], tm=tm)
    return h.reshape(B, S, D)
```

```python
import functools
import math

import jax
import jax.numpy as jnp
from jax import lax
from jax.experimental import pallas as pl
from jax.experimental.pallas import tpu as pltpu

F32 = jnp.float32
BF16 = jnp.bfloat16

NORM_EPS = 1e-6
HALF_STEP = 0.5

RWKV_HEADS = 8
RWKV_HEAD_DIM = 64
RWKV_WIDTH = RWKV_HEADS * RWKV_HEAD_DIM
DECAY_LORA = 64
AAA_LORA = 64
GATE_LORA = 128
RWKV_COLS = 3 * RWKV_WIDTH + DECAY_LORA + AAA_LORA + GATE_LORA
RWKV_GN_EPS = 64e-5
RWKV_CHUNK = 64
RWKV_HEAD_GROUP = 4
assert RWKV_CHUNK == RWKV_HEAD_DIM and RWKV_HEADS % RWKV_HEAD_GROUP == 0
RWKV_CHUNK_STAGGER = 2

SWA_Q_HEADS = 8
SWA_KV_HEADS = 2
SWA_GROUP = SWA_Q_HEADS // SWA_KV_HEADS
SWA_HEAD_DIM = 64
SWA_Q_WIDTH = SWA_Q_HEADS * SWA_HEAD_DIM
SWA_KV_WIDTH = SWA_KV_HEADS * SWA_HEAD_DIM
SWA_COLS = SWA_Q_WIDTH + 2 * SWA_KV_WIDTH
WINDOW = 128
ROPE_THETA = 500000.0
ROPE_DIM = SWA_HEAD_DIM // 4
NEG_INF = -1e30

MEM_HEADS = 4
MEM_HEAD_DIM = 128
MEM_WIDTH = MEM_HEADS * MEM_HEAD_DIM

N_BRANCH = 3
BRANCH_WIDTH = 512

VMEM_LIMIT_BYTES = 56 * 1024 * 1024


def _dot(a, b):
    return jnp.dot(a.astype(BF16), b.astype(BF16), preferred_element_type=F32)


def _dot_nt(a, b):
    return lax.dot_general(a.astype(BF16), b.astype(BF16), (((1,), (1,)), ((), ())),
                           preferred_element_type=F32)


def _dot_tn(a, b):
    return lax.dot_general(a.astype(BF16), b.astype(BF16), (((0,), (0,)), ((), ())),
                           preferred_element_type=F32)


def _split3(x):
    hi = x.astype(BF16)
    r1 = x - hi.astype(F32)
    mid = r1.astype(BF16)
    lo = (r1 - mid.astype(F32)).astype(BF16)
    return hi, mid, lo


def _rms_norm(x, g):
    return x * lax.rsqrt(jnp.mean(x * x, axis=-1, keepdims=True) + NORM_EPS) * g


def _sigmoid(x):
    return 1.0 / (1.0 + jnp.exp(-x))


def _run_interleaved(gens, stagger=0):
    waiting = list(gens)
    active = []
    rnd = 0
    while waiting or active:
        if waiting and (stagger == 0 or rnd % stagger == 0):
            n_new = len(waiting) if stagger == 0 else 1
            active += waiting[:n_new]
            waiting = waiting[n_new:]
        active = [gen for gen in active if next(gen, True) is None]
        rnd += 1


def _rwkv_kernel(h_ref, pre_g_ref, w_ref, mu_ref, w0_ref, wup_ref, a0_ref, aup_ref, gup_ref,
                 kk_ref, ka_ref, rk_ref, gng_ref, gnb_ref, y_ref, carry_ref, state_ref, q_scr,
                 y0_scr, m_scr, n_scr, yh_scr):
    C, H, N, GH = RWKV_CHUNK, RWKV_HEADS, RWKV_HEAD_DIM, RWKV_HEAD_GROUP
    W = RWKV_WIDTH
    ts = h_ref.shape[1]
    n_chunks = ts // C

    @pl.when(pl.program_id(1) == 0)
    def _():
        carry_ref[...] = jnp.zeros_like(carry_ref)
        state_ref[...] = jnp.zeros_like(state_ref)

    u = _rms_norm(h_ref[0], pre_g_ref[...]).astype(BF16)
    p = jnp.dot(u, w_ref[...], preferred_element_type=F32)
    row = lax.broadcasted_iota(jnp.int32, p.shape, 0)
    p_prev = jnp.where(row == 0, carry_ref[...], pltpu.roll(p, 1, axis=0))
    carry_ref[...] = p[ts - 1:ts, :]
    xs = p + (p_prev - p) * mu_ref[...]

    r = xs[:, 0:W]
    k = xs[:, W:2 * W]
    v = xs[:, 2 * W:3 * W]
    o = 3 * W
    wd = xs[:, o:o + DECAY_LORA]
    ad = xs[:, o + DECAY_LORA:o + DECAY_LORA + AAA_LORA]
    gd = xs[:, o + DECAY_LORA + AAA_LORA:o + DECAY_LORA + AAA_LORA + GATE_LORA]

    z = w0_ref[...] + _dot(jnp.tanh(wd), wup_ref[...])
    lw = -math.exp(-0.5) * _sigmoid(z)
    a = _sigmoid(a0_ref[...] + _dot(ad, aup_ref[...]))
    g = _dot(_sigmoid(gd), gup_ref[...])
    kk_raw = k * kk_ref[...]
    k2 = k * (1.0 + (a - 1.0) * ka_ref[...])
    rkr = r * k2 * rk_ref[...]

    GL = GH * N
    row_blk = lax.broadcasted_iota(jnp.int32, (GL, GL), 0) // N
    col_blk = lax.broadcasted_iota(jnp.int32, (GL, GL), 1) // N
    same_head = (row_blk == col_blk).astype(BF16)
    ti = lax.broadcasted_iota(jnp.int32, (C, GL), 0)
    si = lax.broadcasted_iota(jnp.int32, (C, GL), 1) % N
    cat_incl = ti >= si
    cat_strict = ti > si
    cat_eye = ti == si
    ci = lax.broadcasted_iota(jnp.int32, (C, C), 0)
    cj = lax.broadcasted_iota(jnp.int32, (C, C), 1)
    tri_f = (ci >= cj).astype(BF16)

    def block_diag(x):
        xb = x.astype(BF16)
        return jnp.concatenate([xb] * GH, axis=0) * same_head

    def head_sum(x):
        return jnp.dot(x.astype(BF16), same_head, preferred_element_type=F32)

    kk_parts = []
    for gi in range(H // GH):
        gl = slice(gi * GL, (gi + 1) * GL)
        kk_g = kk_raw[:, gl]
        kk_parts.append(kk_g * lax.rsqrt(jnp.maximum(head_sum(kk_g * kk_g), 1e-24)))
    kk = jnp.concatenate(kk_parts, axis=1)
    bvec = kk * a

    def chunk_terms(c, gi, r_t, a_t, b_t, k_t, v_c, p_end):
        bd_b = block_diag(b_t)
        bd_k = block_diag(k_t)
        bd_v = block_diag(v_c)
        diag_p = jnp.where(cat_eye, p_end, 0.0)
        gram = _dot_nt(jnp.concatenate([a_t, r_t, diag_p], axis=0),
                       jnp.concatenate([bd_b, bd_k], axis=0))
        yield
        a_ab = jnp.where(cat_strict, gram[:C, :GL], 0.0)
        a_ak = jnp.where(cat_strict, gram[:C, GL:], 0.0)
        a_rb = jnp.where(cat_incl, gram[C:2 * C, :GL], 0.0)
        a_rk = jnp.where(cat_incl, gram[C:2 * C, GL:], 0.0)
        bt_end = gram[2 * C:, :GL]
        kt_end = gram[2 * C:, GL:]
        x = jnp.where(cat_eye, 1.0, a_ab)
        pw = _dot(a_ab, block_diag(a_ab))
        xv = _dot(jnp.concatenate([a_ak, a_rk, kt_end], axis=0), bd_v)
        akv, y0, n0 = xv[:C], xv[C:2 * C], xv[2 * C:]
        yield
        for _ in range(int(math.log2(C)) - 2):
            sq = _dot(jnp.concatenate([pw, x], axis=0), block_diag(pw))
            pw = sq[:C]
            x = x + sq[C:]
            yield
        x = x + _dot(x, block_diag(pw))
        yield
        lt = _dot(jnp.concatenate([a_rb, bt_end], axis=0), block_diag(x))
        yield
        qymn = _dot(lt, jnp.concatenate([block_diag(a_t), block_diag(akv)], axis=1))
        q_scr[c, gi] = (r_t + qymn[:C, :GL]).astype(BF16)
        y0_scr[c, gi] = qymn[:C, GL:] + y0
        m_scr[c, gi] = (diag_p + qymn[C:, :GL]).astype(BF16)
        n_scr[c, gi] = qymn[C:, GL:] + n0

    def chunk_gen(c):
        rows = slice(c * C, (c + 1) * C)
        lw_c = lw[rows]
        hi, mid, lo = _split3(lw_c)
        cum = (jnp.dot(tri_f, hi, preferred_element_type=F32)
               + jnp.dot(tri_f, mid, preferred_element_type=F32)
               + jnp.dot(tri_f, lo, preferred_element_type=F32))
        e_pos = jnp.exp(cum)
        e_neg = jnp.exp(-cum)
        e_prev = jnp.exp(cum - lw_c)
        p_end = jnp.exp(cum[C - 1:C, :])
        r_t = r[rows] * e_pos
        a_t = -kk[rows] * e_prev
        b_t = bvec[rows] * e_neg
        k_t = k2[rows] * e_neg
        v_c = v[rows]
        yield
        subs = []
        for gi in range(H // GH):
            gl = slice(gi * GL, (gi + 1) * GL)
            subs.append(chunk_terms(c, gi, r_t[:, gl], a_t[:, gl], b_t[:, gl], k_t[:, gl],
                                    v_c[:, gl], p_end[:, gl]))
        while subs:
            subs = [sub for sub in subs if next(sub, True) is None]
            yield
        chunk_done[c] = True

    chunk_done = [False] * n_chunks
    states = [state_ref[gi] for gi in range(H // GH)]

    def scan_gen():
        for c in range(n_chunks):
            while not chunk_done[c]:
                yield
            rows = slice(c * C, (c + 1) * C)
            for gi in range(H // GH):
                gl = slice(gi * GL, (gi + 1) * GL)
                res = _dot(jnp.concatenate([q_scr[c, gi], m_scr[c, gi]], axis=0),
                           block_diag(states[gi]))
                yh_scr[rows, gl] = res[:C] + y0_scr[c, gi]
                states[gi] = res[C:] + n_scr[c, gi]
            yield

    _run_interleaved([scan_gen()] + [chunk_gen(c) for c in range(n_chunks)],
                     stagger=RWKV_CHUNK_STAGGER)
    for gi in range(H // GH):
        state_ref[gi] = states[gi]

    outs = []
    for gi in range(H // GH):
        gl = slice(gi * GL, (gi + 1) * GL)
        y_g = yh_scr[:, gl]
        yc = y_g - head_sum(y_g) * (1.0 / N)
        var = head_sum(yc * yc) * (1.0 / N)
        bonus = head_sum(rkr[:, gl])
        outs.append(yc * lax.rsqrt(var + RWKV_GN_EPS) * gng_ref[:, gl] + gnb_ref[:, gl]
                    + bonus * v[:, gl])
    y_ref[0] = (jnp.concatenate(outs, axis=-1) * g).astype(y_ref.dtype)


def _rwkv_mix(h, pre_g, w_proj, mu, w0, w_up, a0, a_up, g_up, k_k, k_a, r_k, gn_g, gn_b, *,
              ts):
    B, S, D = h.shape
    C, H, N, W = RWKV_CHUNK, RWKV_HEADS, RWKV_HEAD_DIM, RWKV_WIDTH
    n_chunks = ts // C
    NG, GL = H // RWKV_HEAD_GROUP, RWKV_HEAD_GROUP * N
    row = lambda t: t.reshape(1, -1).astype(F32)
    full = lambda shape: pl.BlockSpec(shape, lambda b, i: (0,) * len(shape),
                                      pipeline_mode=pl.Buffered(1))
    return pl.pallas_call(
        _rwkv_kernel,
        out_shape=jax.ShapeDtypeStruct((B, S, W), BF16),
        grid=(B, S // ts),
        in_specs=[
            pl.BlockSpec((1, ts, D), lambda b, i: (b, i, 0)),
            full((1, D)), full((D, RWKV_COLS)),
            full((1, RWKV_COLS)), full((1, W)), full((DECAY_LORA, W)), full((1, W)),
            full((AAA_LORA, W)), full((GATE_LORA, W)), full((1, W)), full((1, W)),
            full((1, W)), full((1, W)), full((1, W)),
        ],
        out_specs=pl.BlockSpec((1, ts, W), lambda b, i: (b, i, 0)),
        scratch_shapes=[
            pltpu.VMEM((1, RWKV_COLS), F32),
            pltpu.VMEM((NG, N, GL), F32),
            pltpu.VMEM((n_chunks, NG, C, GL), BF16),
            pltpu.VMEM((n_chunks, NG, C, GL), F32),
            pltpu.VMEM((n_chunks, NG, N, GL), BF16),
            pltpu.VMEM((n_chunks, NG, N, GL), F32),
            pltpu.VMEM((ts, W), F32),
        ],
        compiler_params=pltpu.CompilerParams(
            dimension_semantics=("arbitrary", "arbitrary"),
            vmem_limit_bytes=VMEM_LIMIT_BYTES),
        name="rwkv7_mix",
    )(h, row(pre_g), w_proj, row(mu), row(w0), w_up.astype(BF16), row(a0), a_up.astype(BF16),
      g_up.astype(BF16), row(k_k), row(k_a), row(r_k), row(gn_g), row(gn_b))


D_MODEL = 1024
D_FF = 2816
FFN_CHUNK = 256
FFN_SUBTILES = 2
FFN_STAGGER = 9


def _const_spec(shape):
    return pl.BlockSpec(shape, lambda *_: (0,) * len(shape), pipeline_mode=pl.Buffered(1))


def _swiglu_half_step(h_ref, pre_g, w_in_ref, w_out_ref, post_g, o_ref, rows):
    h = h_ref[rows, :]
    xn = _rms_norm(h, pre_g).astype(BF16)
    acc = jnp.zeros(h.shape, F32)
    yield
    for c in range(D_FF // FFN_CHUNK):
        lo = c * FFN_CHUNK
        gate = jnp.dot(xn, w_in_ref[:, lo:lo + FFN_CHUNK], preferred_element_type=F32)
        up = jnp.dot(xn, w_in_ref[:, D_FF + lo:D_FF + lo + FFN_CHUNK],
                     preferred_element_type=F32)
        act = (gate * _sigmoid(gate) * up).astype(BF16)
        acc = acc + jnp.dot(act, w_out_ref[lo:lo + FFN_CHUNK, :], preferred_element_type=F32)
        yield
    o_ref[rows, :] = h + HALF_STEP * _rms_norm(acc, post_g)


def _swiglu_tile(h_ref, pre_g, w_in_ref, w_out_ref, post_g, o_ref):
    sub = h_ref.shape[0] // FFN_SUBTILES
    _run_interleaved(
        [_swiglu_half_step(h_ref, pre_g, w_in_ref, w_out_ref, post_g, o_ref,
                           slice(s * sub, (s + 1) * sub)) for s in range(FFN_SUBTILES)],
        stagger=FFN_STAGGER)


def _ffn_kernel(h_ref, pre_g_ref, w_in_ref, w_out_ref, post_g_ref, o_ref):
    _swiglu_tile(h_ref, pre_g_ref[...], w_in_ref, w_out_ref, post_g_ref[...], o_ref)


def _ffn(h, pre_g, w_in, w_out, post_g, *, tm):
    T, D = h.shape
    tile = pl.BlockSpec((tm, D), lambda i: (i, 0))
    return pl.pallas_call(
        _ffn_kernel,
        out_shape=jax.ShapeDtypeStruct((T, D), F32),
        grid=(T // tm,),
        in_specs=[tile, _const_spec((1, D)), _const_spec((D, 2 * D_FF)),
                  _const_spec((D_FF, D)), _const_spec((1, D))],
        out_specs=tile,
        compiler_params=pltpu.CompilerParams(
            dimension_semantics=("arbitrary",), vmem_limit_bytes=VMEM_LIMIT_BYTES),
        name="swiglu_half_step",
    )(h, pre_g.reshape(1, D), w_in.astype(BF16), w_out.astype(BF16), post_g.reshape(1, D))


ATTN_COLS = SWA_COLS + MEM_WIDTH
BRANCH_COLS = RWKV_COLS + ATTN_COLS

def _rope_tables(pos_row):
    half = ROPE_DIM // 2
    f = lax.broadcasted_iota(jnp.int32, (half, 1), 0).astype(F32)
    inv_freq = jnp.exp(f * (-2.0 / ROPE_DIM * math.log(ROPE_THETA)))
    ang = inv_freq * pos_row
    cs = jnp.concatenate([jnp.cos(ang), jnp.sin(ang)], axis=0)
    src = lax.broadcasted_iota(jnp.int32, (2 * half, 256), 0)
    col = lax.broadcasted_iota(jnp.int32, (2 * half, 256), 1)
    d = col % SWA_HEAD_DIM
    is_sin = col >= 128
    want = jnp.where(is_sin, half, 0) + d % half
    sign = jnp.where(is_sin & (d < half), -1.0, 1.0)
    sel = jnp.where((src == want) & (d < ROPE_DIM), sign, 0.0).astype(BF16)
    hi = cs.astype(BF16)
    lo = (cs - hi.astype(F32)).astype(BF16)
    tab = _dot_tn(hi, sel) + _dot_tn(lo, sel)
    lane_d = lax.broadcasted_iota(jnp.int32, (1, 128), 1) % SWA_HEAD_DIM
    c = tab[:, :128] + jnp.where(lane_d < ROPE_DIM, 0.0, 1.0)
    return c, tab[:, 128:]


def _apply_rope(x, c, s):
    half = ROPE_DIM // 2
    low = (lax.broadcasted_iota(jnp.int32, (1, 128), 1) % SWA_HEAD_DIM) < half
    cols = []
    for j in range(x.shape[1] // 128):
        xj = x[:, j * 128:(j + 1) * 128]
        partner = jnp.where(low, pltpu.roll(xj, 128 - half, axis=1),
                            pltpu.roll(xj, half, axis=1))
        cols.append(xj * c + partner * s)
    return cols[0] if len(cols) == 1 else jnp.concatenate(cols, axis=1)


def _kv_lane_variants(x):
    hd = SWA_HEAD_DIM
    low = lax.broadcasted_iota(jnp.int32, (1, 128), 1) < hd
    xs = pltpu.roll(x, hd, axis=1)
    return [[jnp.where(low, x, 0.0).astype(BF16), jnp.where(low, 0.0, xs).astype(BF16)],
            [jnp.where(low, xs, 0.0).astype(BF16), jnp.where(low, 0.0, x).astype(BF16)]]


def _swa_attention(q, k, v, kc_ref, vc_ref, sinks_ref, first_tile):
    tq = q.shape[0]
    blk = WINDOW
    nb = tq // blk
    kv = _kv_lane_variants(k)
    vv = _kv_lane_variants(v)
    qi = lax.broadcasted_iota(jnp.int32, (blk, blk), 0)
    kj = lax.broadcasted_iota(jnp.int32, (blk, blk), 1)
    from_prev = kj > qi
    prev_bias = jnp.where(first_tile, NEG_INF, 0.0)

    def window(var, ref, g, e, j):
        prev = ref[2 * g + e] if j == 0 else var[g][e][(j - 1) * blk:j * blk]
        return [prev, var[g][e][j * blk:(j + 1) * blk]]

    pair_outs = [[None] * (2 * SWA_KV_HEADS) for _ in range(nb)]

    def block_group(j, g):
        rs = slice(j * blk, (j + 1) * blk)
        k_rhs = jnp.concatenate(window(kv, kc_ref, g, 0, j) + window(kv, kc_ref, g, 1, j),
                                axis=0)
        v_rhs = jnp.concatenate(window(vv, vc_ref, g, 0, j) + window(vv, vc_ref, g, 1, j),
                                axis=0)
        q_rows = jnp.concatenate([q[rs, (2 * g + r) * 128:(2 * g + r + 1) * 128]
                                  for r in range(2)], axis=0)
        s_all = _dot_nt(q_rows, k_rhs)
        yield
        probs = []
        for r in range(2):
            row_p = []
            for e in range(2):
                sink = sinks_ref[SWA_GROUP * g + 2 * r + e]
                s_prev = s_all[r * blk:(r + 1) * blk, 2 * e * blk:(2 * e + 1) * blk]
                s_cur = s_all[r * blk:(r + 1) * blk, (2 * e + 1) * blk:(2 * e + 2) * blk]
                if j == 0:
                    s_prev = s_prev + prev_bias
                s = jnp.where(from_prev, s_prev, s_cur)
                m = jnp.maximum(jnp.max(s, axis=-1, keepdims=True), sink)
                ex = jnp.exp(s - m)
                denom = jnp.sum(ex, axis=-1, keepdims=True) + jnp.exp(sink - m)
                pn = ex * (1.0 / denom)
                row_p += [jnp.where(from_prev, pn, 0.0).astype(BF16),
                          jnp.where(from_prev, 0.0, pn).astype(BF16)]
            probs.append(jnp.concatenate(row_p, axis=1))
            yield
        o = _dot(jnp.concatenate(probs, axis=0), v_rhs)
        pair_outs[j][2 * g] = o[:blk]
        pair_outs[j][2 * g + 1] = o[blk:]

    _run_interleaved([block_group(j, g) for j in range(nb) for g in range(SWA_KV_HEADS)])
    outs = [jnp.concatenate(po, axis=1) for po in pair_outs]
    for g in range(SWA_KV_HEADS):
        for e in range(2):
            kc_ref[2 * g + e] = kv[g][e][tq - blk:tq]
            vc_ref[2 * g + e] = vv[g][e][tq - blk:tq]
    return jnp.concatenate(outs, axis=0)


def _attn_kernel(sinks_ref, h_ref, pos_ref, g_ref, w_ref, km_ref, vm_ref, yb_ref, yc_ref,
                 kc_ref, vc_ref):
    first_tile = pl.program_id(1) == 0

    @pl.when(first_tile)
    def _():
        kc_ref[...] = jnp.zeros_like(kc_ref)
        vc_ref[...] = jnp.zeros_like(vc_ref)

    u = _rms_norm(h_ref[0], g_ref[...]).astype(BF16)
    p = jnp.dot(u, w_ref[...], preferred_element_type=F32)
    c, s = _rope_tables(pos_ref[0].astype(F32))
    q = (_apply_rope(p[:, 0:SWA_Q_WIDTH], c, s) * (SWA_HEAD_DIM ** -0.5)).astype(BF16)
    k = _apply_rope(p[:, SWA_Q_WIDTH:SWA_Q_WIDTH + SWA_KV_WIDTH], c, s)
    v = p[:, SWA_Q_WIDTH + SWA_KV_WIDTH:SWA_COLS]
    yb_ref[0] = _swa_attention(q, k, v, kc_ref, vc_ref, sinks_ref,
                               first_tile).astype(yb_ref.dtype)
    yc_ref[0] = _mem_attention(p[:, SWA_COLS:ATTN_COLS], km_ref[0],
                               vm_ref[0]).astype(yc_ref.dtype)


def _attn_branches(h, positions, sinks, g, w_attn, km, vm, *, tq):
    B, S, D = h.shape
    M = km.shape[1]
    n_carry = 2 * SWA_KV_HEADS
    const = lambda shape: pl.BlockSpec(shape, lambda b, i, *_: (0,) * len(shape),
                                       pipeline_mode=pl.Buffered(1))
    return pl.pallas_call(
        _attn_kernel,
        out_shape=(jax.ShapeDtypeStruct((B, S, SWA_Q_WIDTH), BF16),
                   jax.ShapeDtypeStruct((B, S, MEM_WIDTH), BF16)),
        grid_spec=pltpu.PrefetchScalarGridSpec(
            num_scalar_prefetch=1,
            grid=(B, S // tq),
            in_specs=[pl.BlockSpec((1, tq, D), lambda b, i, *_: (b, i, 0)),
                      pl.BlockSpec((1, 1, tq), lambda b, i, *_: (b, 0, i)),
                      const((1, D)), const((D, ATTN_COLS)),
                      pl.BlockSpec((1, M, MEM_WIDTH), lambda b, i, *_: (b, 0, 0)),
                      pl.BlockSpec((1, M, MEM_WIDTH), lambda b, i, *_: (b, 0, 0))],
            out_specs=(pl.BlockSpec((1, tq, SWA_Q_WIDTH), lambda b, i, *_: (b, i, 0)),
                       pl.BlockSpec((1, tq, MEM_WIDTH), lambda b, i, *_: (b, i, 0))),
            scratch_shapes=[pltpu.VMEM((n_carry, WINDOW, 128), BF16),
                            pltpu.VMEM((n_carry, WINDOW, 128), BF16)]),
        compiler_params=pltpu.CompilerParams(
            dimension_semantics=("arbitrary", "arbitrary"),
            vmem_limit_bytes=VMEM_LIMIT_BYTES),
        name="attn_branches",
    )(sinks.astype(F32), h, positions.reshape(B, 1, S), g.reshape(1, D), w_attn, km, vm)


def _mem_kv_kernel(mem_ref, g_ref, w_ref, km_ref, vm_ref):
    mn = _rms_norm(mem_ref[0], g_ref[...]).astype(BF16)
    km_ref[0] = jnp.dot(mn, w_ref[:, 0:MEM_WIDTH],
                        preferred_element_type=F32).astype(km_ref.dtype)
    vm_ref[0] = jnp.dot(mn, w_ref[:, MEM_WIDTH:2 * MEM_WIDTH],
                        preferred_element_type=F32).astype(vm_ref.dtype)


def _mem_kv(mem, g, w_kv):
    B, M, D = mem.shape
    out = jax.ShapeDtypeStruct((B, M, MEM_WIDTH), BF16)
    return pl.pallas_call(
        _mem_kv_kernel,
        out_shape=(out, out),
        grid=(B,),
        in_specs=[pl.BlockSpec((1, M, D), lambda b: (b, 0, 0)), _const_spec((1, D)),
                  _const_spec((D, 2 * MEM_WIDTH))],
        out_specs=(pl.BlockSpec((1, M, MEM_WIDTH), lambda b: (b, 0, 0)),
                   pl.BlockSpec((1, M, MEM_WIDTH), lambda b: (b, 0, 0))),
        compiler_params=pltpu.CompilerParams(
            dimension_semantics=("arbitrary",), vmem_limit_bytes=VMEM_LIMIT_BYTES),
        name="mem_kv_proj",
    )(mem, g.reshape(1, D), w_kv.astype(BF16))


def _mem_attention(q, km, vm):
    hd = MEM_HEAD_DIM
    q = (q * (hd ** -0.5)).astype(BF16)
    outs = []
    for h in range(MEM_HEADS):
        hs = slice(h * hd, (h + 1) * hd)
        s = _dot_nt(q[:, hs], km[:, hs])
        e = jnp.exp(s - jnp.max(s, axis=-1, keepdims=True))
        outs.append(_dot(e, vm[:, hs]) / jnp.sum(e, axis=-1, keepdims=True))
    return jnp.concatenate(outs, axis=1)


MERGE_SUBTILES = 2
MERGE_STAGGER = 3


def _merge_kernel(h_ref, ya_ref, yb_ref, yc_ref, pre_g_ref, wg_ref, gb_ref, wb_ref, wo_ref,
                  post_g_ref, o_ref):
    D = D_MODEL

    def sub_tile(rows):
        h = h_ref[rows, :]
        u = _rms_norm(h, pre_g_ref[...]).astype(BF16)
        merged = jnp.zeros(h.shape, F32)
        yield
        for i, y_ref in enumerate((ya_ref, yb_ref, yc_ref)):
            logits = jnp.dot(u, wg_ref[:, i * D:(i + 1) * D], preferred_element_type=F32)
            gate = _sigmoid(logits + gb_ref[:, i * D:(i + 1) * D])
            merged = merged + gate * jnp.dot(y_ref[rows, :], wb_ref[i],
                                             preferred_element_type=F32)
            yield
        out = jnp.dot(merged.astype(BF16), wo_ref[...], preferred_element_type=F32)
        yield
        o_ref[rows, :] = h + _rms_norm(out, post_g_ref[...])

    sub = h_ref.shape[0] // MERGE_SUBTILES
    _run_interleaved([sub_tile(slice(s * sub, (s + 1) * sub)) for s in range(MERGE_SUBTILES)],
                     stagger=MERGE_STAGGER)


def _merge(h, y_a, y_b, y_c, pre_g, w_gate, gate_b, w_branch, w_out, post_g, *, tm):
    T, D = h.shape
    tile = pl.BlockSpec((tm, D), lambda i: (i, 0))
    ytile = pl.BlockSpec((tm, BRANCH_WIDTH), lambda i: (i, 0))
    return pl.pallas_call(
        _merge_kernel,
        out_shape=jax.ShapeDtypeStruct((T, D), F32),
        grid=(T // tm,),
        in_specs=[tile, ytile, ytile, ytile, _const_spec((1, D)),
                  _const_spec((D, N_BRANCH * D)), _const_spec((1, N_BRANCH * D)),
                  _const_spec((N_BRANCH, BRANCH_WIDTH, D)), _const_spec((D, D)),
                  _const_spec((1, D))],
        out_specs=tile,
        compiler_params=pltpu.CompilerParams(
            dimension_semantics=("arbitrary",), vmem_limit_bytes=VMEM_LIMIT_BYTES),
        name="gated_merge",
    )(h, y_a, y_b, y_c, pre_g.reshape(1, D), w_gate, gate_b.reshape(1, N_BRANCH * D),
      w_branch.astype(BF16), w_out.astype(BF16), post_g.reshape(1, D))


def kernel(x, mem, positions, ffn1_pre_g, ffn1_w_in, ffn1_w_out, ffn1_post_g, mix_pre_g, w_in, gate_b, rwkv_mu, rwkv_w0, rwkv_w_up, rwkv_a0, rwkv_a_up, rwkv_g_up, rwkv_k_k, rwkv_k_a, rwkv_r_k, rwkv_gn_g, rwkv_gn_b, swa_sinks, mem_norm_g, mem_w_kv, w_branch, w_out, mix_post_g, ffn2_pre_g, ffn2_w_in, ffn2_w_out, ffn2_post_g):
    B, S, D = x.shape
    T = B * S
    tm = min(1024, T)
    ts = min(512, S)
    h = x.reshape(T, D)
    for l in range(ffn1_pre_g.shape[0]):
        h = _ffn(h, ffn1_pre_g[l], ffn1_w_in[l], ffn1_w_out[l], ffn1_post_g[l], tm=tm)

        w_in_bf = w_in[l].astype(BF16)
        h3 = h.reshape(B, S, D)
        y_a = _rwkv_mix(h3, mix_pre_g[l], w_in_bf[:, :RWKV_COLS], rwkv_mu[l], rwkv_w0[l],
                        rwkv_w_up[l], rwkv_a0[l], rwkv_a_up[l], rwkv_g_up[l], rwkv_k_k[l],
                        rwkv_k_a[l], rwkv_r_k[l].reshape(-1), rwkv_gn_g[l], rwkv_gn_b[l], ts=ts)
        km, vm = _mem_kv(mem, mem_norm_g[l], mem_w_kv[l])
        y_b, y_c = _attn_branches(h3, positions, swa_sinks[l], mix_pre_g[l],
                                  w_in_bf[:, RWKV_COLS:BRANCH_COLS], km, vm, tq=ts)
        h = _merge(h, y_a.reshape(T, -1), y_b.reshape(T, -1), y_c.reshape(T, -1), mix_pre_g[l],
                   w_in_bf[:, BRANCH_COLS:], gate_b[l], w_branch[l], w_out[l], mix_post_g[l],
                   tm=tm)

        h = _ffn(h, ffn2_pre_g[l], ffn2_w_in[l], ffn2_w_out[l], ffn2_post_g[l], tm=tm)
    return h.reshape(B, S, D)
```

```python
import functools
import math

import jax
import jax.numpy as jnp
from jax import lax
from jax.experimental import pallas as pl
from jax.experimental.pallas import tpu as pltpu

F32 = jnp.float32
BF16 = jnp.bfloat16

NORM_EPS = 1e-6
HALF_STEP = 0.5

RWKV_HEADS = 8
RWKV_HEAD_DIM = 64
RWKV_WIDTH = RWKV_HEADS * RWKV_HEAD_DIM
DECAY_LORA = 64
AAA_LORA = 64
GATE_LORA = 128
RWKV_COLS = 3 * RWKV_WIDTH + DECAY_LORA + AAA_LORA + GATE_LORA
RWKV_GN_EPS = 64e-5
RWKV_CHUNK = 64
RWKV_HEAD_GROUP = 4
assert RWKV_CHUNK == RWKV_HEAD_DIM and RWKV_HEADS % RWKV_HEAD_GROUP == 0
RWKV_CHUNK_STAGGER = 2

SWA_Q_HEADS = 8
SWA_KV_HEADS = 2
SWA_GROUP = SWA_Q_HEADS // SWA_KV_HEADS
SWA_HEAD_DIM = 64
SWA_Q_WIDTH = SWA_Q_HEADS * SWA_HEAD_DIM
SWA_KV_WIDTH = SWA_KV_HEADS * SWA_HEAD_DIM
SWA_COLS = SWA_Q_WIDTH + 2 * SWA_KV_WIDTH
WINDOW = 128
ROPE_THETA = 500000.0
ROPE_DIM = SWA_HEAD_DIM // 4
NEG_INF = -1e30

MEM_HEADS = 4
MEM_HEAD_DIM = 128
MEM_WIDTH = MEM_HEADS * MEM_HEAD_DIM

N_BRANCH = 3
BRANCH_WIDTH = 512

BF16_SUBLANES = 16
VMEM_LIMIT_BYTES = 56 * 1024 * 1024


def _dot(a, b):
    return jnp.dot(a.astype(BF16), b.astype(BF16), preferred_element_type=F32)


def _dot_nt(a, b):
    return lax.dot_general(a.astype(BF16), b.astype(BF16), (((1,), (1,)), ((), ())),
                           preferred_element_type=F32)


def _dot_tn(a, b):
    return lax.dot_general(a.astype(BF16), b.astype(BF16), (((0,), (0,)), ((), ())),
                           preferred_element_type=F32)


def _split3(x):
    hi = x.astype(BF16)
    r1 = x - hi.astype(F32)
    mid = r1.astype(BF16)
    lo = (r1 - mid.astype(F32)).astype(BF16)
    return hi, mid, lo


def _rms_norm(x, g):
    return x * lax.rsqrt(jnp.mean(x * x, axis=-1, keepdims=True) + NORM_EPS) * g


def _sigmoid(x):
    return 1.0 / (1.0 + jnp.exp(-x))


def _run_interleaved(gens, stagger=0):
    waiting = list(gens)
    active = []
    rnd = 0
    while waiting or active:
        if waiting and (stagger == 0 or rnd % stagger == 0):
            n_new = len(waiting) if stagger == 0 else 1
            active += waiting[:n_new]
            waiting = waiting[n_new:]
        active = [gen for gen in active if next(gen, True) is None]
        rnd += 1


def _rwkv_kernel(h_ref, pre_g_ref, w_ref, mu_ref, w0_ref, wup_ref, a0_ref, aup_ref, gup_ref,
                 kk_ref, ka_ref, rk_ref, gng_ref, gnb_ref, y_ref, carry_ref, state_ref, q_scr,
                 y0_scr, m_scr, n_scr, yh_scr):
    C, H, N, GH = RWKV_CHUNK, RWKV_HEADS, RWKV_HEAD_DIM, RWKV_HEAD_GROUP
    W = RWKV_WIDTH
    ts = h_ref.shape[1]
    n_chunks = ts // C

    @pl.when(pl.program_id(1) == 0)
    def _():
        carry_ref[...] = jnp.zeros_like(carry_ref)
        state_ref[...] = jnp.zeros_like(state_ref)

    u = _rms_norm(h_ref[0], pre_g_ref[...]).astype(BF16)
    p = jnp.dot(u, w_ref[...], preferred_element_type=F32)
    row = lax.broadcasted_iota(jnp.int32, p.shape, 0)
    p_prev = jnp.where(row == 0, carry_ref[...], pltpu.roll(p, 1, axis=0))
    carry_ref[...] = p[ts - 1:ts, :]
    xs = p + (p_prev - p) * mu_ref[...]

    r = xs[:, 0:W]
    k = xs[:, W:2 * W]
    v = xs[:, 2 * W:3 * W]
    o = 3 * W
    wd = xs[:, o:o + DECAY_LORA]
    ad = xs[:, o + DECAY_LORA:o + DECAY_LORA + AAA_LORA]
    gd = xs[:, o + DECAY_LORA + AAA_LORA:o + DECAY_LORA + AAA_LORA + GATE_LORA]

    z = w0_ref[...] + _dot(jnp.tanh(wd), wup_ref[...])
    lw = -math.exp(-0.5) * _sigmoid(z)
    a = _sigmoid(a0_ref[...] + _dot(ad, aup_ref[...]))
    g = _dot(_sigmoid(gd), gup_ref[...])
    kk_raw = k * kk_ref[...]
    k2 = k * (1.0 + (a - 1.0) * ka_ref[...])
    rkr = r * k2 * rk_ref[...]

    GL = GH * N
    row_blk = lax.broadcasted_iota(jnp.int32, (GL, GL), 0) // N
    col_blk = lax.broadcasted_iota(jnp.int32, (GL, GL), 1) // N
    same_head = (row_blk == col_blk).astype(BF16)
    ti = lax.broadcasted_iota(jnp.int32, (C, GL), 0)
    si = lax.broadcasted_iota(jnp.int32, (C, GL), 1) % N
    cat_incl = ti >= si
    cat_strict = ti > si
    cat_eye = ti == si
    ci = lax.broadcasted_iota(jnp.int32, (C, C), 0)
    cj = lax.broadcasted_iota(jnp.int32, (C, C), 1)
    tri_f = (ci >= cj).astype(BF16)

    def block_diag(x):
        xb = x.astype(BF16)
        return jnp.concatenate([xb] * GH, axis=0) * same_head

    def head_sum(x):
        return jnp.dot(x.astype(BF16), same_head, preferred_element_type=F32)

    kk_parts = []
    for gi in range(H // GH):
        gl = slice(gi * GL, (gi + 1) * GL)
        kk_g = kk_raw[:, gl]
        kk_parts.append(kk_g * lax.rsqrt(jnp.maximum(head_sum(kk_g * kk_g), 1e-24)))
    kk = jnp.concatenate(kk_parts, axis=1)
    bvec = kk * a

    def chunk_terms(c, gi, r_t, a_t, b_t, k_t, v_c, p_end):
        bd_b = block_diag(b_t)
        bd_k = block_diag(k_t)
        bd_v = block_diag(v_c)
        diag_p = jnp.where(cat_eye, p_end, 0.0)
        gram = _dot_nt(jnp.concatenate([a_t, r_t, diag_p], axis=0),
                       jnp.concatenate([bd_b, bd_k], axis=0))
        yield
        a_ab = jnp.where(cat_strict, gram[:C, :GL], 0.0)
        a_ak = jnp.where(cat_strict, gram[:C, GL:], 0.0)
        a_rb = jnp.where(cat_incl, gram[C:2 * C, :GL], 0.0)
        a_rk = jnp.where(cat_incl, gram[C:2 * C, GL:], 0.0)
        bt_end = gram[2 * C:, :GL]
        kt_end = gram[2 * C:, GL:]
        x = jnp.where(cat_eye, 1.0, a_ab)
        pw = _dot(a_ab, block_diag(a_ab))
        xv = _dot(jnp.concatenate([a_ak, a_rk, kt_end], axis=0), bd_v)
        akv, y0, n0 = xv[:C], xv[C:2 * C], xv[2 * C:]
        yield
        for _ in range(int(math.log2(C)) - 2):
            sq = _dot(jnp.concatenate([pw, x], axis=0), block_diag(pw))
            pw = sq[:C]
            x = x + sq[C:]
            yield
        x = x + _dot(x, block_diag(pw))
        yield
        lt = _dot(jnp.concatenate([a_rb, bt_end], axis=0), block_diag(x))
        yield
        qymn = _dot(lt, jnp.concatenate([block_diag(a_t), block_diag(akv)], axis=1))
        q_scr[c, gi] = (r_t + qymn[:C, :GL]).astype(BF16)
        y0_scr[c, gi] = qymn[:C, GL:] + y0
        m_scr[c, gi] = (diag_p + qymn[C:, :GL]).astype(BF16)
        n_scr[c, gi] = qymn[C:, GL:] + n0

    def chunk_gen(c):
        rows = slice(c * C, (c + 1) * C)
        lw_c = lw[rows]
        hi, mid, lo = _split3(lw_c)
        cum = (jnp.dot(tri_f, hi, preferred_element_type=F32)
               + jnp.dot(tri_f, mid, preferred_element_type=F32)
               + jnp.dot(tri_f, lo, preferred_element_type=F32))
        e_pos = jnp.exp(cum)
        e_neg = jnp.exp(-cum)
        e_prev = jnp.exp(cum - lw_c)
        p_end = jnp.exp(cum[C - 1:C, :])
        r_t = r[rows] * e_pos
        a_t = -kk[rows] * e_prev
        b_t = bvec[rows] * e_neg
        k_t = k2[rows] * e_neg
        v_c = v[rows]
        yield
        subs = []
        for gi in range(H // GH):
            gl = slice(gi * GL, (gi + 1) * GL)
            subs.append(chunk_terms(c, gi, r_t[:, gl], a_t[:, gl], b_t[:, gl], k_t[:, gl],
                                    v_c[:, gl], p_end[:, gl]))
        while subs:
            subs = [sub for sub in subs if next(sub, True) is None]
            yield
        chunk_done[c] = True

    chunk_done = [False] * n_chunks
    states = [state_ref[gi] for gi in range(H // GH)]

    def scan_gen():
        for c in range(n_chunks):
            while not chunk_done[c]:
                yield
            rows = slice(c * C, (c + 1) * C)
            for gi in range(H // GH):
                gl = slice(gi * GL, (gi + 1) * GL)
                res = _dot(jnp.concatenate([q_scr[c, gi], m_scr[c, gi]], axis=0),
                           block_diag(states[gi]))
                yh_scr[rows, gl] = res[:C] + y0_scr[c, gi]
                states[gi] = res[C:] + n_scr[c, gi]
            yield

    _run_interleaved([scan_gen()] + [chunk_gen(c) for c in range(n_chunks)],
                     stagger=RWKV_CHUNK_STAGGER)
    for gi in range(H // GH):
        state_ref[gi] = states[gi]

    outs = []
    for gi in range(H // GH):
        gl = slice(gi * GL, (gi + 1) * GL)
        y_g = yh_scr[:, gl]
        yc = y_g - head_sum(y_g) * (1.0 / N)
        var = head_sum(yc * yc) * (1.0 / N)
        bonus = head_sum(rkr[:, gl])
        outs.append(yc * lax.rsqrt(var + RWKV_GN_EPS) * gng_ref[:, gl] + gnb_ref[:, gl]
                    + bonus * v[:, gl])
    y_ref[0] = (jnp.concatenate(outs, axis=-1) * g).astype(y_ref.dtype)


def _rwkv_mix(h, pre_g, w_proj, mu, w0, w_up, a0, a_up, g_up, k_k, k_a, r_k, gn_g, gn_b, *,
              ts):
    B, S, D = h.shape
    C, H, N, W = RWKV_CHUNK, RWKV_HEADS, RWKV_HEAD_DIM, RWKV_WIDTH
    n_chunks = ts // C
    NG, GL = H // RWKV_HEAD_GROUP, RWKV_HEAD_GROUP * N
    row = lambda t: t.reshape(1, -1).astype(F32)
    full = lambda shape: pl.BlockSpec(shape, lambda b, i: (0,) * len(shape),
                                      pipeline_mode=pl.Buffered(1))
    return pl.pallas_call(
        _rwkv_kernel,
        out_shape=jax.ShapeDtypeStruct((B, S, W), BF16),
        grid=(B, S // ts),
        in_specs=[
            pl.BlockSpec((1, ts, D), lambda b, i: (b, i, 0)),
            full((1, D)), full((D, RWKV_COLS)),
            full((1, RWKV_COLS)), full((1, W)), full((DECAY_LORA, W)), full((1, W)),
            full((AAA_LORA, W)), full((GATE_LORA, W)), full((1, W)), full((1, W)),
            full((1, W)), full((1, W)), full((1, W)),
        ],
        out_specs=pl.BlockSpec((1, ts, W), lambda b, i: (b, i, 0)),
        scratch_shapes=[
            pltpu.VMEM((1, RWKV_COLS), F32),
            pltpu.VMEM((NG, N, GL), F32),
            pltpu.VMEM((n_chunks, NG, C, GL), BF16),
            pltpu.VMEM((n_chunks, NG, C, GL), F32),
            pltpu.VMEM((n_chunks, NG, N, GL), BF16),
            pltpu.VMEM((n_chunks, NG, N, GL), F32),
            pltpu.VMEM((ts, W), F32),
        ],
        compiler_params=pltpu.CompilerParams(
            dimension_semantics=("arbitrary", "arbitrary"),
            vmem_limit_bytes=VMEM_LIMIT_BYTES),
        name="rwkv7_mix",
    )(h, row(pre_g), w_proj, row(mu), row(w0), w_up.astype(BF16), row(a0), a_up.astype(BF16),
      g_up.astype(BF16), row(k_k), row(k_a), row(r_k), row(gn_g), row(gn_b))


D_MODEL = 1024
D_FF = 2816
FFN_CHUNK = 256
FFN_SUBTILES = 2
FFN_STAGGER = 9


def _const_spec(shape):
    return pl.BlockSpec(shape, lambda *_: (0,) * len(shape), pipeline_mode=pl.Buffered(1))


def _swiglu_half_step(h_ref, pre_g, w_in_ref, w_out_ref, post_g, o_ref, rows):
    h = h_ref[rows, :]
    xn = _rms_norm(h, pre_g).astype(BF16)
    acc = jnp.zeros(h.shape, F32)
    yield
    for c in range(D_FF // FFN_CHUNK):
        lo = c * FFN_CHUNK
        gate = jnp.dot(xn, w_in_ref[:, lo:lo + FFN_CHUNK], preferred_element_type=F32)
        up = jnp.dot(xn, w_in_ref[:, D_FF + lo:D_FF + lo + FFN_CHUNK],
                     preferred_element_type=F32)
        act = (gate * _sigmoid(gate) * up).astype(BF16)
        acc = acc + jnp.dot(act, w_out_ref[lo:lo + FFN_CHUNK, :], preferred_element_type=F32)
        yield
    o_ref[rows, :] = h + HALF_STEP * _rms_norm(acc, post_g)


def _swiglu_tile(h_ref, pre_g, w_in_ref, w_out_ref, post_g, o_ref):
    sub = h_ref.shape[0] // FFN_SUBTILES
    _run_interleaved(
        [_swiglu_half_step(h_ref, pre_g, w_in_ref, w_out_ref, post_g, o_ref,
                           slice(s * sub, (s + 1) * sub)) for s in range(FFN_SUBTILES)],
        stagger=FFN_STAGGER)


def _ffn_kernel(col_splits, h_ref, pre_g_ref, w_in_ref, w_out_ref, post_g_ref, *refs):
    n_jobs = len(col_splits)
    cast_in, o_ref, cast_out = refs[:n_jobs], refs[n_jobs], refs[n_jobs + 1:]
    k = 0
    for src_ref, splits in zip(cast_in, col_splits):
        for lo, hi in splits:
            cast_out[k][...] = src_ref[:, lo:hi].astype(BF16)
            k += 1
    _swiglu_tile(h_ref, pre_g_ref[...], w_in_ref, w_out_ref, post_g_ref[...], o_ref)


def _slab_rows(n_rows, n_steps):
    rows = BF16_SUBLANES
    while n_rows % rows or rows * n_steps < n_rows:
        rows += BF16_SUBLANES
    return rows


def _ffn(h, pre_g, w_in, w_out, post_g, *, tm, cast_jobs=()):
    T, D = h.shape
    n_steps = T // tm
    tile = pl.BlockSpec((tm, D), lambda i: (i, 0))
    cast_in_specs, cast_out_specs, cast_out_shapes = [], [], []
    for w, splits in cast_jobs:
        rows = _slab_rows(w.shape[0], n_steps)
        last = w.shape[0] // rows - 1
        index = lambda i, last=last: (jnp.minimum(i, last), 0)
        cast_in_specs.append(pl.BlockSpec((rows, w.shape[1]), index))
        for lo, hi in splits:
            cast_out_specs.append(pl.BlockSpec((rows, hi - lo), index))
            cast_out_shapes.append(jax.ShapeDtypeStruct((w.shape[0], hi - lo), BF16))
    outs = pl.pallas_call(
        functools.partial(_ffn_kernel, tuple(tuple(s) for _, s in cast_jobs)),
        out_shape=[jax.ShapeDtypeStruct((T, D), F32)] + cast_out_shapes,
        grid=(n_steps,),
        in_specs=[tile, _const_spec((1, D)), _const_spec((D, 2 * D_FF)),
                  _const_spec((D_FF, D)), _const_spec((1, D))] + cast_in_specs,
        out_specs=[tile] + cast_out_specs,
        compiler_params=pltpu.CompilerParams(
            dimension_semantics=("arbitrary",), vmem_limit_bytes=VMEM_LIMIT_BYTES),
        name="swiglu_half_step",
    )(h, pre_g.reshape(1, D), w_in.astype(BF16), w_out.astype(BF16), post_g.reshape(1, D),
      *[w for w, _ in cast_jobs])
    return outs


ATTN_COLS = SWA_COLS + MEM_WIDTH
BRANCH_COLS = RWKV_COLS + ATTN_COLS

def _rope_tables(pos_row):
    half = ROPE_DIM // 2
    f = lax.broadcasted_iota(jnp.int32, (half, 1), 0).astype(F32)
    inv_freq = jnp.exp(f * (-2.0 / ROPE_DIM * math.log(ROPE_THETA)))
    ang = inv_freq * pos_row
    cs = jnp.concatenate([jnp.cos(ang), jnp.sin(ang)], axis=0)
    src = lax.broadcasted_iota(jnp.int32, (2 * half, 256), 0)
    col = lax.broadcasted_iota(jnp.int32, (2 * half, 256), 1)
    d = col % SWA_HEAD_DIM
    is_sin = col >= 128
    want = jnp.where(is_sin, half, 0) + d % half
    sign = jnp.where(is_sin & (d < half), -1.0, 1.0)
    sel = jnp.where((src == want) & (d < ROPE_DIM), sign, 0.0).astype(BF16)
    hi = cs.astype(BF16)
    lo = (cs - hi.astype(F32)).astype(BF16)
    tab = _dot_tn(hi, sel) + _dot_tn(lo, sel)
    lane_d = lax.broadcasted_iota(jnp.int32, (1, 128), 1) % SWA_HEAD_DIM
    c = tab[:, :128] + jnp.where(lane_d < ROPE_DIM, 0.0, 1.0)
    return c, tab[:, 128:]


def _apply_rope(x, c, s):
    half = ROPE_DIM // 2
    low = (lax.broadcasted_iota(jnp.int32, (1, 128), 1) % SWA_HEAD_DIM) < half
    cols = []
    for j in range(x.shape[1] // 128):
        xj = x[:, j * 128:(j + 1) * 128]
        partner = jnp.where(low, pltpu.roll(xj, 128 - half, axis=1),
                            pltpu.roll(xj, half, axis=1))
        cols.append(xj * c + partner * s)
    return cols[0] if len(cols) == 1 else jnp.concatenate(cols, axis=1)


def _kv_lane_variants(x):
    hd = SWA_HEAD_DIM
    low = lax.broadcasted_iota(jnp.int32, (1, 128), 1) < hd
    xs = pltpu.roll(x, hd, axis=1)
    return [[jnp.where(low, x, 0.0).astype(BF16), jnp.where(low, 0.0, xs).astype(BF16)],
            [jnp.where(low, xs, 0.0).astype(BF16), jnp.where(low, 0.0, x).astype(BF16)]]


def _swa_attention(q, k, v, kc_ref, vc_ref, sinks_ref, first_tile):
    tq = q.shape[0]
    blk = WINDOW
    nb = tq // blk
    kv = _kv_lane_variants(k)
    vv = _kv_lane_variants(v)
    qi = lax.broadcasted_iota(jnp.int32, (blk, blk), 0)
    kj = lax.broadcasted_iota(jnp.int32, (blk, blk), 1)
    from_prev = kj > qi
    prev_bias = jnp.where(first_tile, NEG_INF, 0.0)

    def window(var, ref, g, e, j):
        prev = ref[2 * g + e] if j == 0 else var[g][e][(j - 1) * blk:j * blk]
        return [prev, var[g][e][j * blk:(j + 1) * blk]]

    pair_outs = [[None] * (2 * SWA_KV_HEADS) for _ in range(nb)]

    def block_group(j, g):
        rs = slice(j * blk, (j + 1) * blk)
        k_rhs = jnp.concatenate(window(kv, kc_ref, g, 0, j) + window(kv, kc_ref, g, 1, j),
                                axis=0)
        v_rhs = jnp.concatenate(window(vv, vc_ref, g, 0, j) + window(vv, vc_ref, g, 1, j),
                                axis=0)
        q_rows = jnp.concatenate([q[rs, (2 * g + r) * 128:(2 * g + r + 1) * 128]
                                  for r in range(2)], axis=0)
        s_all = _dot_nt(q_rows, k_rhs)
        yield
        probs = []
        for r in range(2):
            row_p = []
            for e in range(2):
                sink = sinks_ref[SWA_GROUP * g + 2 * r + e]
                s_prev = s_all[r * blk:(r + 1) * blk, 2 * e * blk:(2 * e + 1) * blk]
                s_cur = s_all[r * blk:(r + 1) * blk, (2 * e + 1) * blk:(2 * e + 2) * blk]
                if j == 0:
                    s_prev = s_prev + prev_bias
                s = jnp.where(from_prev, s_prev, s_cur)
                m = jnp.maximum(jnp.max(s, axis=-1, keepdims=True), sink)
                ex = jnp.exp(s - m)
                denom = jnp.sum(ex, axis=-1, keepdims=True) + jnp.exp(sink - m)
                pn = ex * (1.0 / denom)
                row_p += [jnp.where(from_prev, pn, 0.0).astype(BF16),
                          jnp.where(from_prev, 0.0, pn).astype(BF16)]
            probs.append(jnp.concatenate(row_p, axis=1))
            yield
        o = _dot(jnp.concatenate(probs, axis=0), v_rhs)
        pair_outs[j][2 * g] = o[:blk]
        pair_outs[j][2 * g + 1] = o[blk:]

    _run_interleaved([block_group(j, g) for j in range(nb) for g in range(SWA_KV_HEADS)])
    outs = [jnp.concatenate(po, axis=1) for po in pair_outs]
    for g in range(SWA_KV_HEADS):
        for e in range(2):
            kc_ref[2 * g + e] = kv[g][e][tq - blk:tq]
            vc_ref[2 * g + e] = vv[g][e][tq - blk:tq]
    return jnp.concatenate(outs, axis=0)


def _attn_kernel(sinks_ref, h_ref, pos_ref, g_ref, w_ref, km_ref, vm_ref, yb_ref, yc_ref,
                 kc_ref, vc_ref):
    first_tile = pl.program_id(1) == 0

    @pl.when(first_tile)
    def _():
        kc_ref[...] = jnp.zeros_like(kc_ref)
        vc_ref[...] = jnp.zeros_like(vc_ref)

    u = _rms_norm(h_ref[0], g_ref[...]).astype(BF16)
    p = jnp.dot(u, w_ref[...], preferred_element_type=F32)
    c, s = _rope_tables(pos_ref[0].astype(F32))
    q = (_apply_rope(p[:, 0:SWA_Q_WIDTH], c, s) * (SWA_HEAD_DIM ** -0.5)).astype(BF16)
    k = _apply_rope(p[:, SWA_Q_WIDTH:SWA_Q_WIDTH + SWA_KV_WIDTH], c, s)
    v = p[:, SWA_Q_WIDTH + SWA_KV_WIDTH:SWA_COLS]
    yb_ref[0] = _swa_attention(q, k, v, kc_ref, vc_ref, sinks_ref,
                               first_tile).astype(yb_ref.dtype)
    yc_ref[0] = _mem_attention(p[:, SWA_COLS:ATTN_COLS], km_ref[0],
                               vm_ref[0]).astype(yc_ref.dtype)


def _attn_branches(h, positions, sinks, g, w_attn, km, vm, *, tq):
    B, S, D = h.shape
    M = km.shape[1]
    n_carry = 2 * SWA_KV_HEADS
    const = lambda shape: pl.BlockSpec(shape, lambda b, i, *_: (0,) * len(shape),
                                       pipeline_mode=pl.Buffered(1))
    return pl.pallas_call(
        _attn_kernel,
        out_shape=(jax.ShapeDtypeStruct((B, S, SWA_Q_WIDTH), BF16),
                   jax.ShapeDtypeStruct((B, S, MEM_WIDTH), BF16)),
        grid_spec=pltpu.PrefetchScalarGridSpec(
            num_scalar_prefetch=1,
            grid=(B, S // tq),
            in_specs=[pl.BlockSpec((1, tq, D), lambda b, i, *_: (b, i, 0)),
                      pl.BlockSpec((1, 1, tq), lambda b, i, *_: (b, 0, i)),
                      const((1, D)), const((D, ATTN_COLS)),
                      pl.BlockSpec((1, M, MEM_WIDTH), lambda b, i, *_: (b, 0, 0)),
                      pl.BlockSpec((1, M, MEM_WIDTH), lambda b, i, *_: (b, 0, 0))],
            out_specs=(pl.BlockSpec((1, tq, SWA_Q_WIDTH), lambda b, i, *_: (b, i, 0)),
                       pl.BlockSpec((1, tq, MEM_WIDTH), lambda b, i, *_: (b, i, 0))),
            scratch_shapes=[pltpu.VMEM((n_carry, WINDOW, 128), BF16),
                            pltpu.VMEM((n_carry, WINDOW, 128), BF16)]),
        compiler_params=pltpu.CompilerParams(
            dimension_semantics=("arbitrary", "arbitrary"),
            vmem_limit_bytes=VMEM_LIMIT_BYTES),
        name="attn_branches",
    )(sinks.astype(F32), h, positions.reshape(B, 1, S), g.reshape(1, D), w_attn, km, vm)


def _mem_kv_kernel(mem_ref, g_ref, w_ref, km_ref, vm_ref):
    mn = _rms_norm(mem_ref[0], g_ref[...]).astype(BF16)
    km_ref[0] = jnp.dot(mn, w_ref[:, 0:MEM_WIDTH],
                        preferred_element_type=F32).astype(km_ref.dtype)
    vm_ref[0] = jnp.dot(mn, w_ref[:, MEM_WIDTH:2 * MEM_WIDTH],
                        preferred_element_type=F32).astype(vm_ref.dtype)


def _mem_kv(mem, g, w_kv):
    B, M, D = mem.shape
    out = jax.ShapeDtypeStruct((B, M, MEM_WIDTH), BF16)
    return pl.pallas_call(
        _mem_kv_kernel,
        out_shape=(out, out),
        grid=(B,),
        in_specs=[pl.BlockSpec((1, M, D), lambda b: (b, 0, 0)), _const_spec((1, D)),
                  _const_spec((D, 2 * MEM_WIDTH))],
        out_specs=(pl.BlockSpec((1, M, MEM_WIDTH), lambda b: (b, 0, 0)),
                   pl.BlockSpec((1, M, MEM_WIDTH), lambda b: (b, 0, 0))),
        compiler_params=pltpu.CompilerParams(
            dimension_semantics=("arbitrary",), vmem_limit_bytes=VMEM_LIMIT_BYTES),
        name="mem_kv_proj",
    )(mem, g.reshape(1, D), w_kv.astype(BF16))


def _mem_attention(q, km, vm):
    hd = MEM_HEAD_DIM
    q = (q * (hd ** -0.5)).astype(BF16)
    outs = []
    for h in range(MEM_HEADS):
        hs = slice(h * hd, (h + 1) * hd)
        s = _dot_nt(q[:, hs], km[:, hs])
        e = jnp.exp(s - jnp.max(s, axis=-1, keepdims=True))
        outs.append(_dot(e, vm[:, hs]) / jnp.sum(e, axis=-1, keepdims=True))
    return jnp.concatenate(outs, axis=1)


MERGE_SUBTILES = 2
MERGE_STAGGER = 3


def _merge_kernel(h_ref, ya_ref, yb_ref, yc_ref, pre_g_ref, wg_ref, gb_ref, wb_ref, wo_ref,
                  post_g_ref, o_ref):
    D = D_MODEL

    def sub_tile(rows):
        h = h_ref[rows, :]
        u = _rms_norm(h, pre_g_ref[...]).astype(BF16)
        merged = jnp.zeros(h.shape, F32)
        yield
        for i, y_ref in enumerate((ya_ref, yb_ref, yc_ref)):
            logits = jnp.dot(u, wg_ref[:, i * D:(i + 1) * D], preferred_element_type=F32)
            gate = _sigmoid(logits + gb_ref[:, i * D:(i + 1) * D])
            merged = merged + gate * jnp.dot(y_ref[rows, :], wb_ref[i],
                                             preferred_element_type=F32)
            yield
        out = jnp.dot(merged.astype(BF16), wo_ref[...], preferred_element_type=F32)
        yield
        o_ref[rows, :] = h + _rms_norm(out, post_g_ref[...])

    sub = h_ref.shape[0] // MERGE_SUBTILES
    _run_interleaved([sub_tile(slice(s * sub, (s + 1) * sub)) for s in range(MERGE_SUBTILES)],
                     stagger=MERGE_STAGGER)


def _merge(h, y_a, y_b, y_c, pre_g, w_gate, gate_b, w_branch, w_out, post_g, *, tm):
    T, D = h.shape
    tile = pl.BlockSpec((tm, D), lambda i: (i, 0))
    ytile = pl.BlockSpec((tm, BRANCH_WIDTH), lambda i: (i, 0))
    return pl.pallas_call(
        _merge_kernel,
        out_shape=jax.ShapeDtypeStruct((T, D), F32),
        grid=(T // tm,),
        in_specs=[tile, ytile, ytile, ytile, _const_spec((1, D)),
                  _const_spec((D, N_BRANCH * D)), _const_spec((1, N_BRANCH * D)),
                  _const_spec((N_BRANCH, BRANCH_WIDTH, D)), _const_spec((D, D)),
                  _const_spec((1, D))],
        out_specs=tile,
        compiler_params=pltpu.CompilerParams(
            dimension_semantics=("arbitrary",), vmem_limit_bytes=VMEM_LIMIT_BYTES),
        name="gated_merge",
    )(h, y_a, y_b, y_c, pre_g.reshape(1, D), w_gate, gate_b.reshape(1, N_BRANCH * D),
      w_branch.astype(BF16), w_out.astype(BF16), post_g.reshape(1, D))


def kernel(x, mem, positions, ffn1_pre_g, ffn1_w_in, ffn1_w_out, ffn1_post_g, mix_pre_g, w_in, gate_b, rwkv_mu, rwkv_w0, rwkv_w_up, rwkv_a0, rwkv_a_up, rwkv_g_up, rwkv_k_k, rwkv_k_a, rwkv_r_k, rwkv_gn_g, rwkv_gn_b, swa_sinks, mem_norm_g, mem_w_kv, w_branch, w_out, mix_post_g, ffn2_pre_g, ffn2_w_in, ffn2_w_out, ffn2_post_g):
    B, S, D = x.shape
    T = B * S
    tm = min(1024, T)
    ts = min(512, S)
    h = x.reshape(T, D)
    for l in range(ffn1_pre_g.shape[0]):
        whole = lambda w: ((0, w.shape[1]),)
        cast_jobs = (
            (w_in[l], ((0, RWKV_COLS), (RWKV_COLS, BRANCH_COLS), (BRANCH_COLS, w_in.shape[2]))),
            (mem_w_kv[l], whole(mem_w_kv[l])),
            (w_branch[l].reshape(N_BRANCH * BRANCH_WIDTH, D), ((0, D),)),
            (w_out[l], whole(w_out[l])),
            (ffn2_w_in[l], whole(ffn2_w_in[l])),
            (ffn2_w_out[l], whole(ffn2_w_out[l])),
        )
        (h, w_rwkv, w_attn, w_gate, w_kv_bf, w_branch_bf, w_out_bf, ffn2_w_in_bf,
         ffn2_w_out_bf) = _ffn(h, ffn1_pre_g[l], ffn1_w_in[l], ffn1_w_out[l], ffn1_post_g[l],
                               tm=tm, cast_jobs=cast_jobs)

        h3 = h.reshape(B, S, D)
        y_a = _rwkv_mix(h3, mix_pre_g[l], w_rwkv, rwkv_mu[l], rwkv_w0[l],
                        rwkv_w_up[l], rwkv_a0[l], rwkv_a_up[l], rwkv_g_up[l], rwkv_k_k[l],
                        rwkv_k_a[l], rwkv_r_k[l].reshape(-1), rwkv_gn_g[l], rwkv_gn_b[l], ts=ts)
        km, vm = _mem_kv(mem, mem_norm_g[l], w_kv_bf)
        y_b, y_c = _attn_branches(h3, positions, swa_sinks[l], mix_pre_g[l], w_attn, km, vm,
                                  tq=ts)
        h = _merge(h, y_a.reshape(T, -1), y_b.reshape(T, -1), y_c.reshape(T, -1), mix_pre_g[l],
                   w_gate, gate_b[l], w_branch_bf.reshape(N_BRANCH, BRANCH_WIDTH, D), w_out_bf,
                   mix_post_g[l], tm=tm)

        h, = _ffn(h, ffn2_pre_g[l], ffn2_w_in_bf, ffn2_w_out_bf, ffn2_post_g[l], tm=tm)
    return h.reshape(B, S, D)
```

```python
import functools
import math

import jax
import jax.numpy as jnp
from jax import lax
from jax.experimental import pallas as pl
from jax.experimental.pallas import tpu as pltpu

F32 = jnp.float32
BF16 = jnp.bfloat16

NORM_EPS = 1e-6
HALF_STEP = 0.5

RWKV_HEADS = 8
RWKV_HEAD_DIM = 64
RWKV_WIDTH = RWKV_HEADS * RWKV_HEAD_DIM
DECAY_LORA = 64
AAA_LORA = 64
GATE_LORA = 128
RWKV_COLS = 3 * RWKV_WIDTH + DECAY_LORA + AAA_LORA + GATE_LORA
RWKV_GN_EPS = 64e-5
RWKV_CHUNK = 64
RWKV_HEAD_GROUP = 4
assert RWKV_CHUNK == RWKV_HEAD_DIM and RWKV_HEADS % RWKV_HEAD_GROUP == 0
RWKV_CHUNK_STAGGER = 1

SWA_Q_HEADS = 8
SWA_KV_HEADS = 2
SWA_GROUP = SWA_Q_HEADS // SWA_KV_HEADS
SWA_HEAD_DIM = 64
SWA_Q_WIDTH = SWA_Q_HEADS * SWA_HEAD_DIM
SWA_KV_WIDTH = SWA_KV_HEADS * SWA_HEAD_DIM
SWA_COLS = SWA_Q_WIDTH + 2 * SWA_KV_WIDTH
WINDOW = 128
ROPE_THETA = 500000.0
ROPE_DIM = SWA_HEAD_DIM // 4
NEG_INF = -1e30

MEM_HEADS = 4
MEM_HEAD_DIM = 128
MEM_WIDTH = MEM_HEADS * MEM_HEAD_DIM

N_BRANCH = 3
BRANCH_WIDTH = 512

BF16_SUBLANES = 16
VMEM_LIMIT_BYTES = 56 * 1024 * 1024


def _dot(a, b):
    return jnp.dot(a.astype(BF16), b.astype(BF16), preferred_element_type=F32)


def _dot_nt(a, b):
    return lax.dot_general(a.astype(BF16), b.astype(BF16), (((1,), (1,)), ((), ())),
                           preferred_element_type=F32)


def _dot_tn(a, b):
    return lax.dot_general(a.astype(BF16), b.astype(BF16), (((0,), (0,)), ((), ())),
                           preferred_element_type=F32)


def _split3(x):
    hi = x.astype(BF16)
    r1 = x - hi.astype(F32)
    mid = r1.astype(BF16)
    lo = (r1 - mid.astype(F32)).astype(BF16)
    return hi, mid, lo


def _rms_norm(x, g):
    return x * lax.rsqrt(jnp.mean(x * x, axis=-1, keepdims=True) + NORM_EPS) * g


def _sigmoid(x):
    return 1.0 / (1.0 + jnp.exp(-x))


def _run_interleaved(gens, starts=None):
    waiting = sorted(zip(starts or [0] * len(gens), range(len(gens))))
    active = []
    rnd = 0
    while waiting or active:
        while waiting and waiting[0][0] <= rnd:
            active.append(gens[waiting.pop(0)[1]])
        active = [gen for gen in active if next(gen, True) is None]
        rnd += 1


def _rwkv_kernel(h_ref, pre_g_ref, w_ref, mu_ref, w0_ref, wup_ref, a0_ref, aup_ref, gup_ref,
                 kk_ref, ka_ref, rk_ref, gng_ref, gnb_ref, y_ref, carry_ref, state_ref, q_scr,
                 y0_scr, m_scr, n_scr, yh_scr):
    C, H, N, GH = RWKV_CHUNK, RWKV_HEADS, RWKV_HEAD_DIM, RWKV_HEAD_GROUP
    W = RWKV_WIDTH
    ts = h_ref.shape[1]
    n_chunks = ts // C

    @pl.when(pl.program_id(1) == 0)
    def _():
        carry_ref[...] = jnp.zeros_like(carry_ref)
        state_ref[...] = jnp.zeros_like(state_ref)

    u = _rms_norm(h_ref[0], pre_g_ref[...]).astype(BF16)
    p = jnp.dot(u, w_ref[...], preferred_element_type=F32)
    row = lax.broadcasted_iota(jnp.int32, p.shape, 0)
    p_prev = jnp.where(row == 0, carry_ref[...], pltpu.roll(p, 1, axis=0))
    carry_ref[...] = p[ts - 1:ts, :]
    xs = p + (p_prev - p) * mu_ref[...]

    r = xs[:, 0:W]
    k = xs[:, W:2 * W]
    v = xs[:, 2 * W:3 * W]
    o = 3 * W
    wd = xs[:, o:o + DECAY_LORA]
    ad = xs[:, o + DECAY_LORA:o + DECAY_LORA + AAA_LORA]
    gd = xs[:, o + DECAY_LORA + AAA_LORA:o + DECAY_LORA + AAA_LORA + GATE_LORA]

    z = w0_ref[...] + _dot(jnp.tanh(wd), wup_ref[...])
    lw = -math.exp(-0.5) * _sigmoid(z)
    a = _sigmoid(a0_ref[...] + _dot(ad, aup_ref[...]))
    g = _dot(_sigmoid(gd), gup_ref[...])
    kk_raw = k * kk_ref[...]
    k2 = k * (1.0 + (a - 1.0) * ka_ref[...])
    rkr = r * k2 * rk_ref[...]

    GL = GH * N
    row_blk = lax.broadcasted_iota(jnp.int32, (GL, GL), 0) // N
    col_blk = lax.broadcasted_iota(jnp.int32, (GL, GL), 1) // N
    same_head = (row_blk == col_blk).astype(BF16)
    ti = lax.broadcasted_iota(jnp.int32, (C, GL), 0)
    si = lax.broadcasted_iota(jnp.int32, (C, GL), 1) % N
    cat_incl = ti >= si
    cat_strict = ti > si
    cat_eye = ti == si
    ci = lax.broadcasted_iota(jnp.int32, (C, C), 0)
    cj = lax.broadcasted_iota(jnp.int32, (C, C), 1)
    tri_f = (ci >= cj).astype(BF16)

    def block_diag(x):
        xb = x.astype(BF16)
        return jnp.concatenate([xb] * GH, axis=0) * same_head

    def head_sum(x):
        return jnp.dot(x.astype(BF16), same_head, preferred_element_type=F32)

    kk_parts = []
    for gi in range(H // GH):
        gl = slice(gi * GL, (gi + 1) * GL)
        kk_g = kk_raw[:, gl]
        kk_parts.append(kk_g * lax.rsqrt(jnp.maximum(head_sum(kk_g * kk_g), 1e-24)))
    kk = jnp.concatenate(kk_parts, axis=1)
    bvec = kk * a

    def chunk_terms(c, gi, r_t, a_t, b_t, k_t, v_c, p_end):
        bd_b = block_diag(b_t)
        bd_k = block_diag(k_t)
        bd_v = block_diag(v_c)
        diag_p = jnp.where(cat_eye, p_end, 0.0)
        gram = _dot_nt(jnp.concatenate([a_t, r_t, diag_p], axis=0),
                       jnp.concatenate([bd_b, bd_k], axis=0))
        yield
        a_ab = jnp.where(cat_strict, gram[:C, :GL], 0.0)
        a_ak = jnp.where(cat_strict, gram[:C, GL:], 0.0)
        a_rb = jnp.where(cat_incl, gram[C:2 * C, :GL], 0.0)
        a_rk = jnp.where(cat_incl, gram[C:2 * C, GL:], 0.0)
        bt_end = gram[2 * C:, :GL]
        kt_end = gram[2 * C:, GL:]
        x = jnp.where(cat_eye, 1.0, a_ab)
        pw = _dot(a_ab, block_diag(a_ab))
        xv = _dot(jnp.concatenate([a_ak, a_rk, kt_end], axis=0), bd_v)
        akv, y0, n0 = xv[:C], xv[C:2 * C], xv[2 * C:]
        yield
        for _ in range(int(math.log2(C)) - 2):
            sq = _dot(jnp.concatenate([pw, x], axis=0), block_diag(pw))
            pw = sq[:C]
            x = x + sq[C:]
            yield
        x = x + _dot(x, block_diag(pw))
        yield
        lt = _dot(jnp.concatenate([a_rb, bt_end], axis=0), block_diag(x))
        yield
        qymn = _dot(lt, jnp.concatenate([block_diag(a_t), block_diag(akv)], axis=1))
        q_scr[c, gi] = (r_t + qymn[:C, :GL]).astype(BF16)
        y0_scr[c, gi] = qymn[:C, GL:] + y0
        m_scr[c, gi] = (diag_p + qymn[C:, :GL]).astype(BF16)
        n_scr[c, gi] = qymn[C:, GL:] + n0

    def chunk_gen(c):
        rows = slice(c * C, (c + 1) * C)
        lw_c = lw[rows]
        hi, mid, lo = _split3(lw_c)
        cum = (jnp.dot(tri_f, hi, preferred_element_type=F32)
               + jnp.dot(tri_f, mid, preferred_element_type=F32)
               + jnp.dot(tri_f, lo, preferred_element_type=F32))
        e_pos = jnp.exp(cum)
        e_neg = jnp.exp(-cum)
        e_prev = jnp.exp(cum - lw_c)
        p_end = jnp.exp(cum[C - 1:C, :])
        r_t = r[rows] * e_pos
        a_t = -kk[rows] * e_prev
        b_t = bvec[rows] * e_neg
        k_t = k2[rows] * e_neg
        v_c = v[rows]
        yield
        subs = []
        for gi in range(H // GH):
            gl = slice(gi * GL, (gi + 1) * GL)
            subs.append(chunk_terms(c, gi, r_t[:, gl], a_t[:, gl], b_t[:, gl], k_t[:, gl],
                                    v_c[:, gl], p_end[:, gl]))
        while subs:
            subs = [sub for sub in subs if next(sub, True) is None]
            yield
        chunk_done[c] = True

    chunk_done = [False] * n_chunks
    states = [state_ref[gi] for gi in range(H // GH)]

    def scan_gen():
        for c in range(n_chunks):
            while not chunk_done[c]:
                yield
            rows = slice(c * C, (c + 1) * C)
            for gi in range(H // GH):
                gl = slice(gi * GL, (gi + 1) * GL)
                res = _dot(jnp.concatenate([q_scr[c, gi], m_scr[c, gi]], axis=0),
                           block_diag(states[gi]))
                yh_scr[rows, gl] = res[:C] + y0_scr[c, gi]
                states[gi] = res[C:] + n_scr[c, gi]
            yield

    _run_interleaved([scan_gen()] + [chunk_gen(c) for c in range(n_chunks)],
                     starts=[0] + [c * RWKV_CHUNK_STAGGER for c in range(n_chunks)])
    for gi in range(H // GH):
        state_ref[gi] = states[gi]

    outs = []
    for gi in range(H // GH):
        gl = slice(gi * GL, (gi + 1) * GL)
        y_g = yh_scr[:, gl]
        yc = y_g - head_sum(y_g) * (1.0 / N)
        var = head_sum(yc * yc) * (1.0 / N)
        bonus = head_sum(rkr[:, gl])
        outs.append(yc * lax.rsqrt(var + RWKV_GN_EPS) * gng_ref[:, gl] + gnb_ref[:, gl]
                    + bonus * v[:, gl])
    y_ref[0] = (jnp.concatenate(outs, axis=-1) * g).astype(y_ref.dtype)


def _rwkv_mix(h, pre_g, w_proj, mu, w0, w_up, a0, a_up, g_up, k_k, k_a, r_k, gn_g, gn_b, *,
              ts):
    B, S, D = h.shape
    C, H, N, W = RWKV_CHUNK, RWKV_HEADS, RWKV_HEAD_DIM, RWKV_WIDTH
    n_chunks = ts // C
    NG, GL = H // RWKV_HEAD_GROUP, RWKV_HEAD_GROUP * N
    row = lambda t: t.reshape(1, -1).astype(F32)
    full = lambda shape: pl.BlockSpec(shape, lambda b, i: (0,) * len(shape),
                                      pipeline_mode=pl.Buffered(1))
    return pl.pallas_call(
        _rwkv_kernel,
        out_shape=jax.ShapeDtypeStruct((B, S, W), BF16),
        grid=(B, S // ts),
        in_specs=[
            pl.BlockSpec((1, ts, D), lambda b, i: (b, i, 0)),
            full((1, D)), full((D, RWKV_COLS)),
            full((1, RWKV_COLS)), full((1, W)), full((DECAY_LORA, W)), full((1, W)),
            full((AAA_LORA, W)), full((GATE_LORA, W)), full((1, W)), full((1, W)),
            full((1, W)), full((1, W)), full((1, W)),
        ],
        out_specs=pl.BlockSpec((1, ts, W), lambda b, i: (b, i, 0)),
        scratch_shapes=[
            pltpu.VMEM((1, RWKV_COLS), F32),
            pltpu.VMEM((NG, N, GL), F32),
            pltpu.VMEM((n_chunks, NG, C, GL), BF16),
            pltpu.VMEM((n_chunks, NG, C, GL), F32),
            pltpu.VMEM((n_chunks, NG, N, GL), BF16),
            pltpu.VMEM((n_chunks, NG, N, GL), F32),
            pltpu.VMEM((ts, W), F32),
        ],
        compiler_params=pltpu.CompilerParams(
            dimension_semantics=("arbitrary", "arbitrary"),
            vmem_limit_bytes=VMEM_LIMIT_BYTES),
        name="rwkv7_mix",
    )(h, row(pre_g), w_proj, row(mu), row(w0), w_up.astype(BF16), row(a0), a_up.astype(BF16),
      g_up.astype(BF16), row(k_k), row(k_a), row(r_k), row(gn_g), row(gn_b))


D_MODEL = 1024
D_FF = 2816
FFN_CHUNK = 256
FFN_SUBTILES = 2
FFN_STAGGER = 9


def _const_spec(shape):
    return pl.BlockSpec(shape, lambda *_: (0,) * len(shape), pipeline_mode=pl.Buffered(1))


def _swiglu_half_step(h_ref, pre_g, w_in_ref, w_out_ref, post_g, o_ref, rows):
    h = h_ref[rows, :]
    xn = _rms_norm(h, pre_g).astype(BF16)
    acc = jnp.zeros(h.shape, F32)
    yield
    for c in range(D_FF // FFN_CHUNK):
        lo = c * FFN_CHUNK
        gate = jnp.dot(xn, w_in_ref[:, lo:lo + FFN_CHUNK], preferred_element_type=F32)
        up = jnp.dot(xn, w_in_ref[:, D_FF + lo:D_FF + lo + FFN_CHUNK],
                     preferred_element_type=F32)
        act = (gate * _sigmoid(gate) * up).astype(BF16)
        acc = acc + jnp.dot(act, w_out_ref[lo:lo + FFN_CHUNK, :], preferred_element_type=F32)
        yield
    o_ref[rows, :] = h + HALF_STEP * _rms_norm(acc, post_g)


def _swiglu_tile(h_ref, pre_g, w_in_ref, w_out_ref, post_g, o_ref):
    sub = h_ref.shape[0] // FFN_SUBTILES
    _run_interleaved(
        [_swiglu_half_step(h_ref, pre_g, w_in_ref, w_out_ref, post_g, o_ref,
                           slice(s * sub, (s + 1) * sub)) for s in range(FFN_SUBTILES)],
        starts=[s * FFN_STAGGER for s in range(FFN_SUBTILES)])


def _ffn_kernel(col_splits, h_ref, pre_g_ref, w_in_ref, w_out_ref, post_g_ref, *refs):
    n_jobs = len(col_splits)
    cast_in, o_ref, cast_out = refs[:n_jobs], refs[n_jobs], refs[n_jobs + 1:]
    k = 0
    for src_ref, splits in zip(cast_in, col_splits):
        for lo, hi in splits:
            cast_out[k][...] = src_ref[:, lo:hi].astype(BF16)
            k += 1
    _swiglu_tile(h_ref, pre_g_ref[...], w_in_ref, w_out_ref, post_g_ref[...], o_ref)


def _slab_rows(n_rows, n_steps):
    rows = BF16_SUBLANES
    while n_rows % rows or rows * n_steps < n_rows:
        rows += BF16_SUBLANES
    return rows


def _ffn(h, pre_g, w_in, w_out, post_g, *, tm, cast_jobs=()):
    T, D = h.shape
    n_steps = T // tm
    tile = pl.BlockSpec((tm, D), lambda i: (i, 0))
    cast_in_specs, cast_out_specs, cast_out_shapes = [], [], []
    for w, splits in cast_jobs:
        rows = _slab_rows(w.shape[0], n_steps)
        last = w.shape[0] // rows - 1
        index = lambda i, last=last: (jnp.minimum(i, last), 0)
        cast_in_specs.append(pl.BlockSpec((rows, w.shape[1]), index))
        for lo, hi in splits:
            cast_out_specs.append(pl.BlockSpec((rows, hi - lo), index))
            cast_out_shapes.append(jax.ShapeDtypeStruct((w.shape[0], hi - lo), BF16))
    outs = pl.pallas_call(
        functools.partial(_ffn_kernel, tuple(tuple(s) for _, s in cast_jobs)),
        out_shape=[jax.ShapeDtypeStruct((T, D), F32)] + cast_out_shapes,
        grid=(n_steps,),
        in_specs=[tile, _const_spec((1, D)), _const_spec((D, 2 * D_FF)),
                  _const_spec((D_FF, D)), _const_spec((1, D))] + cast_in_specs,
        out_specs=[tile] + cast_out_specs,
        compiler_params=pltpu.CompilerParams(
            dimension_semantics=("arbitrary",), vmem_limit_bytes=VMEM_LIMIT_BYTES),
        name="swiglu_half_step",
    )(h, pre_g.reshape(1, D), w_in.astype(BF16), w_out.astype(BF16), post_g.reshape(1, D),
      *[w for w, _ in cast_jobs])
    return outs


ATTN_COLS = SWA_COLS + MEM_WIDTH
BRANCH_COLS = RWKV_COLS + ATTN_COLS

def _rope_tables(pos_row):
    half = ROPE_DIM // 2
    f = lax.broadcasted_iota(jnp.int32, (half, 1), 0).astype(F32)
    inv_freq = jnp.exp(f * (-2.0 / ROPE_DIM * math.log(ROPE_THETA)))
    ang = inv_freq * pos_row
    cs = jnp.concatenate([jnp.cos(ang), jnp.sin(ang)], axis=0)
    src = lax.broadcasted_iota(jnp.int32, (2 * half, 256), 0)
    col = lax.broadcasted_iota(jnp.int32, (2 * half, 256), 1)
    d = col % SWA_HEAD_DIM
    is_sin = col >= 128
    want = jnp.where(is_sin, half, 0) + d % half
    sign = jnp.where(is_sin & (d < half), -1.0, 1.0)
    sel = jnp.where((src == want) & (d < ROPE_DIM), sign, 0.0).astype(BF16)
    hi = cs.astype(BF16)
    lo = (cs - hi.astype(F32)).astype(BF16)
    tab = _dot_tn(hi, sel) + _dot_tn(lo, sel)
    lane_d = lax.broadcasted_iota(jnp.int32, (1, 128), 1) % SWA_HEAD_DIM
    c = tab[:, :128] + jnp.where(lane_d < ROPE_DIM, 0.0, 1.0)
    return c, tab[:, 128:]


def _apply_rope(x, c, s):
    half = ROPE_DIM // 2
    low = (lax.broadcasted_iota(jnp.int32, (1, 128), 1) % SWA_HEAD_DIM) < half
    cols = []
    for j in range(x.shape[1] // 128):
        xj = x[:, j * 128:(j + 1) * 128]
        partner = jnp.where(low, pltpu.roll(xj, 128 - half, axis=1),
                            pltpu.roll(xj, half, axis=1))
        cols.append(xj * c + partner * s)
    return cols[0] if len(cols) == 1 else jnp.concatenate(cols, axis=1)


def _kv_lane_variants(x):
    hd = SWA_HEAD_DIM
    low = lax.broadcasted_iota(jnp.int32, (1, 128), 1) < hd
    xs = pltpu.roll(x, hd, axis=1)
    return [[jnp.where(low, x, 0.0).astype(BF16), jnp.where(low, 0.0, xs).astype(BF16)],
            [jnp.where(low, xs, 0.0).astype(BF16), jnp.where(low, 0.0, x).astype(BF16)]]


def _swa_attention(q, k, v, kc_ref, vc_ref, sinks_ref, first_tile):
    tq = q.shape[0]
    blk = WINDOW
    nb = tq // blk
    kv = _kv_lane_variants(k)
    vv = _kv_lane_variants(v)
    qi = lax.broadcasted_iota(jnp.int32, (blk, blk), 0)
    kj = lax.broadcasted_iota(jnp.int32, (blk, blk), 1)
    from_prev = kj > qi
    prev_bias = jnp.where(first_tile, NEG_INF, 0.0)

    def window(var, ref, g, e, j):
        prev = ref[2 * g + e] if j == 0 else var[g][e][(j - 1) * blk:j * blk]
        return [prev, var[g][e][j * blk:(j + 1) * blk]]

    pair_outs = [[None] * (2 * SWA_KV_HEADS) for _ in range(nb)]

    def block_group(j, g):
        rs = slice(j * blk, (j + 1) * blk)
        k_rhs = jnp.concatenate(window(kv, kc_ref, g, 0, j) + window(kv, kc_ref, g, 1, j),
                                axis=0)
        v_rhs = jnp.concatenate(window(vv, vc_ref, g, 0, j) + window(vv, vc_ref, g, 1, j),
                                axis=0)
        q_rows = jnp.concatenate([q[rs, (2 * g + r) * 128:(2 * g + r + 1) * 128]
                                  for r in range(2)], axis=0)
        s_all = _dot_nt(q_rows, k_rhs)
        yield
        probs = []
        for r in range(2):
            row_p = []
            for e in range(2):
                sink = sinks_ref[SWA_GROUP * g + 2 * r + e]
                s_prev = s_all[r * blk:(r + 1) * blk, 2 * e * blk:(2 * e + 1) * blk]
                s_cur = s_all[r * blk:(r + 1) * blk, (2 * e + 1) * blk:(2 * e + 2) * blk]
                if j == 0:
                    s_prev = s_prev + prev_bias
                s = jnp.where(from_prev, s_prev, s_cur)
                m = jnp.maximum(jnp.max(s, axis=-1, keepdims=True), sink)
                ex = jnp.exp(s - m)
                denom = jnp.sum(ex, axis=-1, keepdims=True) + jnp.exp(sink - m)
                pn = ex * (1.0 / denom)
                row_p += [jnp.where(from_prev, pn, 0.0).astype(BF16),
                          jnp.where(from_prev, 0.0, pn).astype(BF16)]
            probs.append(jnp.concatenate(row_p, axis=1))
            yield
        o = _dot(jnp.concatenate(probs, axis=0), v_rhs)
        pair_outs[j][2 * g] = o[:blk]
        pair_outs[j][2 * g + 1] = o[blk:]

    _run_interleaved([block_group(j, g) for j in range(nb) for g in range(SWA_KV_HEADS)])
    outs = [jnp.concatenate(po, axis=1) for po in pair_outs]
    for g in range(SWA_KV_HEADS):
        for e in range(2):
            kc_ref[2 * g + e] = kv[g][e][tq - blk:tq]
            vc_ref[2 * g + e] = vv[g][e][tq - blk:tq]
    return jnp.concatenate(outs, axis=0)


def _attn_kernel(sinks_ref, h_ref, pos_ref, g_ref, w_ref, km_ref, vm_ref, yb_ref, yc_ref,
                 kc_ref, vc_ref):
    first_tile = pl.program_id(1) == 0

    @pl.when(first_tile)
    def _():
        kc_ref[...] = jnp.zeros_like(kc_ref)
        vc_ref[...] = jnp.zeros_like(vc_ref)

    u = _rms_norm(h_ref[0], g_ref[...]).astype(BF16)
    p = jnp.dot(u, w_ref[...], preferred_element_type=F32)
    c, s = _rope_tables(pos_ref[0].astype(F32))
    q = (_apply_rope(p[:, 0:SWA_Q_WIDTH], c, s) * (SWA_HEAD_DIM ** -0.5)).astype(BF16)
    k = _apply_rope(p[:, SWA_Q_WIDTH:SWA_Q_WIDTH + SWA_KV_WIDTH], c, s)
    v = p[:, SWA_Q_WIDTH + SWA_KV_WIDTH:SWA_COLS]
    yb_ref[0] = _swa_attention(q, k, v, kc_ref, vc_ref, sinks_ref,
                               first_tile).astype(yb_ref.dtype)
    yc_ref[0] = _mem_attention(p[:, SWA_COLS:ATTN_COLS], km_ref[0],
                               vm_ref[0]).astype(yc_ref.dtype)


def _attn_branches(h, positions, sinks, g, w_attn, km, vm, *, tq):
    B, S, D = h.shape
    M = km.shape[1]
    n_carry = 2 * SWA_KV_HEADS
    const = lambda shape: pl.BlockSpec(shape, lambda b, i, *_: (0,) * len(shape),
                                       pipeline_mode=pl.Buffered(1))
    return pl.pallas_call(
        _attn_kernel,
        out_shape=(jax.ShapeDtypeStruct((B, S, SWA_Q_WIDTH), BF16),
                   jax.ShapeDtypeStruct((B, S, MEM_WIDTH), BF16)),
        grid_spec=pltpu.PrefetchScalarGridSpec(
            num_scalar_prefetch=1,
            grid=(B, S // tq),
            in_specs=[pl.BlockSpec((1, tq, D), lambda b, i, *_: (b, i, 0)),
                      pl.BlockSpec((1, 1, tq), lambda b, i, *_: (b, 0, i)),
                      const((1, D)), const((D, ATTN_COLS)),
                      pl.BlockSpec((1, M, MEM_WIDTH), lambda b, i, *_: (b, 0, 0)),
                      pl.BlockSpec((1, M, MEM_WIDTH), lambda b, i, *_: (b, 0, 0))],
            out_specs=(pl.BlockSpec((1, tq, SWA_Q_WIDTH), lambda b, i, *_: (b, i, 0)),
                       pl.BlockSpec((1, tq, MEM_WIDTH), lambda b, i, *_: (b, i, 0))),
            scratch_shapes=[pltpu.VMEM((n_carry, WINDOW, 128), BF16),
                            pltpu.VMEM((n_carry, WINDOW, 128), BF16)]),
        compiler_params=pltpu.CompilerParams(
            dimension_semantics=("arbitrary", "arbitrary"),
            vmem_limit_bytes=VMEM_LIMIT_BYTES),
        name="attn_branches",
    )(sinks.astype(F32), h, positions.reshape(B, 1, S), g.reshape(1, D), w_attn, km, vm)


def _mem_kv_kernel(mem_ref, g_ref, w_ref, km_ref, vm_ref):
    mn = _rms_norm(mem_ref[0], g_ref[...]).astype(BF16)
    km_ref[0] = jnp.dot(mn, w_ref[:, 0:MEM_WIDTH],
                        preferred_element_type=F32).astype(km_ref.dtype)
    vm_ref[0] = jnp.dot(mn, w_ref[:, MEM_WIDTH:2 * MEM_WIDTH],
                        preferred_element_type=F32).astype(vm_ref.dtype)


def _mem_kv(mem, g, w_kv):
    B, M, D = mem.shape
    out = jax.ShapeDtypeStruct((B, M, MEM_WIDTH), BF16)
    return pl.pallas_call(
        _mem_kv_kernel,
        out_shape=(out, out),
        grid=(B,),
        in_specs=[pl.BlockSpec((1, M, D), lambda b: (b, 0, 0)), _const_spec((1, D)),
                  _const_spec((D, 2 * MEM_WIDTH))],
        out_specs=(pl.BlockSpec((1, M, MEM_WIDTH), lambda b: (b, 0, 0)),
                   pl.BlockSpec((1, M, MEM_WIDTH), lambda b: (b, 0, 0))),
        compiler_params=pltpu.CompilerParams(
            dimension_semantics=("arbitrary",), vmem_limit_bytes=VMEM_LIMIT_BYTES),
        name="mem_kv_proj",
    )(mem, g.reshape(1, D), w_kv.astype(BF16))


def _mem_attention(q, km, vm):
    hd = MEM_HEAD_DIM
    q = (q * (hd ** -0.5)).astype(BF16)
    outs = []
    for h in range(MEM_HEADS):
        hs = slice(h * hd, (h + 1) * hd)
        s = _dot_nt(q[:, hs], km[:, hs])
        e = jnp.exp(s - jnp.max(s, axis=-1, keepdims=True))
        outs.append(_dot(e, vm[:, hs]) / jnp.sum(e, axis=-1, keepdims=True))
    return jnp.concatenate(outs, axis=1)


MERGE_SUBTILES = 2
MERGE_STAGGER = 3


def _merge_kernel(h_ref, ya_ref, yb_ref, yc_ref, pre_g_ref, wg_ref, gb_ref, wb_ref, wo_ref,
                  post_g_ref, o_ref):
    D = D_MODEL

    def sub_tile(rows):
        h = h_ref[rows, :]
        u = _rms_norm(h, pre_g_ref[...]).astype(BF16)
        merged = jnp.zeros(h.shape, F32)
        yield
        for i, y_ref in enumerate((ya_ref, yb_ref, yc_ref)):
            logits = jnp.dot(u, wg_ref[:, i * D:(i + 1) * D], preferred_element_type=F32)
            gate = _sigmoid(logits + gb_ref[:, i * D:(i + 1) * D])
            merged = merged + gate * jnp.dot(y_ref[rows, :], wb_ref[i],
                                             preferred_element_type=F32)
            yield
        out = jnp.dot(merged.astype(BF16), wo_ref[...], preferred_element_type=F32)
        yield
        o_ref[rows, :] = h + _rms_norm(out, post_g_ref[...])

    sub = h_ref.shape[0] // MERGE_SUBTILES
    _run_interleaved([sub_tile(slice(s * sub, (s + 1) * sub)) for s in range(MERGE_SUBTILES)],
                     starts=[s * MERGE_STAGGER for s in range(MERGE_SUBTILES)])


def _merge(h, y_a, y_b, y_c, pre_g, w_gate, gate_b, w_branch, w_out, post_g, *, tm):
    T, D = h.shape
    tile = pl.BlockSpec((tm, D), lambda i: (i, 0))
    ytile = pl.BlockSpec((tm, BRANCH_WIDTH), lambda i: (i, 0))
    return pl.pallas_call(
        _merge_kernel,
        out_shape=jax.ShapeDtypeStruct((T, D), F32),
        grid=(T // tm,),
        in_specs=[tile, ytile, ytile, ytile, _const_spec((1, D)),
                  _const_spec((D, N_BRANCH * D)), _const_spec((1, N_BRANCH * D)),
                  _const_spec((N_BRANCH, BRANCH_WIDTH, D)), _const_spec((D, D)),
                  _const_spec((1, D))],
        out_specs=tile,
        compiler_params=pltpu.CompilerParams(
            dimension_semantics=("arbitrary",), vmem_limit_bytes=VMEM_LIMIT_BYTES),
        name="gated_merge",
    )(h, y_a, y_b, y_c, pre_g.reshape(1, D), w_gate, gate_b.reshape(1, N_BRANCH * D),
      w_branch.astype(BF16), w_out.astype(BF16), post_g.reshape(1, D))


def kernel(x, mem, positions, ffn1_pre_g, ffn1_w_in, ffn1_w_out, ffn1_post_g, mix_pre_g, w_in, gate_b, rwkv_mu, rwkv_w0, rwkv_w_up, rwkv_a0, rwkv_a_up, rwkv_g_up, rwkv_k_k, rwkv_k_a, rwkv_r_k, rwkv_gn_g, rwkv_gn_b, swa_sinks, mem_norm_g, mem_w_kv, w_branch, w_out, mix_post_g, ffn2_pre_g, ffn2_w_in, ffn2_w_out, ffn2_post_g):
    B, S, D = x.shape
    T = B * S
    tm = min(1024, T)
    ts = min(1024, S)
    h = x.reshape(T, D)
    for l in range(ffn1_pre_g.shape[0]):
        whole = lambda w: ((0, w.shape[1]),)
        cast_jobs = (
            (w_in[l], ((0, RWKV_COLS), (RWKV_COLS, BRANCH_COLS), (BRANCH_COLS, w_in.shape[2]))),
            (mem_w_kv[l], whole(mem_w_kv[l])),
            (w_branch[l].reshape(N_BRANCH * BRANCH_WIDTH, D), ((0, D),)),
            (w_out[l], whole(w_out[l])),
            (ffn2_w_in[l], whole(ffn2_w_in[l])),
            (ffn2_w_out[l], whole(ffn2_w_out[l])),
        )
        (h, w_rwkv, w_attn, w_gate, w_kv_bf, w_branch_bf, w_out_bf, ffn2_w_in_bf,
         ffn2_w_out_bf) = _ffn(h, ffn1_pre_g[l], ffn1_w_in[l], ffn1_w_out[l], ffn1_post_g[l],
                               tm=tm, cast_jobs=cast_jobs)

        h3 = h.reshape(B, S, D)
        y_a = _rwkv_mix(h3, mix_pre_g[l], w_rwkv, rwkv_mu[l], rwkv_w0[l],
                        rwkv_w_up[l], rwkv_a0[l], rwkv_a_up[l], rwkv_g_up[l], rwkv_k_k[l],
                        rwkv_k_a[l], rwkv_r_k[l].reshape(-1), rwkv_gn_g[l], rwkv_gn_b[l], ts=ts)
        km, vm = _mem_kv(mem, mem_norm_g[l], w_kv_bf)
        y_b, y_c = _attn_branches(h3, positions, swa_sinks[l], mix_pre_g[l], w_attn, km, vm,
                                  tq=ts)
        h = _merge(h, y_a.reshape(T, -1), y_b.reshape(T, -1), y_c.reshape(T, -1), mix_pre_g[l],
                   w_gate, gate_b[l], w_branch_bf.reshape(N_BRANCH, BRANCH_WIDTH, D), w_out_bf,
                   mix_post_g[l], tm=tm)

        h, = _ffn(h, ffn2_pre_g[l], ffn2_w_in_bf, ffn2_w_out_bf, ffn2_post_g[l], tm=tm)
    return h.reshape(B, S, D)
```

```python
import functools
import math

import jax
import jax.numpy as jnp
from jax import lax
from jax.experimental import pallas as pl
from jax.experimental.pallas import tpu as pltpu

F32 = jnp.float32
BF16 = jnp.bfloat16

NORM_EPS = 1e-6
HALF_STEP = 0.5

RWKV_HEADS = 8
RWKV_HEAD_DIM = 64
RWKV_WIDTH = RWKV_HEADS * RWKV_HEAD_DIM
DECAY_LORA = 64
AAA_LORA = 64
GATE_LORA = 128
RWKV_COLS = 3 * RWKV_WIDTH + DECAY_LORA + AAA_LORA + GATE_LORA
RWKV_GN_EPS = 64e-5
RWKV_CHUNK = 64
RWKV_HEAD_GROUP = 4
assert RWKV_CHUNK == RWKV_HEAD_DIM and RWKV_HEADS % RWKV_HEAD_GROUP == 0
RWKV_CHUNK_STAGGER = 1
RWKV_BLOCK_ROWS = 256

SWA_Q_HEADS = 8
SWA_KV_HEADS = 2
SWA_GROUP = SWA_Q_HEADS // SWA_KV_HEADS
SWA_HEAD_DIM = 64
SWA_Q_WIDTH = SWA_Q_HEADS * SWA_HEAD_DIM
SWA_KV_WIDTH = SWA_KV_HEADS * SWA_HEAD_DIM
SWA_COLS = SWA_Q_WIDTH + 2 * SWA_KV_WIDTH
WINDOW = 128
ROPE_THETA = 500000.0
ROPE_DIM = SWA_HEAD_DIM // 4
NEG_INF = -1e30

MEM_HEADS = 4
MEM_HEAD_DIM = 128
MEM_WIDTH = MEM_HEADS * MEM_HEAD_DIM

N_BRANCH = 3
BRANCH_WIDTH = 512

LANES = 128
BF16_SUBLANES = 16
VMEM_LIMIT_BYTES = 56 * 1024 * 1024


def _dot(a, b):
    return jnp.dot(a.astype(BF16), b.astype(BF16), preferred_element_type=F32)


def _dot_nt(a, b):
    return lax.dot_general(a.astype(BF16), b.astype(BF16), (((1,), (1,)), ((), ())),
                           preferred_element_type=F32)


def _dot_tn(a, b):
    return lax.dot_general(a.astype(BF16), b.astype(BF16), (((0,), (0,)), ((), ())),
                           preferred_element_type=F32)


def _split3(x):
    hi = x.astype(BF16)
    r1 = x - hi.astype(F32)
    mid = r1.astype(BF16)
    lo = (r1 - mid.astype(F32)).astype(BF16)
    return hi, mid, lo


def _rms_norm(x, g):
    return x * lax.rsqrt(jnp.mean(x * x, axis=-1, keepdims=True) + NORM_EPS) * g


def _sigmoid(x):
    return 1.0 / (1.0 + jnp.exp(-x))


def _run_interleaved(gens, starts=None):
    waiting = sorted(zip(starts or [0] * len(gens), range(len(gens))))
    active = []
    rnd = 0
    while waiting or active:
        while waiting and waiting[0][0] <= rnd:
            active.append(gens[waiting.pop(0)[1]])
        active = [gen for gen in active if next(gen, True) is None]
        rnd += 1


def _rwkv_kernel(h_ref, pre_g_ref, w_ref, mu_ref, w0_ref, wup_ref, a0_ref, aup_ref, gup_ref,
                 kk_ref, ka_ref, rk_ref, gng_ref, gnb_ref, y_ref, carry_ref, state_ref, q_scr,
                 y0_scr, m_scr, n_scr, yh_scr):
    C, H, N, GH = RWKV_CHUNK, RWKV_HEADS, RWKV_HEAD_DIM, RWKV_HEAD_GROUP
    W = RWKV_WIDTH
    ts = h_ref.shape[1]
    n_chunks = ts // C

    @pl.when(pl.program_id(1) == 0)
    def _():
        carry_ref[...] = jnp.zeros_like(carry_ref)
        state_ref[...] = jnp.zeros_like(state_ref)

    GL = GH * N
    row_blk = lax.broadcasted_iota(jnp.int32, (GL, GL), 0) // N
    col_blk = lax.broadcasted_iota(jnp.int32, (GL, GL), 1) // N
    same_head = (row_blk == col_blk).astype(BF16)
    ti = lax.broadcasted_iota(jnp.int32, (C, GL), 0)
    si = lax.broadcasted_iota(jnp.int32, (C, GL), 1) % N
    cat_incl = ti >= si
    cat_strict = ti > si
    cat_eye = ti == si
    ci = lax.broadcasted_iota(jnp.int32, (C, C), 0)
    cj = lax.broadcasted_iota(jnp.int32, (C, C), 1)
    tri_f = (ci >= cj).astype(BF16)

    heads_per_vreg = LANES // N
    lane_head = lax.broadcasted_iota(jnp.int32, (C, LANES), 1) // N
    keep_head = [(lane_head == e).astype(BF16) for e in range(heads_per_vreg)]
    zero_cols = jnp.zeros((C, LANES), BF16)

    def block_diag(x):
        xb = x.astype(BF16)
        row_blocks = []
        for h in range(GH):
            col, e = divmod(h, heads_per_vreg)
            cols = [zero_cols] * (GL // LANES)
            cols[col] = xb[:, col * LANES:(col + 1) * LANES] * keep_head[e]
            row_blocks.append(jnp.concatenate(cols, axis=1))
        return jnp.concatenate(row_blocks, axis=0)

    def head_sum(x):
        return jnp.dot(x.astype(BF16), same_head, preferred_element_type=F32)

    NG = H // GH
    group_lanes = [slice(gi * GL, (gi + 1) * GL) for gi in range(NG)]
    RB = min(RWKV_BLOCK_ROWS, ts)
    n_blocks = ts // RB
    blocks = [None] * n_blocks

    proj = [None] * n_blocks

    def prologue_gen(b):
        u = _rms_norm(h_ref[0, b * RB:(b + 1) * RB, :], pre_g_ref[...]).astype(BF16)
        p_b = jnp.dot(u, w_ref[...], preferred_element_type=F32)
        proj[b] = p_b
        yield
        prev_row = carry_ref[...] if b == 0 else proj[b - 1][RB - 1:RB]
        row = lax.broadcasted_iota(jnp.int32, p_b.shape, 0)
        p_prev = jnp.where(row == 0, prev_row, pltpu.roll(p_b, 1, axis=0))
        xs = p_b + (p_prev - p_b) * mu_ref[...]
        r = xs[:, 0:W]
        k = xs[:, W:2 * W]
        v = xs[:, 2 * W:3 * W]
        o = 3 * W
        wd = xs[:, o:o + DECAY_LORA]
        ad = xs[:, o + DECAY_LORA:o + DECAY_LORA + AAA_LORA]
        gd = xs[:, o + DECAY_LORA + AAA_LORA:o + DECAY_LORA + AAA_LORA + GATE_LORA]
        z = w0_ref[...] + _dot(jnp.tanh(wd), wup_ref[...])
        lw = -math.exp(-0.5) * _sigmoid(z)
        a = _sigmoid(a0_ref[...] + _dot(ad, aup_ref[...]))
        g = _dot(_sigmoid(gd), gup_ref[...])
        kk_raw = k * kk_ref[...]
        k2 = k * (1.0 + (a - 1.0) * ka_ref[...])
        rkr = r * k2 * rk_ref[...]
        yield
        sums = head_sum(jnp.concatenate([kk_raw[:, gl] * kk_raw[:, gl] for gl in group_lanes]
                                        + [rkr[:, gl] for gl in group_lanes], axis=0))
        ssq = jnp.concatenate([sums[gi * RB:(gi + 1) * RB] for gi in range(NG)], axis=1)
        bonus = jnp.concatenate([sums[(NG + gi) * RB:(NG + gi + 1) * RB] for gi in range(NG)],
                                axis=1)
        kk = kk_raw * lax.rsqrt(jnp.maximum(ssq, 1e-24))
        blocks[b] = dict(r=r, k2=k2, v=v, lw=lw, kk=kk, bvec=kk * a, g=g, bonus=bonus)

    def chunk_terms(c, gi, r_t, a_t, b_t, k_t, v_c, p_end):
        bd_b = block_diag(b_t)
        bd_k = block_diag(k_t)
        bd_v = block_diag(v_c)
        diag_p = jnp.where(cat_eye, p_end, 0.0)
        gram = _dot_nt(jnp.concatenate([a_t, r_t, diag_p], axis=0),
                       jnp.concatenate([bd_b, bd_k], axis=0))
        yield
        a_ab = jnp.where(cat_strict, gram[:C, :GL], 0.0)
        a_ak = jnp.where(cat_strict, gram[:C, GL:], 0.0)
        a_rb = jnp.where(cat_incl, gram[C:2 * C, :GL], 0.0)
        a_rk = jnp.where(cat_incl, gram[C:2 * C, GL:], 0.0)
        bt_end = gram[2 * C:, :GL]
        kt_end = gram[2 * C:, GL:]
        x = jnp.where(cat_eye, 1.0, a_ab)
        pw = _dot(a_ab, block_diag(a_ab))
        xv = _dot(jnp.concatenate([a_ak, a_rk, kt_end], axis=0), bd_v)
        akv, y0, n0 = xv[:C], xv[C:2 * C], xv[2 * C:]
        yield
        for _ in range(int(math.log2(C)) - 2):
            sq = _dot(jnp.concatenate([pw, x], axis=0), block_diag(pw))
            pw = sq[:C]
            x = x + sq[C:]
            yield
        x = x + _dot(x, block_diag(pw))
        yield
        lt = _dot(jnp.concatenate([a_rb, bt_end], axis=0), block_diag(x))
        yield
        qymn = _dot(lt, jnp.concatenate([block_diag(a_t), block_diag(akv)], axis=1))
        q_scr[c, gi] = (r_t + qymn[:C, :GL]).astype(BF16)
        y0_scr[c, gi] = qymn[:C, GL:] + y0
        m_scr[c, gi] = (diag_p + qymn[C:, :GL]).astype(BF16)
        n_scr[c, gi] = qymn[C:, GL:] + n0

    def chunk_gen(c):
        b, off = divmod(c * C, RB)
        while blocks[b] is None:
            yield
        blk = blocks[b]
        rows = slice(off, off + C)
        r, k2, v, lw, kk, bvec = (blk[name] for name in ("r", "k2", "v", "lw", "kk", "bvec"))
        lw_c = lw[rows]
        hi, mid, lo = _split3(lw_c)
        cum = (jnp.dot(tri_f, hi, preferred_element_type=F32)
               + jnp.dot(tri_f, mid, preferred_element_type=F32)
               + jnp.dot(tri_f, lo, preferred_element_type=F32))
        e_pos = jnp.exp(cum)
        e_neg = jnp.exp(-cum)
        e_prev = jnp.exp(cum - lw_c)
        p_end = jnp.exp(cum[C - 1:C, :])
        r_t = r[rows] * e_pos
        a_t = -kk[rows] * e_prev
        b_t = bvec[rows] * e_neg
        k_t = k2[rows] * e_neg
        v_c = v[rows]
        yield
        subs = [chunk_terms(c, gi, r_t[:, gl], a_t[:, gl], b_t[:, gl], k_t[:, gl], v_c[:, gl],
                            p_end[:, gl]) for gi, gl in enumerate(group_lanes)]
        while subs:
            subs = [sub for sub in subs if next(sub, True) is None]
            yield
        chunk_done[c] = True

    chunk_done = [False] * n_chunks
    scanned = [False] * n_chunks
    states = [state_ref[gi] for gi in range(NG)]

    def scan_gen():
        for c in range(n_chunks):
            while not chunk_done[c]:
                yield
            rows = slice(c * C, (c + 1) * C)
            for gi, gl in enumerate(group_lanes):
                res = _dot(jnp.concatenate([q_scr[c, gi], m_scr[c, gi]], axis=0),
                           block_diag(states[gi]))
                yh_scr[rows, gl] = res[:C] + y0_scr[c, gi]
                states[gi] = res[C:] + n_scr[c, gi]
            scanned[c] = True
            yield

    def epilogue_gen(b):
        while not scanned[(b + 1) * RB // C - 1]:
            yield
        rows = slice(b * RB, (b + 1) * RB)
        blk = blocks[b]
        ys = [yh_scr[rows, gl] for gl in group_lanes]
        mean = head_sum(jnp.concatenate(ys, axis=0)) * (1.0 / N)
        ycs = [ys[gi] - mean[gi * RB:(gi + 1) * RB] for gi in range(NG)]
        yield
        var = head_sum(jnp.concatenate([yc * yc for yc in ycs], axis=0)) * (1.0 / N)
        yn = jnp.concatenate([ycs[gi] * lax.rsqrt(var[gi * RB:(gi + 1) * RB] + RWKV_GN_EPS)
                              for gi in range(NG)], axis=1)
        out = yn * gng_ref[...] + gnb_ref[...] + blk["bonus"] * blk["v"]
        y_ref[0, rows, :] = (out * blk["g"]).astype(y_ref.dtype)

    blocks_per_round = RB // C * max(RWKV_CHUNK_STAGGER, 1)
    _run_interleaved(
        [scan_gen()] + [prologue_gen(b) for b in range(n_blocks)]
        + [chunk_gen(c) for c in range(n_chunks)] + [epilogue_gen(b) for b in range(n_blocks)],
        starts=[0] + [b * blocks_per_round for b in range(n_blocks)]
        + [c * RWKV_CHUNK_STAGGER for c in range(n_chunks)] + [0] * n_blocks)
    for gi in range(NG):
        state_ref[gi] = states[gi]
    carry_ref[...] = proj[n_blocks - 1][RB - 1:RB]


def _rwkv_mix(h, pre_g, w_proj, mu, w0, w_up, a0, a_up, g_up, k_k, k_a, r_k, gn_g, gn_b, *,
              ts):
    B, S, D = h.shape
    C, H, N, W = RWKV_CHUNK, RWKV_HEADS, RWKV_HEAD_DIM, RWKV_WIDTH
    assert S % ts == 0 and ts % min(RWKV_BLOCK_ROWS, ts) == 0 and RWKV_BLOCK_ROWS % C == 0
    n_chunks = ts // C
    NG, GL = H // RWKV_HEAD_GROUP, RWKV_HEAD_GROUP * N
    row = lambda t: t.reshape(1, -1).astype(F32)
    full = lambda shape: pl.BlockSpec(shape, lambda b, i: (0,) * len(shape),
                                      pipeline_mode=pl.Buffered(1))
    return pl.pallas_call(
        _rwkv_kernel,
        out_shape=jax.ShapeDtypeStruct((B, S, W), BF16),
        grid=(B, S // ts),
        in_specs=[
            pl.BlockSpec((1, ts, D), lambda b, i: (b, i, 0)),
            full((1, D)), full((D, RWKV_COLS)),
            full((1, RWKV_COLS)), full((1, W)), full((DECAY_LORA, W)), full((1, W)),
            full((AAA_LORA, W)), full((GATE_LORA, W)), full((1, W)), full((1, W)),
            full((1, W)), full((1, W)), full((1, W)),
        ],
        out_specs=pl.BlockSpec((1, ts, W), lambda b, i: (b, i, 0)),
        scratch_shapes=[
            pltpu.VMEM((1, RWKV_COLS), F32),
            pltpu.VMEM((NG, N, GL), F32),
            pltpu.VMEM((n_chunks, NG, C, GL), BF16),
            pltpu.VMEM((n_chunks, NG, C, GL), F32),
            pltpu.VMEM((n_chunks, NG, N, GL), BF16),
            pltpu.VMEM((n_chunks, NG, N, GL), F32),
            pltpu.VMEM((ts, W), F32),
        ],
        compiler_params=pltpu.CompilerParams(
            dimension_semantics=("arbitrary", "arbitrary"),
            vmem_limit_bytes=VMEM_LIMIT_BYTES),
        name="rwkv7_mix",
    )(h, row(pre_g), w_proj, row(mu), row(w0), w_up.astype(BF16), row(a0), a_up.astype(BF16),
      g_up.astype(BF16), row(k_k), row(k_a), row(r_k), row(gn_g), row(gn_b))


D_MODEL = 1024
D_FF = 2816
FFN_CHUNK = 256
FFN_SUBTILES = 2
FFN_STAGGER = 9


def _const_spec(shape):
    return pl.BlockSpec(shape, lambda *_: (0,) * len(shape), pipeline_mode=pl.Buffered(1))


def _swiglu_half_step(h_ref, pre_g, w_in_ref, w_out_ref, post_g, o_ref, rows):
    h = h_ref[rows, :]
    xn = _rms_norm(h, pre_g).astype(BF16)
    acc = jnp.zeros(h.shape, F32)
    yield
    for c in range(D_FF // FFN_CHUNK):
        lo = c * FFN_CHUNK
        gate = jnp.dot(xn, w_in_ref[:, lo:lo + FFN_CHUNK], preferred_element_type=F32)
        up = jnp.dot(xn, w_in_ref[:, D_FF + lo:D_FF + lo + FFN_CHUNK],
                     preferred_element_type=F32)
        act = (gate * _sigmoid(gate) * up).astype(BF16)
        acc = acc + jnp.dot(act, w_out_ref[lo:lo + FFN_CHUNK, :], preferred_element_type=F32)
        yield
    o_ref[rows, :] = h + HALF_STEP * _rms_norm(acc, post_g)


def _swiglu_tile(h_ref, pre_g, w_in_ref, w_out_ref, post_g, o_ref):
    sub = h_ref.shape[0] // FFN_SUBTILES
    _run_interleaved(
        [_swiglu_half_step(h_ref, pre_g, w_in_ref, w_out_ref, post_g, o_ref,
                           slice(s * sub, (s + 1) * sub)) for s in range(FFN_SUBTILES)],
        starts=[s * FFN_STAGGER for s in range(FFN_SUBTILES)])


def _ffn_kernel(col_splits, h_ref, pre_g_ref, w_in_ref, w_out_ref, post_g_ref, *refs):
    n_jobs = len(col_splits)
    cast_in, o_ref, cast_out = refs[:n_jobs], refs[n_jobs], refs[n_jobs + 1:]
    k = 0
    for src_ref, splits in zip(cast_in, col_splits):
        for lo, hi in splits:
            cast_out[k][...] = src_ref[:, lo:hi].astype(BF16)
            k += 1
    _swiglu_tile(h_ref, pre_g_ref[...], w_in_ref, w_out_ref, post_g_ref[...], o_ref)


def _slab_rows(n_rows, n_steps):
    rows = BF16_SUBLANES
    while n_rows % rows or rows * n_steps < n_rows:
        rows += BF16_SUBLANES
    return rows


def _ffn(h, pre_g, w_in, w_out, post_g, *, tm, cast_jobs=()):
    T, D = h.shape
    assert T % tm == 0 and tm % (FFN_SUBTILES * BF16_SUBLANES) == 0
    n_steps = T // tm
    tile = pl.BlockSpec((tm, D), lambda i: (i, 0))
    cast_in_specs, cast_out_specs, cast_out_shapes = [], [], []
    for w, splits in cast_jobs:
        rows = _slab_rows(w.shape[0], n_steps)
        last = w.shape[0] // rows - 1
        index = lambda i, last=last: (jnp.minimum(i, last), 0)
        cast_in_specs.append(pl.BlockSpec((rows, w.shape[1]), index))
        for lo, hi in splits:
            cast_out_specs.append(pl.BlockSpec((rows, hi - lo), index))
            cast_out_shapes.append(jax.ShapeDtypeStruct((w.shape[0], hi - lo), BF16))
    outs = pl.pallas_call(
        functools.partial(_ffn_kernel, tuple(tuple(s) for _, s in cast_jobs)),
        out_shape=[jax.ShapeDtypeStruct((T, D), F32)] + cast_out_shapes,
        grid=(n_steps,),
        in_specs=[tile, _const_spec((1, D)), _const_spec((D, 2 * D_FF)),
                  _const_spec((D_FF, D)), _const_spec((1, D))] + cast_in_specs,
        out_specs=[tile] + cast_out_specs,
        compiler_params=pltpu.CompilerParams(
            dimension_semantics=("arbitrary",), vmem_limit_bytes=VMEM_LIMIT_BYTES),
        name="swiglu_half_step",
    )(h, pre_g.reshape(1, D), w_in.astype(BF16), w_out.astype(BF16), post_g.reshape(1, D),
      *[w for w, _ in cast_jobs])
    return outs


ATTN_COLS = SWA_COLS + MEM_WIDTH
BRANCH_COLS = RWKV_COLS + ATTN_COLS

def _rope_tables(pos_row):
    half = ROPE_DIM // 2
    f = lax.broadcasted_iota(jnp.int32, (half, 1), 0).astype(F32)
    inv_freq = jnp.exp(f * (-2.0 / ROPE_DIM * math.log(ROPE_THETA)))
    ang = inv_freq * pos_row
    cs = jnp.concatenate([jnp.cos(ang), jnp.sin(ang)], axis=0)
    src = lax.broadcasted_iota(jnp.int32, (2 * half, 256), 0)
    col = lax.broadcasted_iota(jnp.int32, (2 * half, 256), 1)
    d = col % SWA_HEAD_DIM
    is_sin = col >= 128
    want = jnp.where(is_sin, half, 0) + d % half
    sign = jnp.where(is_sin & (d < half), -1.0, 1.0)
    sel = jnp.where((src == want) & (d < ROPE_DIM), sign, 0.0).astype(BF16)
    hi = cs.astype(BF16)
    lo = (cs - hi.astype(F32)).astype(BF16)
    tab = _dot_tn(hi, sel) + _dot_tn(lo, sel)
    lane_d = lax.broadcasted_iota(jnp.int32, (1, 128), 1) % SWA_HEAD_DIM
    c = tab[:, :128] + jnp.where(lane_d < ROPE_DIM, 0.0, 1.0)
    return c, tab[:, 128:]


def _apply_rope(x, c, s):
    half = ROPE_DIM // 2
    low = (lax.broadcasted_iota(jnp.int32, (1, 128), 1) % SWA_HEAD_DIM) < half
    cols = []
    for j in range(x.shape[1] // 128):
        xj = x[:, j * 128:(j + 1) * 128]
        partner = jnp.where(low, pltpu.roll(xj, 128 - half, axis=1),
                            pltpu.roll(xj, half, axis=1))
        cols.append(xj * c + partner * s)
    return cols[0] if len(cols) == 1 else jnp.concatenate(cols, axis=1)


def _kv_lane_variants(x):
    hd = SWA_HEAD_DIM
    low = lax.broadcasted_iota(jnp.int32, (1, 128), 1) < hd
    xs = pltpu.roll(x, hd, axis=1)
    return [[jnp.where(low, x, 0.0).astype(BF16), jnp.where(low, 0.0, xs).astype(BF16)],
            [jnp.where(low, xs, 0.0).astype(BF16), jnp.where(low, 0.0, x).astype(BF16)]]


def _swa_attention(q, k, v, kc_ref, vc_ref, sinks_ref, first_tile):
    tq = q.shape[0]
    blk = WINDOW
    nb = tq // blk
    kv = _kv_lane_variants(k)
    vv = _kv_lane_variants(v)
    qi = lax.broadcasted_iota(jnp.int32, (blk, blk), 0)
    kj = lax.broadcasted_iota(jnp.int32, (blk, blk), 1)
    from_prev = kj > qi
    prev_bias = jnp.where(first_tile, NEG_INF, 0.0)

    def window(var, ref, g, e, j):
        prev = ref[2 * g + e] if j == 0 else var[g][e][(j - 1) * blk:j * blk]
        return [prev, var[g][e][j * blk:(j + 1) * blk]]

    pair_outs = [[None] * (2 * SWA_KV_HEADS) for _ in range(nb)]

    def block_group(j, g):
        rs = slice(j * blk, (j + 1) * blk)
        k_rhs = jnp.concatenate(window(kv, kc_ref, g, 0, j) + window(kv, kc_ref, g, 1, j),
                                axis=0)
        v_rhs = jnp.concatenate(window(vv, vc_ref, g, 0, j) + window(vv, vc_ref, g, 1, j),
                                axis=0)
        q_rows = jnp.concatenate([q[rs, (2 * g + r) * 128:(2 * g + r + 1) * 128]
                                  for r in range(2)], axis=0)
        s_all = _dot_nt(q_rows, k_rhs)
        yield
        probs = []
        for r in range(2):
            row_p = []
            for e in range(2):
                sink = sinks_ref[SWA_GROUP * g + 2 * r + e]
                s_prev = s_all[r * blk:(r + 1) * blk, 2 * e * blk:(2 * e + 1) * blk]
                s_cur = s_all[r * blk:(r + 1) * blk, (2 * e + 1) * blk:(2 * e + 2) * blk]
                if j == 0:
                    s_prev = s_prev + prev_bias
                s = jnp.where(from_prev, s_prev, s_cur)
                m = jnp.maximum(jnp.max(s, axis=-1, keepdims=True), sink)
                ex = jnp.exp(s - m)
                denom = jnp.sum(ex, axis=-1, keepdims=True) + jnp.exp(sink - m)
                pn = ex * (1.0 / denom)
                row_p += [jnp.where(from_prev, pn, 0.0).astype(BF16),
                          jnp.where(from_prev, 0.0, pn).astype(BF16)]
            probs.append(jnp.concatenate(row_p, axis=1))
            yield
        o = _dot(jnp.concatenate(probs, axis=0), v_rhs)
        pair_outs[j][2 * g] = o[:blk]
        pair_outs[j][2 * g + 1] = o[blk:]

    _run_interleaved([block_group(j, g) for j in range(nb) for g in range(SWA_KV_HEADS)])
    outs = [jnp.concatenate(po, axis=1) for po in pair_outs]
    for g in range(SWA_KV_HEADS):
        for e in range(2):
            kc_ref[2 * g + e] = kv[g][e][tq - blk:tq]
            vc_ref[2 * g + e] = vv[g][e][tq - blk:tq]
    return jnp.concatenate(outs, axis=0)


def _attn_kernel(sinks_ref, h_ref, pos_ref, g_ref, w_ref, km_ref, vm_ref, yb_ref, yc_ref,
                 kc_ref, vc_ref):
    first_tile = pl.program_id(1) == 0

    @pl.when(first_tile)
    def _():
        kc_ref[...] = jnp.zeros_like(kc_ref)
        vc_ref[...] = jnp.zeros_like(vc_ref)

    u = _rms_norm(h_ref[0], g_ref[...]).astype(BF16)
    p = jnp.dot(u, w_ref[...], preferred_element_type=F32)
    c, s = _rope_tables(pos_ref[0].astype(F32))
    q = (_apply_rope(p[:, 0:SWA_Q_WIDTH], c, s) * (SWA_HEAD_DIM ** -0.5)).astype(BF16)
    k = _apply_rope(p[:, SWA_Q_WIDTH:SWA_Q_WIDTH + SWA_KV_WIDTH], c, s)
    v = p[:, SWA_Q_WIDTH + SWA_KV_WIDTH:SWA_COLS]
    yb_ref[0] = _swa_attention(q, k, v, kc_ref, vc_ref, sinks_ref,
                               first_tile).astype(yb_ref.dtype)
    yc_ref[0] = _mem_attention(p[:, SWA_COLS:ATTN_COLS], km_ref[0],
                               vm_ref[0]).astype(yc_ref.dtype)


def _attn_branches(h, positions, sinks, g, w_attn, km, vm, *, tq):
    B, S, D = h.shape
    M = km.shape[1]
    assert S % tq == 0 and tq % WINDOW == 0
    n_carry = 2 * SWA_KV_HEADS
    const = lambda shape: pl.BlockSpec(shape, lambda b, i, *_: (0,) * len(shape),
                                       pipeline_mode=pl.Buffered(1))
    return pl.pallas_call(
        _attn_kernel,
        out_shape=(jax.ShapeDtypeStruct((B, S, SWA_Q_WIDTH), BF16),
                   jax.ShapeDtypeStruct((B, S, MEM_WIDTH), BF16)),
        grid_spec=pltpu.PrefetchScalarGridSpec(
            num_scalar_prefetch=1,
            grid=(B, S // tq),
            in_specs=[pl.BlockSpec((1, tq, D), lambda b, i, *_: (b, i, 0)),
                      pl.BlockSpec((1, 1, tq), lambda b, i, *_: (b, 0, i)),
                      const((1, D)), const((D, ATTN_COLS)),
                      pl.BlockSpec((1, M, MEM_WIDTH), lambda b, i, *_: (b, 0, 0)),
                      pl.BlockSpec((1, M, MEM_WIDTH), lambda b, i, *_: (b, 0, 0))],
            out_specs=(pl.BlockSpec((1, tq, SWA_Q_WIDTH), lambda b, i, *_: (b, i, 0)),
                       pl.BlockSpec((1, tq, MEM_WIDTH), lambda b, i, *_: (b, i, 0))),
            scratch_shapes=[pltpu.VMEM((n_carry, WINDOW, 128), BF16),
                            pltpu.VMEM((n_carry, WINDOW, 128), BF16)]),
        compiler_params=pltpu.CompilerParams(
            dimension_semantics=("arbitrary", "arbitrary"),
            vmem_limit_bytes=VMEM_LIMIT_BYTES),
        name="attn_branches",
    )(sinks.astype(F32), h, positions.reshape(B, 1, S), g.reshape(1, D), w_attn, km, vm)


def _mem_kv_kernel(mem_ref, g_ref, w_ref, km_ref, vm_ref):
    mn = _rms_norm(mem_ref[0], g_ref[...]).astype(BF16)
    km_ref[0] = jnp.dot(mn, w_ref[:, 0:MEM_WIDTH],
                        preferred_element_type=F32).astype(km_ref.dtype)
    vm_ref[0] = jnp.dot(mn, w_ref[:, MEM_WIDTH:2 * MEM_WIDTH],
                        preferred_element_type=F32).astype(vm_ref.dtype)


def _mem_kv(mem, g, w_kv):
    B, M, D = mem.shape
    out = jax.ShapeDtypeStruct((B, M, MEM_WIDTH), BF16)
    return pl.pallas_call(
        _mem_kv_kernel,
        out_shape=(out, out),
        grid=(B,),
        in_specs=[pl.BlockSpec((1, M, D), lambda b: (b, 0, 0)), _const_spec((1, D)),
                  _const_spec((D, 2 * MEM_WIDTH))],
        out_specs=(pl.BlockSpec((1, M, MEM_WIDTH), lambda b: (b, 0, 0)),
                   pl.BlockSpec((1, M, MEM_WIDTH), lambda b: (b, 0, 0))),
        compiler_params=pltpu.CompilerParams(
            dimension_semantics=("arbitrary",), vmem_limit_bytes=VMEM_LIMIT_BYTES),
        name="mem_kv_proj",
    )(mem, g.reshape(1, D), w_kv.astype(BF16))


def _mem_attention(q, km, vm):
    hd = MEM_HEAD_DIM
    q = (q * (hd ** -0.5)).astype(BF16)
    outs = []
    for h in range(MEM_HEADS):
        hs = slice(h * hd, (h + 1) * hd)
        s = _dot_nt(q[:, hs], km[:, hs])
        e = jnp.exp(s - jnp.max(s, axis=-1, keepdims=True))
        outs.append(_dot(e, vm[:, hs]) / jnp.sum(e, axis=-1, keepdims=True))
    return jnp.concatenate(outs, axis=1)


MERGE_SUBTILES = 2
MERGE_STAGGER = 3


def _merge_kernel(h_ref, ya_ref, yb_ref, yc_ref, pre_g_ref, wg_ref, gb_ref, wb_ref, wo_ref,
                  post_g_ref, o_ref):
    D = D_MODEL

    def sub_tile(rows):
        h = h_ref[rows, :]
        u = _rms_norm(h, pre_g_ref[...]).astype(BF16)
        merged = jnp.zeros(h.shape, F32)
        yield
        for i, y_ref in enumerate((ya_ref, yb_ref, yc_ref)):
            logits = jnp.dot(u, wg_ref[:, i * D:(i + 1) * D], preferred_element_type=F32)
            gate = _sigmoid(logits + gb_ref[:, i * D:(i + 1) * D])
            merged = merged + gate * jnp.dot(y_ref[rows, :], wb_ref[i],
                                             preferred_element_type=F32)
            yield
        out = jnp.dot(merged.astype(BF16), wo_ref[...], preferred_element_type=F32)
        yield
        o_ref[rows, :] = h + _rms_norm(out, post_g_ref[...])

    sub = h_ref.shape[0] // MERGE_SUBTILES
    _run_interleaved([sub_tile(slice(s * sub, (s + 1) * sub)) for s in range(MERGE_SUBTILES)],
                     starts=[s * MERGE_STAGGER for s in range(MERGE_SUBTILES)])


def _merge(h, y_a, y_b, y_c, pre_g, w_gate, gate_b, w_branch, w_out, post_g, *, tm):
    T, D = h.shape
    assert T % tm == 0 and tm % (MERGE_SUBTILES * BF16_SUBLANES) == 0
    tile = pl.BlockSpec((tm, D), lambda i: (i, 0))
    ytile = pl.BlockSpec((tm, BRANCH_WIDTH), lambda i: (i, 0))
    return pl.pallas_call(
        _merge_kernel,
        out_shape=jax.ShapeDtypeStruct((T, D), F32),
        grid=(T // tm,),
        in_specs=[tile, ytile, ytile, ytile, _const_spec((1, D)),
                  _const_spec((D, N_BRANCH * D)), _const_spec((1, N_BRANCH * D)),
                  _const_spec((N_BRANCH, BRANCH_WIDTH, D)), _const_spec((D, D)),
                  _const_spec((1, D))],
        out_specs=tile,
        compiler_params=pltpu.CompilerParams(
            dimension_semantics=("arbitrary",), vmem_limit_bytes=VMEM_LIMIT_BYTES),
        name="gated_merge",
    )(h, y_a, y_b, y_c, pre_g.reshape(1, D), w_gate, gate_b.reshape(1, N_BRANCH * D),
      w_branch.astype(BF16), w_out.astype(BF16), post_g.reshape(1, D))


def kernel(x, mem, positions, ffn1_pre_g, ffn1_w_in, ffn1_w_out, ffn1_post_g, mix_pre_g, w_in, gate_b, rwkv_mu, rwkv_w0, rwkv_w_up, rwkv_a0, rwkv_a_up, rwkv_g_up, rwkv_k_k, rwkv_k_a, rwkv_r_k, rwkv_gn_g, rwkv_gn_b, swa_sinks, mem_norm_g, mem_w_kv, w_branch, w_out, mix_post_g, ffn2_pre_g, ffn2_w_in, ffn2_w_out, ffn2_post_g):
    B, S, D = x.shape
    T = B * S
    tm = min(1024, T)
    ts = min(1024, S)
    h = x.reshape(T, D)
    for l in range(ffn1_pre_g.shape[0]):
        whole = lambda w: ((0, w.shape[1]),)
        cast_jobs = (
            (w_in[l], ((0, RWKV_COLS), (RWKV_COLS, BRANCH_COLS), (BRANCH_COLS, w_in.shape[2]))),
            (mem_w_kv[l], whole(mem_w_kv[l])),
            (w_branch[l].reshape(N_BRANCH * BRANCH_WIDTH, D), ((0, D),)),
            (w_out[l], whole(w_out[l])),
            (ffn2_w_in[l], whole(ffn2_w_in[l])),
            (ffn2_w_out[l], whole(ffn2_w_out[l])),
        )
        (h, w_rwkv, w_attn, w_gate, w_kv_bf, w_branch_bf, w_out_bf, ffn2_w_in_bf,
         ffn2_w_out_bf) = _ffn(h, ffn1_pre_g[l], ffn1_w_in[l], ffn1_w_out[l], ffn1_post_g[l],
                               tm=tm, cast_jobs=cast_jobs)

        h3 = h.reshape(B, S, D)
        y_a = _rwkv_mix(h3, mix_pre_g[l], w_rwkv, rwkv_mu[l], rwkv_w0[l],
                        rwkv_w_up[l], rwkv_a0[l], rwkv_a_up[l], rwkv_g_up[l], rwkv_k_k[l],
                        rwkv_k_a[l], rwkv_r_k[l].reshape(-1), rwkv_gn_g[l], rwkv_gn_b[l], ts=ts)
        km, vm = _mem_kv(mem, mem_norm_g[l], w_kv_bf)
        y_b, y_c = _attn_branches(h3, positions, swa_sinks[l], mix_pre_g[l], w_attn, km, vm,
                                  tq=ts)
        h = _merge(h, y_a.reshape(T, -1), y_b.reshape(T, -1), y_c.reshape(T, -1), mix_pre_g[l],
                   w_gate, gate_b[l], w_branch_bf.reshape(N_BRANCH, BRANCH_WIDTH, D), w_out_bf,
                   mix_post_g[l], tm=tm)

        h, = _ffn(h, ffn2_pre_g[l], ffn2_w_in_bf, ffn2_w_out_bf, ffn2_post_g[l], tm=tm)
    return h.reshape(B, S, D)
```

```python
import functools
import math

import jax
import jax.numpy as jnp
from jax import lax
from jax.experimental import pallas as pl
from jax.experimental.pallas import tpu as pltpu

F32 = jnp.float32
BF16 = jnp.bfloat16

LANES = 128
BF16_SUBLANES = 16
MXU_WIDTH = 256
VMEM_LIMIT_BYTES = 56 * 1024 * 1024
TOKEN_TILE = 1024

NORM_EPS = 1e-6
HALF_STEP = 0.5

RWKV_HEADS = 8
RWKV_HEAD_DIM = 64
RWKV_WIDTH = RWKV_HEADS * RWKV_HEAD_DIM
DECAY_LORA = 64
AAA_LORA = 64
GATE_LORA = 128
RWKV_COLS = 3 * RWKV_WIDTH + DECAY_LORA + AAA_LORA + GATE_LORA
RWKV_GN_EPS = 64e-5
RWKV_CHUNK = 64
RWKV_HEAD_GROUP = MXU_WIDTH // RWKV_HEAD_DIM
assert RWKV_CHUNK == RWKV_HEAD_DIM and RWKV_HEADS % RWKV_HEAD_GROUP == 0
RWKV_CHUNK_STAGGER = 1
RWKV_BLOCK_ROWS = 256

SWA_Q_HEADS = 8
SWA_KV_HEADS = 2
SWA_GROUP = SWA_Q_HEADS // SWA_KV_HEADS
SWA_HEAD_DIM = 64
SWA_Q_WIDTH = SWA_Q_HEADS * SWA_HEAD_DIM
SWA_KV_WIDTH = SWA_KV_HEADS * SWA_HEAD_DIM
SWA_COLS = SWA_Q_WIDTH + 2 * SWA_KV_WIDTH
WINDOW = 128
ROPE_THETA = 500000.0
ROPE_DIM = SWA_HEAD_DIM // 4
NEG_INF = -1e30

MEM_HEADS = 4
MEM_HEAD_DIM = 128
MEM_WIDTH = MEM_HEADS * MEM_HEAD_DIM

N_BRANCH = 3
BRANCH_WIDTH = 512


def _dot(a, b):
    return jnp.dot(a.astype(BF16), b.astype(BF16), preferred_element_type=F32)


def _dot_nt(a, b):
    return lax.dot_general(a.astype(BF16), b.astype(BF16), (((1,), (1,)), ((), ())),
                           preferred_element_type=F32)


def _dot_tn(a, b):
    return lax.dot_general(a.astype(BF16), b.astype(BF16), (((0,), (0,)), ((), ())),
                           preferred_element_type=F32)


def _split3(x):
    hi = x.astype(BF16)
    r1 = x - hi.astype(F32)
    mid = r1.astype(BF16)
    lo = (r1 - mid.astype(F32)).astype(BF16)
    return hi, mid, lo


def _rms_norm(x, g):
    return x * lax.rsqrt(jnp.mean(x * x, axis=-1, keepdims=True) + NORM_EPS) * g


def _sigmoid(x):
    return 1.0 / (1.0 + jnp.exp(-x))


def _run_interleaved(gens, starts=None):
    waiting = sorted(zip(starts or [0] * len(gens), range(len(gens))))
    active = []
    rnd = 0
    while waiting or active:
        while waiting and waiting[0][0] <= rnd:
            active.append(gens[waiting.pop(0)[1]])
        active = [gen for gen in active if next(gen, True) is None]
        rnd += 1


def _rwkv_kernel(h_ref, pre_g_ref, w_ref, mu_ref, w0_ref, wup_ref, a0_ref, aup_ref, gup_ref,
                 kk_ref, ka_ref, rk_ref, gng_ref, gnb_ref, y_ref, carry_ref, state_ref, q_scr,
                 y0_scr, m_scr, n_scr, yh_scr):
    C, H, N, GH = RWKV_CHUNK, RWKV_HEADS, RWKV_HEAD_DIM, RWKV_HEAD_GROUP
    W = RWKV_WIDTH
    ts = h_ref.shape[1]
    n_chunks = ts // C

    @pl.when(pl.program_id(1) == 0)
    def _():
        carry_ref[...] = jnp.zeros_like(carry_ref)
        state_ref[...] = jnp.zeros_like(state_ref)

    GL = GH * N
    row_blk = lax.broadcasted_iota(jnp.int32, (GL, GL), 0) // N
    col_blk = lax.broadcasted_iota(jnp.int32, (GL, GL), 1) // N
    same_head = (row_blk == col_blk).astype(BF16)
    ti = lax.broadcasted_iota(jnp.int32, (C, GL), 0)
    si = lax.broadcasted_iota(jnp.int32, (C, GL), 1) % N
    cat_incl = ti >= si
    cat_strict = ti > si
    cat_eye = ti == si
    ci = lax.broadcasted_iota(jnp.int32, (C, C), 0)
    cj = lax.broadcasted_iota(jnp.int32, (C, C), 1)
    tri_f = (ci >= cj).astype(BF16)

    heads_per_vreg = LANES // N
    lane_head = lax.broadcasted_iota(jnp.int32, (C, LANES), 1) // N
    keep_head = [(lane_head == e).astype(BF16) for e in range(heads_per_vreg)]
    zero_cols = jnp.zeros((C, LANES), BF16)

    def block_diag(x):
        xb = x.astype(BF16)
        row_blocks = []
        for h in range(GH):
            col, e = divmod(h, heads_per_vreg)
            cols = [zero_cols] * (GL // LANES)
            cols[col] = xb[:, col * LANES:(col + 1) * LANES] * keep_head[e]
            row_blocks.append(jnp.concatenate(cols, axis=1))
        return jnp.concatenate(row_blocks, axis=0)

    def head_sum(x):
        return jnp.dot(x.astype(BF16), same_head, preferred_element_type=F32)

    NG = H // GH
    group_lanes = [slice(gi * GL, (gi + 1) * GL) for gi in range(NG)]
    RB = min(RWKV_BLOCK_ROWS, ts)
    n_blocks = ts // RB
    blocks = [None] * n_blocks

    proj = [None] * n_blocks

    def prologue_gen(b):
        u = _rms_norm(h_ref[0, b * RB:(b + 1) * RB, :], pre_g_ref[...]).astype(BF16)
        p_b = jnp.dot(u, w_ref[...], preferred_element_type=F32)
        proj[b] = p_b
        yield
        prev_row = carry_ref[...] if b == 0 else proj[b - 1][RB - 1:RB]
        row = lax.broadcasted_iota(jnp.int32, p_b.shape, 0)
        p_prev = jnp.where(row == 0, prev_row, pltpu.roll(p_b, 1, axis=0))
        xs = p_b + (p_prev - p_b) * mu_ref[...]
        r = xs[:, 0:W]
        k = xs[:, W:2 * W]
        v = xs[:, 2 * W:3 * W]
        o = 3 * W
        wd = xs[:, o:o + DECAY_LORA]
        ad = xs[:, o + DECAY_LORA:o + DECAY_LORA + AAA_LORA]
        gd = xs[:, o + DECAY_LORA + AAA_LORA:o + DECAY_LORA + AAA_LORA + GATE_LORA]
        z = w0_ref[...] + _dot(jnp.tanh(wd), wup_ref[...])
        lw = -math.exp(-0.5) * _sigmoid(z)
        a = _sigmoid(a0_ref[...] + _dot(ad, aup_ref[...]))
        g = _dot(_sigmoid(gd), gup_ref[...])
        kk_raw = k * kk_ref[...]
        k2 = k * (1.0 + (a - 1.0) * ka_ref[...])
        rkr = r * k2 * rk_ref[...]
        yield
        sums = head_sum(jnp.concatenate([kk_raw[:, gl] * kk_raw[:, gl] for gl in group_lanes]
                                        + [rkr[:, gl] for gl in group_lanes], axis=0))
        ssq = jnp.concatenate([sums[gi * RB:(gi + 1) * RB] for gi in range(NG)], axis=1)
        bonus = jnp.concatenate([sums[(NG + gi) * RB:(NG + gi + 1) * RB] for gi in range(NG)],
                                axis=1)
        kk = kk_raw * lax.rsqrt(jnp.maximum(ssq, 1e-24))
        blocks[b] = dict(r=r, k2=k2, v=v, lw=lw, kk=kk, bvec=kk * a, g=g, bonus=bonus)

    def chunk_terms(c, gi, r_t, a_t, b_t, k_t, v_c, p_end):
        bd_b = block_diag(b_t)
        bd_k = block_diag(k_t)
        bd_v = block_diag(v_c)
        diag_p = jnp.where(cat_eye, p_end, 0.0)
        gram = _dot_nt(jnp.concatenate([a_t, r_t, diag_p], axis=0),
                       jnp.concatenate([bd_b, bd_k], axis=0))
        yield
        a_ab = jnp.where(cat_strict, gram[:C, :GL], 0.0)
        a_ak = jnp.where(cat_strict, gram[:C, GL:], 0.0)
        a_rb = jnp.where(cat_incl, gram[C:2 * C, :GL], 0.0)
        a_rk = jnp.where(cat_incl, gram[C:2 * C, GL:], 0.0)
        bt_end = gram[2 * C:, :GL]
        kt_end = gram[2 * C:, GL:]
        x = jnp.where(cat_eye, 1.0, a_ab)
        pw = _dot(a_ab, block_diag(a_ab))
        xv = _dot(jnp.concatenate([a_ak, a_rk, kt_end], axis=0), bd_v)
        akv, y0, n0 = xv[:C], xv[C:2 * C], xv[2 * C:]
        yield
        for _ in range(int(math.log2(C)) - 2):
            sq = _dot(jnp.concatenate([pw, x], axis=0), block_diag(pw))
            pw = sq[:C]
            x = x + sq[C:]
            yield
        x = x + _dot(x, block_diag(pw))
        yield
        lt = _dot(jnp.concatenate([a_rb, bt_end], axis=0), block_diag(x))
        yield
        qymn = _dot(lt, jnp.concatenate([block_diag(a_t), block_diag(akv)], axis=1))
        q_scr[c, gi] = (r_t + qymn[:C, :GL]).astype(BF16)
        y0_scr[c, gi] = qymn[:C, GL:] + y0
        m_scr[c, gi] = (diag_p + qymn[C:, :GL]).astype(BF16)
        n_scr[c, gi] = qymn[C:, GL:] + n0

    def chunk_gen(c):
        b, off = divmod(c * C, RB)
        while blocks[b] is None:
            yield
        blk = blocks[b]
        rows = slice(off, off + C)
        r, k2, v, lw, kk, bvec = (blk[name] for name in ("r", "k2", "v", "lw", "kk", "bvec"))
        lw_c = lw[rows]
        hi, mid, lo = _split3(lw_c)
        cum = (jnp.dot(tri_f, hi, preferred_element_type=F32)
               + jnp.dot(tri_f, mid, preferred_element_type=F32)
               + jnp.dot(tri_f, lo, preferred_element_type=F32))
        e_pos = jnp.exp(cum)
        e_neg = jnp.exp(-cum)
        e_prev = jnp.exp(cum - lw_c)
        p_end = jnp.exp(cum[C - 1:C, :])
        r_t = r[rows] * e_pos
        a_t = -kk[rows] * e_prev
        b_t = bvec[rows] * e_neg
        k_t = k2[rows] * e_neg
        v_c = v[rows]
        yield
        subs = [chunk_terms(c, gi, r_t[:, gl], a_t[:, gl], b_t[:, gl], k_t[:, gl], v_c[:, gl],
                            p_end[:, gl]) for gi, gl in enumerate(group_lanes)]
        while subs:
            subs = [sub for sub in subs if next(sub, True) is None]
            yield
        chunk_done[c] = True

    chunk_done = [False] * n_chunks
    scanned = [False] * n_chunks
    states = [state_ref[gi] for gi in range(NG)]

    def scan_gen():
        for c in range(n_chunks):
            while not chunk_done[c]:
                yield
            rows = slice(c * C, (c + 1) * C)
            for gi, gl in enumerate(group_lanes):
                res = _dot(jnp.concatenate([q_scr[c, gi], m_scr[c, gi]], axis=0),
                           block_diag(states[gi]))
                yh_scr[rows, gl] = res[:C] + y0_scr[c, gi]
                states[gi] = res[C:] + n_scr[c, gi]
            scanned[c] = True
            yield

    def epilogue_gen(b):
        while not scanned[(b + 1) * RB // C - 1]:
            yield
        rows = slice(b * RB, (b + 1) * RB)
        blk = blocks[b]
        ys = [yh_scr[rows, gl] for gl in group_lanes]
        mean = head_sum(jnp.concatenate(ys, axis=0)) * (1.0 / N)
        ycs = [ys[gi] - mean[gi * RB:(gi + 1) * RB] for gi in range(NG)]
        yield
        var = head_sum(jnp.concatenate([yc * yc for yc in ycs], axis=0)) * (1.0 / N)
        yn = jnp.concatenate([ycs[gi] * lax.rsqrt(var[gi * RB:(gi + 1) * RB] + RWKV_GN_EPS)
                              for gi in range(NG)], axis=1)
        out = yn * gng_ref[...] + gnb_ref[...] + blk["bonus"] * blk["v"]
        y_ref[0, rows, :] = (out * blk["g"]).astype(y_ref.dtype)

    blocks_per_round = RB // C * max(RWKV_CHUNK_STAGGER, 1)
    _run_interleaved(
        [scan_gen()] + [prologue_gen(b) for b in range(n_blocks)]
        + [chunk_gen(c) for c in range(n_chunks)] + [epilogue_gen(b) for b in range(n_blocks)],
        starts=[0] + [b * blocks_per_round for b in range(n_blocks)]
        + [c * RWKV_CHUNK_STAGGER for c in range(n_chunks)] + [0] * n_blocks)
    for gi in range(NG):
        state_ref[gi] = states[gi]
    carry_ref[...] = proj[n_blocks - 1][RB - 1:RB]


def _rwkv_mix(h, pre_g, w_proj, mu, w0, w_up, a0, a_up, g_up, k_k, k_a, r_k, gn_g, gn_b, *,
              ts):
    B, S, D = h.shape
    C, H, N, W = RWKV_CHUNK, RWKV_HEADS, RWKV_HEAD_DIM, RWKV_WIDTH
    assert S % ts == 0 and ts % min(RWKV_BLOCK_ROWS, ts) == 0 and RWKV_BLOCK_ROWS % C == 0
    n_chunks = ts // C
    NG, GL = H // RWKV_HEAD_GROUP, RWKV_HEAD_GROUP * N
    row = lambda t: t.reshape(1, -1).astype(F32)
    full = lambda shape: pl.BlockSpec(shape, lambda b, i: (0,) * len(shape),
                                      pipeline_mode=pl.Buffered(1))
    return pl.pallas_call(
        _rwkv_kernel,
        out_shape=jax.ShapeDtypeStruct((B, S, W), BF16),
        grid=(B, S // ts),
        in_specs=[
            pl.BlockSpec((1, ts, D), lambda b, i: (b, i, 0)),
            full((1, D)), full((D, RWKV_COLS)),
            full((1, RWKV_COLS)), full((1, W)), full((DECAY_LORA, W)), full((1, W)),
            full((AAA_LORA, W)), full((GATE_LORA, W)), full((1, W)), full((1, W)),
            full((1, W)), full((1, W)), full((1, W)),
        ],
        out_specs=pl.BlockSpec((1, ts, W), lambda b, i: (b, i, 0)),
        scratch_shapes=[
            pltpu.VMEM((1, RWKV_COLS), F32),
            pltpu.VMEM((NG, N, GL), F32),
            pltpu.VMEM((n_chunks, NG, C, GL), BF16),
            pltpu.VMEM((n_chunks, NG, C, GL), F32),
            pltpu.VMEM((n_chunks, NG, N, GL), BF16),
            pltpu.VMEM((n_chunks, NG, N, GL), F32),
            pltpu.VMEM((ts, W), F32),
        ],
        compiler_params=pltpu.CompilerParams(
            dimension_semantics=("arbitrary", "arbitrary"),
            vmem_limit_bytes=VMEM_LIMIT_BYTES),
        name="rwkv7_mix",
    )(h, row(pre_g), w_proj, row(mu), row(w0), w_up.astype(BF16), row(a0), a_up.astype(BF16),
      g_up.astype(BF16), row(k_k), row(k_a), row(r_k), row(gn_g), row(gn_b))


D_MODEL = 1024
D_FF = 2816
FFN_CHUNK = 256
FFN_SUBTILES = 2
FFN_STAGGER = 9


def _const_spec(shape):
    return pl.BlockSpec(shape, lambda *_: (0,) * len(shape), pipeline_mode=pl.Buffered(1))


def _swiglu_half_step(h_ref, pre_g, w_in_ref, w_out_ref, post_g, o_ref, rows):
    h = h_ref[rows, :]
    xn = _rms_norm(h, pre_g).astype(BF16)
    acc = jnp.zeros(h.shape, F32)
    yield
    for c in range(D_FF // FFN_CHUNK):
        lo = c * FFN_CHUNK
        gate = jnp.dot(xn, w_in_ref[:, lo:lo + FFN_CHUNK], preferred_element_type=F32)
        up = jnp.dot(xn, w_in_ref[:, D_FF + lo:D_FF + lo + FFN_CHUNK],
                     preferred_element_type=F32)
        act = (gate * _sigmoid(gate) * up).astype(BF16)
        acc = acc + jnp.dot(act, w_out_ref[lo:lo + FFN_CHUNK, :], preferred_element_type=F32)
        yield
    o_ref[rows, :] = h + HALF_STEP * _rms_norm(acc, post_g)


def _swiglu_tile(h_ref, pre_g, w_in_ref, w_out_ref, post_g, o_ref):
    sub = h_ref.shape[0] // FFN_SUBTILES
    _run_interleaved(
        [_swiglu_half_step(h_ref, pre_g, w_in_ref, w_out_ref, post_g, o_ref,
                           slice(s * sub, (s + 1) * sub)) for s in range(FFN_SUBTILES)],
        starts=[s * FFN_STAGGER for s in range(FFN_SUBTILES)])


def _ffn_kernel(col_splits, h_ref, pre_g_ref, w_in_ref, w_out_ref, post_g_ref, *refs):
    n_jobs = len(col_splits)
    cast_in, o_ref, cast_out = refs[:n_jobs], refs[n_jobs], refs[n_jobs + 1:]
    k = 0
    for src_ref, splits in zip(cast_in, col_splits):
        for lo, hi in splits:
            cast_out[k][...] = src_ref[:, lo:hi].astype(BF16)
            k += 1
    _swiglu_tile(h_ref, pre_g_ref[...], w_in_ref, w_out_ref, post_g_ref[...], o_ref)


def _slab_rows(n_rows, n_steps):
    rows = BF16_SUBLANES
    while n_rows % rows or rows * n_steps < n_rows:
        rows += BF16_SUBLANES
    return rows


def _ffn(h, pre_g, w_in, w_out, post_g, *, tm, cast_jobs=()):
    T, D = h.shape
    assert T % tm == 0 and tm % (FFN_SUBTILES * BF16_SUBLANES) == 0
    n_steps = T // tm
    tile = pl.BlockSpec((tm, D), lambda i: (i, 0))
    cast_in_specs, cast_out_specs, cast_out_shapes = [], [], []
    for w, splits in cast_jobs:
        rows = _slab_rows(w.shape[0], n_steps)
        last = w.shape[0] // rows - 1
        index = lambda i, last=last: (jnp.minimum(i, last), 0)
        cast_in_specs.append(pl.BlockSpec((rows, w.shape[1]), index))
        for lo, hi in splits:
            cast_out_specs.append(pl.BlockSpec((rows, hi - lo), index))
            cast_out_shapes.append(jax.ShapeDtypeStruct((w.shape[0], hi - lo), BF16))
    outs = pl.pallas_call(
        functools.partial(_ffn_kernel, tuple(tuple(s) for _, s in cast_jobs)),
        out_shape=[jax.ShapeDtypeStruct((T, D), F32)] + cast_out_shapes,
        grid=(n_steps,),
        in_specs=[tile, _const_spec((1, D)), _const_spec((D, 2 * D_FF)),
                  _const_spec((D_FF, D)), _const_spec((1, D))] + cast_in_specs,
        out_specs=[tile] + cast_out_specs,
        compiler_params=pltpu.CompilerParams(
            dimension_semantics=("arbitrary",), vmem_limit_bytes=VMEM_LIMIT_BYTES),
        name="swiglu_half_step",
    )(h, pre_g.reshape(1, D), w_in.astype(BF16), w_out.astype(BF16), post_g.reshape(1, D),
      *[w for w, _ in cast_jobs])
    return outs


ATTN_COLS = SWA_COLS + MEM_WIDTH
BRANCH_COLS = RWKV_COLS + ATTN_COLS

def _rope_tables(pos_row):
    half = ROPE_DIM // 2
    f = lax.broadcasted_iota(jnp.int32, (half, 1), 0).astype(F32)
    inv_freq = jnp.exp(f * (-2.0 / ROPE_DIM * math.log(ROPE_THETA)))
    ang = inv_freq * pos_row
    cs = jnp.concatenate([jnp.cos(ang), jnp.sin(ang)], axis=0)
    src = lax.broadcasted_iota(jnp.int32, (2 * half, 2 * LANES), 0)
    col = lax.broadcasted_iota(jnp.int32, (2 * half, 2 * LANES), 1)
    d = col % SWA_HEAD_DIM
    is_sin = col >= LANES
    want = jnp.where(is_sin, half, 0) + d % half
    sign = jnp.where(is_sin & (d < half), -1.0, 1.0)
    sel = jnp.where((src == want) & (d < ROPE_DIM), sign, 0.0).astype(BF16)
    hi = cs.astype(BF16)
    lo = (cs - hi.astype(F32)).astype(BF16)
    tab = _dot_tn(hi, sel) + _dot_tn(lo, sel)
    lane_d = lax.broadcasted_iota(jnp.int32, (1, LANES), 1) % SWA_HEAD_DIM
    c = tab[:, :LANES] + jnp.where(lane_d < ROPE_DIM, 0.0, 1.0)
    return c, tab[:, LANES:]


def _apply_rope(x, c, s):
    half = ROPE_DIM // 2
    low = (lax.broadcasted_iota(jnp.int32, (1, LANES), 1) % SWA_HEAD_DIM) < half
    cols = []
    for j in range(x.shape[1] // LANES):
        xj = x[:, j * LANES:(j + 1) * LANES]
        partner = jnp.where(low, pltpu.roll(xj, LANES - half, axis=1),
                            pltpu.roll(xj, half, axis=1))
        cols.append(xj * c + partner * s)
    return cols[0] if len(cols) == 1 else jnp.concatenate(cols, axis=1)


def _kv_lane_variants(x):
    hd = SWA_HEAD_DIM
    low = lax.broadcasted_iota(jnp.int32, (1, LANES), 1) < hd
    xs = pltpu.roll(x, hd, axis=1)
    return [[jnp.where(low, x, 0.0).astype(BF16), jnp.where(low, 0.0, xs).astype(BF16)],
            [jnp.where(low, xs, 0.0).astype(BF16), jnp.where(low, 0.0, x).astype(BF16)]]


def _swa_attention(q, k, v, kc_ref, vc_ref, sinks_ref, first_tile):
    tq = q.shape[0]
    blk = WINDOW
    nb = tq // blk
    kv = _kv_lane_variants(k)
    vv = _kv_lane_variants(v)
    qi = lax.broadcasted_iota(jnp.int32, (blk, blk), 0)
    kj = lax.broadcasted_iota(jnp.int32, (blk, blk), 1)
    from_prev = kj > qi
    prev_bias = jnp.where(first_tile, NEG_INF, 0.0)

    def window(var, ref, g, e, j):
        prev = ref[2 * g + e] if j == 0 else var[g][e][(j - 1) * blk:j * blk]
        return [prev, var[g][e][j * blk:(j + 1) * blk]]

    pair_outs = [[None] * (2 * SWA_KV_HEADS) for _ in range(nb)]

    def block_group(j, g):
        rs = slice(j * blk, (j + 1) * blk)
        k_rhs = jnp.concatenate(window(kv, kc_ref, g, 0, j) + window(kv, kc_ref, g, 1, j),
                                axis=0)
        v_rhs = jnp.concatenate(window(vv, vc_ref, g, 0, j) + window(vv, vc_ref, g, 1, j),
                                axis=0)
        q_rows = jnp.concatenate([q[rs, (2 * g + r) * LANES:(2 * g + r + 1) * LANES]
                                  for r in range(2)], axis=0)
        s_all = _dot_nt(q_rows, k_rhs)
        yield
        probs = []
        for r in range(2):
            row_p = []
            for e in range(2):
                sink = sinks_ref[SWA_GROUP * g + 2 * r + e]
                s_prev = s_all[r * blk:(r + 1) * blk, 2 * e * blk:(2 * e + 1) * blk]
                s_cur = s_all[r * blk:(r + 1) * blk, (2 * e + 1) * blk:(2 * e + 2) * blk]
                if j == 0:
                    s_prev = s_prev + prev_bias
                s = jnp.where(from_prev, s_prev, s_cur)
                m = jnp.maximum(jnp.max(s, axis=-1, keepdims=True), sink)
                ex = jnp.exp(s - m)
                denom = jnp.sum(ex, axis=-1, keepdims=True) + jnp.exp(sink - m)
                pn = ex * (1.0 / denom)
                row_p += [jnp.where(from_prev, pn, 0.0).astype(BF16),
                          jnp.where(from_prev, 0.0, pn).astype(BF16)]
            probs.append(jnp.concatenate(row_p, axis=1))
            yield
        o = _dot(jnp.concatenate(probs, axis=0), v_rhs)
        pair_outs[j][2 * g] = o[:blk]
        pair_outs[j][2 * g + 1] = o[blk:]

    _run_interleaved([block_group(j, g) for j in range(nb) for g in range(SWA_KV_HEADS)])
    outs = [jnp.concatenate(po, axis=1) for po in pair_outs]
    for g in range(SWA_KV_HEADS):
        for e in range(2):
            kc_ref[2 * g + e] = kv[g][e][tq - blk:tq]
            vc_ref[2 * g + e] = vv[g][e][tq - blk:tq]
    return jnp.concatenate(outs, axis=0)


def _attn_kernel(sinks_ref, h_ref, pos_ref, g_ref, w_ref, km_ref, vm_ref, yb_ref, yc_ref,
                 kc_ref, vc_ref):
    first_tile = pl.program_id(1) == 0

    @pl.when(first_tile)
    def _():
        kc_ref[...] = jnp.zeros_like(kc_ref)
        vc_ref[...] = jnp.zeros_like(vc_ref)

    u = _rms_norm(h_ref[0], g_ref[...]).astype(BF16)
    p = jnp.dot(u, w_ref[...], preferred_element_type=F32)
    c, s = _rope_tables(pos_ref[0].astype(F32))
    q = (_apply_rope(p[:, 0:SWA_Q_WIDTH], c, s) * (SWA_HEAD_DIM ** -0.5)).astype(BF16)
    k = _apply_rope(p[:, SWA_Q_WIDTH:SWA_Q_WIDTH + SWA_KV_WIDTH], c, s)
    v = p[:, SWA_Q_WIDTH + SWA_KV_WIDTH:SWA_COLS]
    yb_ref[0] = _swa_attention(q, k, v, kc_ref, vc_ref, sinks_ref,
                               first_tile).astype(yb_ref.dtype)
    yc_ref[0] = _mem_attention(p[:, SWA_COLS:ATTN_COLS], km_ref[0],
                               vm_ref[0]).astype(yc_ref.dtype)


def _attn_branches(h, positions, sinks, g, w_attn, km, vm, *, tq):
    B, S, D = h.shape
    M = km.shape[1]
    assert S % tq == 0 and tq % WINDOW == 0
    n_carry = 2 * SWA_KV_HEADS
    const = lambda shape: pl.BlockSpec(shape, lambda b, i, *_: (0,) * len(shape),
                                       pipeline_mode=pl.Buffered(1))
    return pl.pallas_call(
        _attn_kernel,
        out_shape=(jax.ShapeDtypeStruct((B, S, SWA_Q_WIDTH), BF16),
                   jax.ShapeDtypeStruct((B, S, MEM_WIDTH), BF16)),
        grid_spec=pltpu.PrefetchScalarGridSpec(
            num_scalar_prefetch=1,
            grid=(B, S // tq),
            in_specs=[pl.BlockSpec((1, tq, D), lambda b, i, *_: (b, i, 0)),
                      pl.BlockSpec((1, 1, tq), lambda b, i, *_: (b, 0, i)),
                      const((1, D)), const((D, ATTN_COLS)),
                      pl.BlockSpec((1, M, MEM_WIDTH), lambda b, i, *_: (b, 0, 0)),
                      pl.BlockSpec((1, M, MEM_WIDTH), lambda b, i, *_: (b, 0, 0))],
            out_specs=(pl.BlockSpec((1, tq, SWA_Q_WIDTH), lambda b, i, *_: (b, i, 0)),
                       pl.BlockSpec((1, tq, MEM_WIDTH), lambda b, i, *_: (b, i, 0))),
            scratch_shapes=[pltpu.VMEM((n_carry, WINDOW, LANES), BF16),
                            pltpu.VMEM((n_carry, WINDOW, LANES), BF16)]),
        compiler_params=pltpu.CompilerParams(
            dimension_semantics=("arbitrary", "arbitrary"),
            vmem_limit_bytes=VMEM_LIMIT_BYTES),
        name="attn_branches",
    )(sinks.astype(F32), h, positions.reshape(B, 1, S), g.reshape(1, D), w_attn, km, vm)


def _mem_kv_kernel(mem_ref, g_ref, w_ref, km_ref, vm_ref):
    mn = _rms_norm(mem_ref[0], g_ref[...]).astype(BF16)
    km_ref[0] = jnp.dot(mn, w_ref[:, 0:MEM_WIDTH],
                        preferred_element_type=F32).astype(km_ref.dtype)
    vm_ref[0] = jnp.dot(mn, w_ref[:, MEM_WIDTH:2 * MEM_WIDTH],
                        preferred_element_type=F32).astype(vm_ref.dtype)


def _mem_kv(mem, g, w_kv):
    B, M, D = mem.shape
    out = jax.ShapeDtypeStruct((B, M, MEM_WIDTH), BF16)
    return pl.pallas_call(
        _mem_kv_kernel,
        out_shape=(out, out),
        grid=(B,),
        in_specs=[pl.BlockSpec((1, M, D), lambda b: (b, 0, 0)), _const_spec((1, D)),
                  _const_spec((D, 2 * MEM_WIDTH))],
        out_specs=(pl.BlockSpec((1, M, MEM_WIDTH), lambda b: (b, 0, 0)),
                   pl.BlockSpec((1, M, MEM_WIDTH), lambda b: (b, 0, 0))),
        compiler_params=pltpu.CompilerParams(
            dimension_semantics=("arbitrary",), vmem_limit_bytes=VMEM_LIMIT_BYTES),
        name="mem_kv_proj",
    )(mem, g.reshape(1, D), w_kv.astype(BF16))


def _mem_attention(q, km, vm):
    hd = MEM_HEAD_DIM
    q = (q * (hd ** -0.5)).astype(BF16)
    outs = []
    for h in range(MEM_HEADS):
        hs = slice(h * hd, (h + 1) * hd)
        s = _dot_nt(q[:, hs], km[:, hs])
        e = jnp.exp(s - jnp.max(s, axis=-1, keepdims=True))
        outs.append(_dot(e, vm[:, hs]) / jnp.sum(e, axis=-1, keepdims=True))
    return jnp.concatenate(outs, axis=1)


MERGE_SUBTILES = 2
MERGE_STAGGER = 3


def _merge_kernel(h_ref, ya_ref, yb_ref, yc_ref, pre_g_ref, wg_ref, gb_ref, wb_ref, wo_ref,
                  post_g_ref, o_ref):
    D = D_MODEL

    def sub_tile(rows):
        h = h_ref[rows, :]
        u = _rms_norm(h, pre_g_ref[...]).astype(BF16)
        merged = jnp.zeros(h.shape, F32)
        yield
        for i, y_ref in enumerate((ya_ref, yb_ref, yc_ref)):
            logits = jnp.dot(u, wg_ref[:, i * D:(i + 1) * D], preferred_element_type=F32)
            gate = _sigmoid(logits + gb_ref[:, i * D:(i + 1) * D])
            merged = merged + gate * jnp.dot(y_ref[rows, :], wb_ref[i],
                                             preferred_element_type=F32)
            yield
        out = jnp.dot(merged.astype(BF16), wo_ref[...], preferred_element_type=F32)
        yield
        o_ref[rows, :] = h + _rms_norm(out, post_g_ref[...])

    sub = h_ref.shape[0] // MERGE_SUBTILES
    _run_interleaved([sub_tile(slice(s * sub, (s + 1) * sub)) for s in range(MERGE_SUBTILES)],
                     starts=[s * MERGE_STAGGER for s in range(MERGE_SUBTILES)])


def _merge(h, y_a, y_b, y_c, pre_g, w_gate, gate_b, w_branch, w_out, post_g, *, tm):
    T, D = h.shape
    assert T % tm == 0 and tm % (MERGE_SUBTILES * BF16_SUBLANES) == 0
    tile = pl.BlockSpec((tm, D), lambda i: (i, 0))
    ytile = pl.BlockSpec((tm, BRANCH_WIDTH), lambda i: (i, 0))
    return pl.pallas_call(
        _merge_kernel,
        out_shape=jax.ShapeDtypeStruct((T, D), F32),
        grid=(T // tm,),
        in_specs=[tile, ytile, ytile, ytile, _const_spec((1, D)),
                  _const_spec((D, N_BRANCH * D)), _const_spec((1, N_BRANCH * D)),
                  _const_spec((N_BRANCH, BRANCH_WIDTH, D)), _const_spec((D, D)),
                  _const_spec((1, D))],
        out_specs=tile,
        compiler_params=pltpu.CompilerParams(
            dimension_semantics=("arbitrary",), vmem_limit_bytes=VMEM_LIMIT_BYTES),
        name="gated_merge",
    )(h, y_a, y_b, y_c, pre_g.reshape(1, D), w_gate, gate_b.reshape(1, N_BRANCH * D),
      w_branch.astype(BF16), w_out.astype(BF16), post_g.reshape(1, D))


def kernel(x, mem, positions, ffn1_pre_g, ffn1_w_in, ffn1_w_out, ffn1_post_g, mix_pre_g, w_in, gate_b, rwkv_mu, rwkv_w0, rwkv_w_up, rwkv_a0, rwkv_a_up, rwkv_g_up, rwkv_k_k, rwkv_k_a, rwkv_r_k, rwkv_gn_g, rwkv_gn_b, swa_sinks, mem_norm_g, mem_w_kv, w_branch, w_out, mix_post_g, ffn2_pre_g, ffn2_w_in, ffn2_w_out, ffn2_post_g):
    B, S, D = x.shape
    T = B * S
    tm = min(TOKEN_TILE, T)
    ts = min(TOKEN_TILE, S)
    h = x.reshape(T, D)
    for l in range(ffn1_pre_g.shape[0]):
        whole = lambda w: ((0, w.shape[1]),)
        cast_jobs = (
            (w_in[l], ((0, RWKV_COLS), (RWKV_COLS, BRANCH_COLS), (BRANCH_COLS, w_in.shape[2]))),
            (mem_w_kv[l], whole(mem_w_kv[l])),
            (w_branch[l].reshape(N_BRANCH * BRANCH_WIDTH, D), ((0, D),)),
            (w_out[l], whole(w_out[l])),
            (ffn2_w_in[l], whole(ffn2_w_in[l])),
            (ffn2_w_out[l], whole(ffn2_w_out[l])),
        )
        (h, w_rwkv, w_attn, w_gate, w_kv_bf, w_branch_bf, w_out_bf, ffn2_w_in_bf,
         ffn2_w_out_bf) = _ffn(h, ffn1_pre_g[l], ffn1_w_in[l], ffn1_w_out[l], ffn1_post_g[l],
                               tm=tm, cast_jobs=cast_jobs)

        h3 = h.reshape(B, S, D)
        y_a = _rwkv_mix(h3, mix_pre_g[l], w_rwkv, rwkv_mu[l], rwkv_w0[l],
                        rwkv_w_up[l], rwkv_a0[l], rwkv_a_up[l], rwkv_g_up[l], rwkv_k_k[l],
                        rwkv_k_a[l], rwkv_r_k[l].reshape(-1), rwkv_gn_g[l], rwkv_gn_b[l], ts=ts)
        km, vm = _mem_kv(mem, mem_norm_g[l], w_kv_bf)
        y_b, y_c = _attn_branches(h3, positions, swa_sinks[l], mix_pre_g[l], w_attn, km, vm,
                                  tq=ts)
        h = _merge(h, y_a.reshape(T, -1), y_b.reshape(T, -1), y_c.reshape(T, -1), mix_pre_g[l],
                   w_gate, gate_b[l], w_branch_bf.reshape(N_BRANCH, BRANCH_WIDTH, D), w_out_bf,
                   mix_post_g[l], tm=tm)

        h, = _ffn(h, ffn2_pre_g[l], ffn2_w_in_bf, ffn2_w_out_bf, ffn2_post_g[l], tm=tm)
    return h.reshape(B, S, D)
```

```python
import functools
import math

import jax
import jax.numpy as jnp
from jax import lax
from jax.experimental import pallas as pl
from jax.experimental.pallas import tpu as pltpu

F32 = jnp.float32
BF16 = jnp.bfloat16

LANES = 128
BF16_SUBLANES = 16
MXU_WIDTH = 256
VMEM_LIMIT_BYTES = 56 * 1024 * 1024
TOKEN_TILE = 1024

NORM_EPS = 1e-6
HALF_STEP = 0.5

RWKV_HEADS = 8
RWKV_HEAD_DIM = 64
RWKV_WIDTH = RWKV_HEADS * RWKV_HEAD_DIM
DECAY_LORA = 64
AAA_LORA = 64
GATE_LORA = 128
RWKV_COLS = 3 * RWKV_WIDTH + DECAY_LORA + AAA_LORA + GATE_LORA
RWKV_GN_EPS = 64e-5
RWKV_CHUNK = 64
RWKV_HEAD_GROUP = MXU_WIDTH // RWKV_HEAD_DIM
assert RWKV_CHUNK == RWKV_HEAD_DIM and RWKV_HEADS % RWKV_HEAD_GROUP == 0
RWKV_CHUNK_STAGGER = 1
RWKV_BLOCK_ROWS = 256

SWA_Q_HEADS = 8
SWA_KV_HEADS = 2
SWA_GROUP = SWA_Q_HEADS // SWA_KV_HEADS
SWA_HEAD_DIM = 64
SWA_Q_WIDTH = SWA_Q_HEADS * SWA_HEAD_DIM
SWA_KV_WIDTH = SWA_KV_HEADS * SWA_HEAD_DIM
SWA_COLS = SWA_Q_WIDTH + 2 * SWA_KV_WIDTH
WINDOW = 128
ROPE_THETA = 500000.0
ROPE_DIM = SWA_HEAD_DIM // 4
NEG_INF = -1e30

MEM_HEADS = 4
MEM_HEAD_DIM = 128
MEM_WIDTH = MEM_HEADS * MEM_HEAD_DIM

N_BRANCH = 3
BRANCH_WIDTH = 512


def _dot(a, b):
    return jnp.dot(a.astype(BF16), b.astype(BF16), preferred_element_type=F32)


def _dot_nt(a, b):
    return lax.dot_general(a.astype(BF16), b.astype(BF16), (((1,), (1,)), ((), ())),
                           preferred_element_type=F32)


def _dot_tn(a, b):
    return lax.dot_general(a.astype(BF16), b.astype(BF16), (((0,), (0,)), ((), ())),
                           preferred_element_type=F32)


def _split3(x):
    hi = x.astype(BF16)
    r1 = x - hi.astype(F32)
    mid = r1.astype(BF16)
    lo = (r1 - mid.astype(F32)).astype(BF16)
    return hi, mid, lo


def _rms_norm(x, g):
    return x * lax.rsqrt(jnp.mean(x * x, axis=-1, keepdims=True) + NORM_EPS) * g


def _sigmoid(x):
    return 1.0 / (1.0 + jnp.exp(-x))


def _run_interleaved(gens, starts=None):
    waiting = sorted(zip(starts or [0] * len(gens), range(len(gens))))
    active = []
    rnd = 0
    while waiting or active:
        while waiting and waiting[0][0] <= rnd:
            active.append(gens[waiting.pop(0)[1]])
        active = [gen for gen in active if next(gen, True) is None]
        rnd += 1


def _rwkv_kernel(h_ref, pre_g_ref, w_ref, mu_ref, w0_ref, wup_ref, a0_ref, aup_ref, gup_ref,
                 kk_ref, ka_ref, rk_ref, gng_ref, gnb_ref, y_ref, carry_ref, state_ref, q_scr,
                 y0_scr, m_scr, n_scr, yh_scr):
    C, H, N, GH = RWKV_CHUNK, RWKV_HEADS, RWKV_HEAD_DIM, RWKV_HEAD_GROUP
    W = RWKV_WIDTH
    ts = h_ref.shape[1]
    n_chunks = ts // C

    @pl.when(pl.program_id(1) == 0)
    def _():
        carry_ref[...] = jnp.zeros_like(carry_ref)
        state_ref[...] = jnp.zeros_like(state_ref)

    GL = GH * N
    row_blk = lax.broadcasted_iota(jnp.int32, (GL, GL), 0) // N
    col_blk = lax.broadcasted_iota(jnp.int32, (GL, GL), 1) // N
    same_head = (row_blk == col_blk).astype(BF16)
    ti = lax.broadcasted_iota(jnp.int32, (C, GL), 0)
    si = lax.broadcasted_iota(jnp.int32, (C, GL), 1) % N
    cat_incl = ti >= si
    cat_strict = ti > si
    cat_eye = ti == si
    ci = lax.broadcasted_iota(jnp.int32, (C, C), 0)
    cj = lax.broadcasted_iota(jnp.int32, (C, C), 1)
    tri_3 = jnp.concatenate([(ci >= cj).astype(BF16)] * 3, axis=1)

    heads_per_vreg = LANES // N
    lane_head = lax.broadcasted_iota(jnp.int32, (C, LANES), 1) // N
    keep_head = [(lane_head == e).astype(BF16) for e in range(heads_per_vreg)]
    zero_cols = jnp.zeros((C, LANES), BF16)

    def block_diag(x):
        xb = x.astype(BF16)
        row_blocks = []
        for h in range(GH):
            col, e = divmod(h, heads_per_vreg)
            cols = [zero_cols] * (GL // LANES)
            cols[col] = xb[:, col * LANES:(col + 1) * LANES] * keep_head[e]
            row_blocks.append(jnp.concatenate(cols, axis=1))
        return jnp.concatenate(row_blocks, axis=0)

    def head_sum(x):
        return jnp.dot(x.astype(BF16), same_head, preferred_element_type=F32)

    NG = H // GH
    group_lanes = [slice(gi * GL, (gi + 1) * GL) for gi in range(NG)]
    RB = min(RWKV_BLOCK_ROWS, ts)
    n_blocks = ts // RB
    blocks = [None] * n_blocks

    proj = [None] * n_blocks

    def prologue_gen(b):
        u = _rms_norm(h_ref[0, b * RB:(b + 1) * RB, :], pre_g_ref[...]).astype(BF16)
        p_b = jnp.dot(u, w_ref[...], preferred_element_type=F32)
        proj[b] = p_b
        yield
        prev_row = carry_ref[...] if b == 0 else proj[b - 1][RB - 1:RB]
        row = lax.broadcasted_iota(jnp.int32, p_b.shape, 0)
        p_prev = jnp.where(row == 0, prev_row, pltpu.roll(p_b, 1, axis=0))
        xs = p_b + (p_prev - p_b) * mu_ref[...]
        r = xs[:, 0:W]
        k = xs[:, W:2 * W]
        v = xs[:, 2 * W:3 * W]
        o = 3 * W
        wd = xs[:, o:o + DECAY_LORA]
        ad = xs[:, o + DECAY_LORA:o + DECAY_LORA + AAA_LORA]
        gd = xs[:, o + DECAY_LORA + AAA_LORA:o + DECAY_LORA + AAA_LORA + GATE_LORA]
        z = w0_ref[...] + _dot(jnp.tanh(wd), wup_ref[...])
        lw = -math.exp(-0.5) * _sigmoid(z)
        a = _sigmoid(a0_ref[...] + _dot(ad, aup_ref[...]))
        g = _dot(_sigmoid(gd), gup_ref[...])
        kk_raw = k * kk_ref[...]
        k2 = k * (1.0 + (a - 1.0) * ka_ref[...])
        rkr = r * k2 * rk_ref[...]
        yield
        sums = head_sum(jnp.concatenate([kk_raw[:, gl] * kk_raw[:, gl] for gl in group_lanes]
                                        + [rkr[:, gl] for gl in group_lanes], axis=0))
        ssq = jnp.concatenate([sums[gi * RB:(gi + 1) * RB] for gi in range(NG)], axis=1)
        bonus = jnp.concatenate([sums[(NG + gi) * RB:(NG + gi + 1) * RB] for gi in range(NG)],
                                axis=1)
        kk = kk_raw * lax.rsqrt(jnp.maximum(ssq, 1e-24))
        blocks[b] = dict(r=r, k2=k2, v=v, lw=lw, kk=kk, bvec=kk * a, g=g, bonus=bonus)

    def chunk_terms(c, gi, r_t, a_t, b_t, k_t, v_c, p_end):
        bd_b = block_diag(b_t)
        bd_k = block_diag(k_t)
        bd_v = block_diag(v_c)
        diag_p = jnp.where(cat_eye, p_end, 0.0)
        gram = _dot_nt(jnp.concatenate([a_t, r_t, diag_p], axis=0),
                       jnp.concatenate([bd_b, bd_k], axis=0))
        yield
        a_ab = jnp.where(cat_strict, gram[:C, :GL], 0.0)
        a_ak = jnp.where(cat_strict, gram[:C, GL:], 0.0)
        a_rb = jnp.where(cat_incl, gram[C:2 * C, :GL], 0.0)
        a_rk = jnp.where(cat_incl, gram[C:2 * C, GL:], 0.0)
        bt_end = gram[2 * C:, :GL]
        kt_end = gram[2 * C:, GL:]
        x = jnp.where(cat_eye, 1.0, a_ab)
        pw = _dot(a_ab, block_diag(a_ab))
        xv = _dot(jnp.concatenate([a_ak, a_rk, kt_end], axis=0), bd_v)
        akv, y0, n0 = xv[:C], xv[C:2 * C], xv[2 * C:]
        yield
        for _ in range(int(math.log2(C)) - 2):
            sq = _dot(jnp.concatenate([pw, x], axis=0), block_diag(pw))
            pw = sq[:C]
            x = x + sq[C:]
            yield
        x = x + _dot(x, block_diag(pw))
        yield
        lt = _dot(jnp.concatenate([a_rb, bt_end], axis=0), block_diag(x))
        yield
        qymn = _dot(lt, jnp.concatenate([block_diag(a_t), block_diag(akv)], axis=1))
        q_scr[c, gi] = (r_t + qymn[:C, :GL]).astype(BF16)
        y0_scr[c, gi] = qymn[:C, GL:] + y0
        m_scr[c, gi] = (diag_p + qymn[C:, :GL]).astype(BF16)
        n_scr[c, gi] = qymn[C:, GL:] + n0

    def chunk_gen(c):
        b, off = divmod(c * C, RB)
        while blocks[b] is None:
            yield
        blk = blocks[b]
        rows = slice(off, off + C)
        r, k2, v, lw, kk, bvec = (blk[name] for name in ("r", "k2", "v", "lw", "kk", "bvec"))
        lw_c = lw[rows]
        cum = jnp.dot(tri_3, jnp.concatenate(_split3(lw_c), axis=0),
                      preferred_element_type=F32)
        e_pos = jnp.exp(cum)
        e_neg = jnp.exp(-cum)
        e_prev = jnp.exp(cum - lw_c)
        p_end = jnp.exp(cum[C - 1:C, :])
        r_t = r[rows] * e_pos
        a_t = -kk[rows] * e_prev
        b_t = bvec[rows] * e_neg
        k_t = k2[rows] * e_neg
        v_c = v[rows]
        yield
        subs = [chunk_terms(c, gi, r_t[:, gl], a_t[:, gl], b_t[:, gl], k_t[:, gl], v_c[:, gl],
                            p_end[:, gl]) for gi, gl in enumerate(group_lanes)]
        while subs:
            subs = [sub for sub in subs if next(sub, True) is None]
            yield
        chunk_done[c] = True

    chunk_done = [False] * n_chunks
    scanned = [False] * n_chunks
    states = [state_ref[gi] for gi in range(NG)]

    def scan_gen():
        for c in range(n_chunks):
            while not chunk_done[c]:
                yield
            rows = slice(c * C, (c + 1) * C)
            for gi, gl in enumerate(group_lanes):
                res = _dot(jnp.concatenate([q_scr[c, gi], m_scr[c, gi]], axis=0),
                           block_diag(states[gi]))
                yh_scr[rows, gl] = res[:C] + y0_scr[c, gi]
                states[gi] = res[C:] + n_scr[c, gi]
            scanned[c] = True
            yield

    def epilogue_gen(b):
        while not scanned[(b + 1) * RB // C - 1]:
            yield
        rows = slice(b * RB, (b + 1) * RB)
        blk = blocks[b]
        ys = [yh_scr[rows, gl] for gl in group_lanes]
        mean = head_sum(jnp.concatenate(ys, axis=0)) * (1.0 / N)
        ycs = [ys[gi] - mean[gi * RB:(gi + 1) * RB] for gi in range(NG)]
        yield
        var = head_sum(jnp.concatenate([yc * yc for yc in ycs], axis=0)) * (1.0 / N)
        yn = jnp.concatenate([ycs[gi] * lax.rsqrt(var[gi * RB:(gi + 1) * RB] + RWKV_GN_EPS)
                              for gi in range(NG)], axis=1)
        out = yn * gng_ref[...] + gnb_ref[...] + blk["bonus"] * blk["v"]
        y_ref[0, rows, :] = (out * blk["g"]).astype(y_ref.dtype)

    blocks_per_round = RB // C * max(RWKV_CHUNK_STAGGER, 1)
    _run_interleaved(
        [scan_gen()] + [prologue_gen(b) for b in range(n_blocks)]
        + [chunk_gen(c) for c in range(n_chunks)] + [epilogue_gen(b) for b in range(n_blocks)],
        starts=[0] + [b * blocks_per_round for b in range(n_blocks)]
        + [c * RWKV_CHUNK_STAGGER for c in range(n_chunks)] + [0] * n_blocks)
    for gi in range(NG):
        state_ref[gi] = states[gi]
    carry_ref[...] = proj[n_blocks - 1][RB - 1:RB]


def _rwkv_mix(h, pre_g, w_proj, mu, w0, w_up, a0, a_up, g_up, k_k, k_a, r_k, gn_g, gn_b, *,
              ts):
    B, S, D = h.shape
    C, H, N, W = RWKV_CHUNK, RWKV_HEADS, RWKV_HEAD_DIM, RWKV_WIDTH
    assert S % ts == 0 and ts % min(RWKV_BLOCK_ROWS, ts) == 0 and RWKV_BLOCK_ROWS % C == 0
    n_chunks = ts // C
    NG, GL = H // RWKV_HEAD_GROUP, RWKV_HEAD_GROUP * N
    row = lambda t: t.reshape(1, -1).astype(F32)
    full = lambda shape: pl.BlockSpec(shape, lambda b, i: (0,) * len(shape),
                                      pipeline_mode=pl.Buffered(1))
    return pl.pallas_call(
        _rwkv_kernel,
        out_shape=jax.ShapeDtypeStruct((B, S, W), BF16),
        grid=(B, S // ts),
        in_specs=[
            pl.BlockSpec((1, ts, D), lambda b, i: (b, i, 0)),
            full((1, D)), full((D, RWKV_COLS)),
            full((1, RWKV_COLS)), full((1, W)), full((DECAY_LORA, W)), full((1, W)),
            full((AAA_LORA, W)), full((GATE_LORA, W)), full((1, W)), full((1, W)),
            full((1, W)), full((1, W)), full((1, W)),
        ],
        out_specs=pl.BlockSpec((1, ts, W), lambda b, i: (b, i, 0)),
        scratch_shapes=[
            pltpu.VMEM((1, RWKV_COLS), F32),
            pltpu.VMEM((NG, N, GL), F32),
            pltpu.VMEM((n_chunks, NG, C, GL), BF16),
            pltpu.VMEM((n_chunks, NG, C, GL), F32),
            pltpu.VMEM((n_chunks, NG, N, GL), BF16),
            pltpu.VMEM((n_chunks, NG, N, GL), F32),
            pltpu.VMEM((ts, W), F32),
        ],
        compiler_params=pltpu.CompilerParams(
            dimension_semantics=("arbitrary", "arbitrary"),
            vmem_limit_bytes=VMEM_LIMIT_BYTES),
        name="rwkv7_mix",
    )(h, row(pre_g), w_proj, row(mu), row(w0), w_up.astype(BF16), row(a0), a_up.astype(BF16),
      g_up.astype(BF16), row(k_k), row(k_a), row(r_k), row(gn_g), row(gn_b))


D_MODEL = 1024
D_FF = 2816
FFN_CHUNK = 256
FFN_SUBTILES = 2
FFN_STAGGER = 9


def _const_spec(shape):
    return pl.BlockSpec(shape, lambda *_: (0,) * len(shape), pipeline_mode=pl.Buffered(1))


def _swiglu_half_step(h_ref, pre_g, w_in_ref, w_out_ref, post_g, o_ref, rows):
    h = h_ref[rows, :]
    xn = _rms_norm(h, pre_g).astype(BF16)
    acc = jnp.zeros(h.shape, F32)
    yield
    for c in range(D_FF // FFN_CHUNK):
        lo = c * FFN_CHUNK
        gate = jnp.dot(xn, w_in_ref[:, lo:lo + FFN_CHUNK], preferred_element_type=F32)
        up = jnp.dot(xn, w_in_ref[:, D_FF + lo:D_FF + lo + FFN_CHUNK],
                     preferred_element_type=F32)
        act = (gate * _sigmoid(gate) * up).astype(BF16)
        acc = acc + jnp.dot(act, w_out_ref[lo:lo + FFN_CHUNK, :], preferred_element_type=F32)
        yield
    o_ref[rows, :] = h + HALF_STEP * _rms_norm(acc, post_g)


def _swiglu_tile(h_ref, pre_g, w_in_ref, w_out_ref, post_g, o_ref):
    sub = h_ref.shape[0] // FFN_SUBTILES
    _run_interleaved(
        [_swiglu_half_step(h_ref, pre_g, w_in_ref, w_out_ref, post_g, o_ref,
                           slice(s * sub, (s + 1) * sub)) for s in range(FFN_SUBTILES)],
        starts=[s * FFN_STAGGER for s in range(FFN_SUBTILES)])


def _ffn_kernel(col_splits, h_ref, pre_g_ref, w_in_ref, w_out_ref, post_g_ref, *refs):
    n_jobs = len(col_splits)
    cast_in, o_ref, cast_out = refs[:n_jobs], refs[n_jobs], refs[n_jobs + 1:]
    k = 0
    for src_ref, splits in zip(cast_in, col_splits):
        for lo, hi in splits:
            cast_out[k][...] = src_ref[:, lo:hi].astype(BF16)
            k += 1
    _swiglu_tile(h_ref, pre_g_ref[...], w_in_ref, w_out_ref, post_g_ref[...], o_ref)


def _slab_rows(n_rows, n_steps):
    rows = BF16_SUBLANES
    while n_rows % rows or rows * n_steps < n_rows:
        rows += BF16_SUBLANES
    return rows


def _ffn(h, pre_g, w_in, w_out, post_g, *, tm, cast_jobs=()):
    T, D = h.shape
    assert T % tm == 0 and tm % (FFN_SUBTILES * BF16_SUBLANES) == 0
    n_steps = T // tm
    tile = pl.BlockSpec((tm, D), lambda i: (i, 0))
    cast_in_specs, cast_out_specs, cast_out_shapes = [], [], []
    for w, splits in cast_jobs:
        rows = _slab_rows(w.shape[0], n_steps)
        last = w.shape[0] // rows - 1
        index = lambda i, last=last: (jnp.minimum(i, last), 0)
        cast_in_specs.append(pl.BlockSpec((rows, w.shape[1]), index))
        for lo, hi in splits:
            cast_out_specs.append(pl.BlockSpec((rows, hi - lo), index))
            cast_out_shapes.append(jax.ShapeDtypeStruct((w.shape[0], hi - lo), BF16))
    outs = pl.pallas_call(
        functools.partial(_ffn_kernel, tuple(tuple(s) for _, s in cast_jobs)),
        out_shape=[jax.ShapeDtypeStruct((T, D), F32)] + cast_out_shapes,
        grid=(n_steps,),
        in_specs=[tile, _const_spec((1, D)), _const_spec((D, 2 * D_FF)),
                  _const_spec((D_FF, D)), _const_spec((1, D))] + cast_in_specs,
        out_specs=[tile] + cast_out_specs,
        compiler_params=pltpu.CompilerParams(
            dimension_semantics=("arbitrary",), vmem_limit_bytes=VMEM_LIMIT_BYTES),
        name="swiglu_half_step",
    )(h, pre_g.reshape(1, D), w_in.astype(BF16), w_out.astype(BF16), post_g.reshape(1, D),
      *[w for w, _ in cast_jobs])
    return outs


ATTN_COLS = SWA_COLS + MEM_WIDTH
BRANCH_COLS = RWKV_COLS + ATTN_COLS

def _rope_tables(pos_row):
    half = ROPE_DIM // 2
    f = lax.broadcasted_iota(jnp.int32, (half, 1), 0).astype(F32)
    inv_freq = jnp.exp(f * (-2.0 / ROPE_DIM * math.log(ROPE_THETA)))
    ang = inv_freq * pos_row
    cs = jnp.concatenate([jnp.cos(ang), jnp.sin(ang)], axis=0)
    src = lax.broadcasted_iota(jnp.int32, (2 * half, 2 * LANES), 0)
    col = lax.broadcasted_iota(jnp.int32, (2 * half, 2 * LANES), 1)
    d = col % SWA_HEAD_DIM
    is_sin = col >= LANES
    want = jnp.where(is_sin, half, 0) + d % half
    sign = jnp.where(is_sin & (d < half), -1.0, 1.0)
    sel = jnp.where((src == want) & (d < ROPE_DIM), sign, 0.0).astype(BF16)
    hi = cs.astype(BF16)
    lo = (cs - hi.astype(F32)).astype(BF16)
    tab = _dot_tn(hi, sel) + _dot_tn(lo, sel)
    lane_d = lax.broadcasted_iota(jnp.int32, (1, LANES), 1) % SWA_HEAD_DIM
    c = tab[:, :LANES] + jnp.where(lane_d < ROPE_DIM, 0.0, 1.0)
    return c, tab[:, LANES:]


def _apply_rope(x, c, s):
    half = ROPE_DIM // 2
    low = (lax.broadcasted_iota(jnp.int32, (1, LANES), 1) % SWA_HEAD_DIM) < half
    cols = []
    for j in range(x.shape[1] // LANES):
        xj = x[:, j * LANES:(j + 1) * LANES]
        partner = jnp.where(low, pltpu.roll(xj, LANES - half, axis=1),
                            pltpu.roll(xj, half, axis=1))
        cols.append(xj * c + partner * s)
    return cols[0] if len(cols) == 1 else jnp.concatenate(cols, axis=1)


def _kv_lane_variants(x):
    hd = SWA_HEAD_DIM
    low = lax.broadcasted_iota(jnp.int32, (1, LANES), 1) < hd
    xs = pltpu.roll(x, hd, axis=1)
    return [[jnp.where(low, x, 0.0).astype(BF16), jnp.where(low, 0.0, xs).astype(BF16)],
            [jnp.where(low, xs, 0.0).astype(BF16), jnp.where(low, 0.0, x).astype(BF16)]]


def _swa_attention(q, k, v, kc_ref, vc_ref, sinks_ref, first_tile):
    tq = q.shape[0]
    blk = WINDOW
    nb = tq // blk
    kv = _kv_lane_variants(k)
    vv = _kv_lane_variants(v)
    qi = lax.broadcasted_iota(jnp.int32, (blk, blk), 0)
    kj = lax.broadcasted_iota(jnp.int32, (blk, blk), 1)
    from_prev = kj > qi
    prev_bias = jnp.where(first_tile, NEG_INF, 0.0)

    def window(var, ref, g, e, j):
        prev = ref[2 * g + e] if j == 0 else var[g][e][(j - 1) * blk:j * blk]
        return [prev, var[g][e][j * blk:(j + 1) * blk]]

    pair_outs = [[None] * (2 * SWA_KV_HEADS) for _ in range(nb)]

    def block_group(j, g):
        rs = slice(j * blk, (j + 1) * blk)
        k_rhs = jnp.concatenate(window(kv, kc_ref, g, 0, j) + window(kv, kc_ref, g, 1, j),
                                axis=0)
        v_rhs = jnp.concatenate(window(vv, vc_ref, g, 0, j) + window(vv, vc_ref, g, 1, j),
                                axis=0)
        q_rows = jnp.concatenate([q[rs, (2 * g + r) * LANES:(2 * g + r + 1) * LANES]
                                  for r in range(2)], axis=0)
        s_all = _dot_nt(q_rows, k_rhs)
        yield
        probs = []
        for r in range(2):
            row_p = []
            for e in range(2):
                sink = sinks_ref[SWA_GROUP * g + 2 * r + e]
                s_prev = s_all[r * blk:(r + 1) * blk, 2 * e * blk:(2 * e + 1) * blk]
                s_cur = s_all[r * blk:(r + 1) * blk, (2 * e + 1) * blk:(2 * e + 2) * blk]
                if j == 0:
                    s_prev = s_prev + prev_bias
                s = jnp.where(from_prev, s_prev, s_cur)
                m = jnp.maximum(jnp.max(s, axis=-1, keepdims=True), sink)
                ex = jnp.exp(s - m)
                denom = jnp.sum(ex, axis=-1, keepdims=True) + jnp.exp(sink - m)
                pn = ex * (1.0 / denom)
                row_p += [jnp.where(from_prev, pn, 0.0).astype(BF16),
                          jnp.where(from_prev, 0.0, pn).astype(BF16)]
            probs.append(jnp.concatenate(row_p, axis=1))
            yield
        o = _dot(jnp.concatenate(probs, axis=0), v_rhs)
        pair_outs[j][2 * g] = o[:blk]
        pair_outs[j][2 * g + 1] = o[blk:]

    _run_interleaved([block_group(j, g) for j in range(nb) for g in range(SWA_KV_HEADS)])
    outs = [jnp.concatenate(po, axis=1) for po in pair_outs]
    for g in range(SWA_KV_HEADS):
        for e in range(2):
            kc_ref[2 * g + e] = kv[g][e][tq - blk:tq]
            vc_ref[2 * g + e] = vv[g][e][tq - blk:tq]
    return jnp.concatenate(outs, axis=0)


def _attn_kernel(sinks_ref, h_ref, pos_ref, g_ref, w_ref, km_ref, vm_ref, yb_ref, yc_ref,
                 kc_ref, vc_ref):
    first_tile = pl.program_id(1) == 0

    @pl.when(first_tile)
    def _():
        kc_ref[...] = jnp.zeros_like(kc_ref)
        vc_ref[...] = jnp.zeros_like(vc_ref)

    u = _rms_norm(h_ref[0], g_ref[...]).astype(BF16)
    p = jnp.dot(u, w_ref[...], preferred_element_type=F32)
    c, s = _rope_tables(pos_ref[0].astype(F32))
    q = (_apply_rope(p[:, 0:SWA_Q_WIDTH], c, s) * (SWA_HEAD_DIM ** -0.5)).astype(BF16)
    k = _apply_rope(p[:, SWA_Q_WIDTH:SWA_Q_WIDTH + SWA_KV_WIDTH], c, s)
    v = p[:, SWA_Q_WIDTH + SWA_KV_WIDTH:SWA_COLS]
    yb_ref[0] = _swa_attention(q, k, v, kc_ref, vc_ref, sinks_ref,
                               first_tile).astype(yb_ref.dtype)
    yc_ref[0] = _mem_attention(p[:, SWA_COLS:ATTN_COLS], km_ref[0],
                               vm_ref[0]).astype(yc_ref.dtype)


def _attn_branches(h, positions, sinks, g, w_attn, km, vm, *, tq):
    B, S, D = h.shape
    M = km.shape[1]
    assert S % tq == 0 and tq % WINDOW == 0
    n_carry = 2 * SWA_KV_HEADS
    const = lambda shape: pl.BlockSpec(shape, lambda b, i, *_: (0,) * len(shape),
                                       pipeline_mode=pl.Buffered(1))
    return pl.pallas_call(
        _attn_kernel,
        out_shape=(jax.ShapeDtypeStruct((B, S, SWA_Q_WIDTH), BF16),
                   jax.ShapeDtypeStruct((B, S, MEM_WIDTH), BF16)),
        grid_spec=pltpu.PrefetchScalarGridSpec(
            num_scalar_prefetch=1,
            grid=(B, S // tq),
            in_specs=[pl.BlockSpec((1, tq, D), lambda b, i, *_: (b, i, 0)),
                      pl.BlockSpec((1, 1, tq), lambda b, i, *_: (b, 0, i)),
                      const((1, D)), const((D, ATTN_COLS)),
                      pl.BlockSpec((1, M, MEM_WIDTH), lambda b, i, *_: (b, 0, 0)),
                      pl.BlockSpec((1, M, MEM_WIDTH), lambda b, i, *_: (b, 0, 0))],
            out_specs=(pl.BlockSpec((1, tq, SWA_Q_WIDTH), lambda b, i, *_: (b, i, 0)),
                       pl.BlockSpec((1, tq, MEM_WIDTH), lambda b, i, *_: (b, i, 0))),
            scratch_shapes=[pltpu.VMEM((n_carry, WINDOW, LANES), BF16),
                            pltpu.VMEM((n_carry, WINDOW, LANES), BF16)]),
        compiler_params=pltpu.CompilerParams(
            dimension_semantics=("arbitrary", "arbitrary"),
            vmem_limit_bytes=VMEM_LIMIT_BYTES),
        name="attn_branches",
    )(sinks.astype(F32), h, positions.reshape(B, 1, S), g.reshape(1, D), w_attn, km, vm)


def _mem_kv_kernel(mem_ref, g_ref, w_ref, km_ref, vm_ref):
    mn = _rms_norm(mem_ref[0], g_ref[...]).astype(BF16)
    km_ref[0] = jnp.dot(mn, w_ref[:, 0:MEM_WIDTH],
                        preferred_element_type=F32).astype(km_ref.dtype)
    vm_ref[0] = jnp.dot(mn, w_ref[:, MEM_WIDTH:2 * MEM_WIDTH],
                        preferred_element_type=F32).astype(vm_ref.dtype)


def _mem_kv(mem, g, w_kv):
    B, M, D = mem.shape
    out = jax.ShapeDtypeStruct((B, M, MEM_WIDTH), BF16)
    return pl.pallas_call(
        _mem_kv_kernel,
        out_shape=(out, out),
        grid=(B,),
        in_specs=[pl.BlockSpec((1, M, D), lambda b: (b, 0, 0)), _const_spec((1, D)),
                  _const_spec((D, 2 * MEM_WIDTH))],
        out_specs=(pl.BlockSpec((1, M, MEM_WIDTH), lambda b: (b, 0, 0)),
                   pl.BlockSpec((1, M, MEM_WIDTH), lambda b: (b, 0, 0))),
        compiler_params=pltpu.CompilerParams(
            dimension_semantics=("arbitrary",), vmem_limit_bytes=VMEM_LIMIT_BYTES),
        name="mem_kv_proj",
    )(mem, g.reshape(1, D), w_kv.astype(BF16))


def _mem_attention(q, km, vm):
    hd = MEM_HEAD_DIM
    q = (q * (hd ** -0.5)).astype(BF16)
    outs = []
    for h in range(MEM_HEADS):
        hs = slice(h * hd, (h + 1) * hd)
        s = _dot_nt(q[:, hs], km[:, hs])
        e = jnp.exp(s - jnp.max(s, axis=-1, keepdims=True))
        outs.append(_dot(e, vm[:, hs]) / jnp.sum(e, axis=-1, keepdims=True))
    return jnp.concatenate(outs, axis=1)


MERGE_SUBTILES = 2
MERGE_STAGGER = 3


def _merge_kernel(h_ref, ya_ref, yb_ref, yc_ref, pre_g_ref, wg_ref, gb_ref, wb_ref, wo_ref,
                  post_g_ref, o_ref):
    D = D_MODEL

    def sub_tile(rows):
        h = h_ref[rows, :]
        u = _rms_norm(h, pre_g_ref[...]).astype(BF16)
        merged = jnp.zeros(h.shape, F32)
        yield
        for i, y_ref in enumerate((ya_ref, yb_ref, yc_ref)):
            logits = jnp.dot(u, wg_ref[:, i * D:(i + 1) * D], preferred_element_type=F32)
            gate = _sigmoid(logits + gb_ref[:, i * D:(i + 1) * D])
            merged = merged + gate * jnp.dot(y_ref[rows, :], wb_ref[i],
                                             preferred_element_type=F32)
            yield
        out = jnp.dot(merged.astype(BF16), wo_ref[...], preferred_element_type=F32)
        yield
        o_ref[rows, :] = h + _rms_norm(out, post_g_ref[...])

    sub = h_ref.shape[0] // MERGE_SUBTILES
    _run_interleaved([sub_tile(slice(s * sub, (s + 1) * sub)) for s in range(MERGE_SUBTILES)],
                     starts=[s * MERGE_STAGGER for s in range(MERGE_SUBTILES)])


def _merge(h, y_a, y_b, y_c, pre_g, w_gate, gate_b, w_branch, w_out, post_g, *, tm):
    T, D = h.shape
    assert T % tm == 0 and tm % (MERGE_SUBTILES * BF16_SUBLANES) == 0
    tile = pl.BlockSpec((tm, D), lambda i: (i, 0))
    ytile = pl.BlockSpec((tm, BRANCH_WIDTH), lambda i: (i, 0))
    return pl.pallas_call(
        _merge_kernel,
        out_shape=jax.ShapeDtypeStruct((T, D), F32),
        grid=(T // tm,),
        in_specs=[tile, ytile, ytile, ytile, _const_spec((1, D)),
                  _const_spec((D, N_BRANCH * D)), _const_spec((1, N_BRANCH * D)),
                  _const_spec((N_BRANCH, BRANCH_WIDTH, D)), _const_spec((D, D)),
                  _const_spec((1, D))],
        out_specs=tile,
        compiler_params=pltpu.CompilerParams(
            dimension_semantics=("arbitrary",), vmem_limit_bytes=VMEM_LIMIT_BYTES),
        name="gated_merge",
    )(h, y_a, y_b, y_c, pre_g.reshape(1, D), w_gate, gate_b.reshape(1, N_BRANCH * D),
      w_branch.astype(BF16), w_out.astype(BF16), post_g.reshape(1, D))


def kernel(x, mem, positions, ffn1_pre_g, ffn1_w_in, ffn1_w_out, ffn1_post_g, mix_pre_g, w_in, gate_b, rwkv_mu, rwkv_w0, rwkv_w_up, rwkv_a0, rwkv_a_up, rwkv_g_up, rwkv_k_k, rwkv_k_a, rwkv_r_k, rwkv_gn_g, rwkv_gn_b, swa_sinks, mem_norm_g, mem_w_kv, w_branch, w_out, mix_post_g, ffn2_pre_g, ffn2_w_in, ffn2_w_out, ffn2_post_g):
    B, S, D = x.shape
    T = B * S
    tm = min(TOKEN_TILE, T)
    ts = min(TOKEN_TILE, S)
    h = x.reshape(T, D)
    for l in range(ffn1_pre_g.shape[0]):
        whole = lambda w: ((0, w.shape[1]),)
        cast_jobs = (
            (w_in[l], ((0, RWKV_COLS), (RWKV_COLS, BRANCH_COLS), (BRANCH_COLS, w_in.shape[2]))),
            (mem_w_kv[l], whole(mem_w_kv[l])),
            (w_branch[l].reshape(N_BRANCH * BRANCH_WIDTH, D), ((0, D),)),
            (w_out[l], whole(w_out[l])),
            (ffn2_w_in[l], whole(ffn2_w_in[l])),
            (ffn2_w_out[l], whole(ffn2_w_out[l])),
        )
        (h, w_rwkv, w_attn, w_gate, w_kv_bf, w_branch_bf, w_out_bf, ffn2_w_in_bf,
         ffn2_w_out_bf) = _ffn(h, ffn1_pre_g[l], ffn1_w_in[l], ffn1_w_out[l], ffn1_post_g[l],
                               tm=tm, cast_jobs=cast_jobs)

        h3 = h.reshape(B, S, D)
        y_a = _rwkv_mix(h3, mix_pre_g[l], w_rwkv, rwkv_mu[l], rwkv_w0[l],
                        rwkv_w_up[l], rwkv_a0[l], rwkv_a_up[l], rwkv_g_up[l], rwkv_k_k[l],
                        rwkv_k_a[l], rwkv_r_k[l].reshape(-1), rwkv_gn_g[l], rwkv_gn_b[l], ts=ts)
        km, vm = _mem_kv(mem, mem_norm_g[l], w_kv_bf)
        y_b, y_c = _attn_branches(h3, positions, swa_sinks[l], mix_pre_g[l], w_attn, km, vm,
                                  tq=ts)
        h = _merge(h, y_a.reshape(T, -1), y_b.reshape(T, -1), y_c.reshape(T, -1), mix_pre_g[l],
                   w_gate, gate_b[l], w_branch_bf.reshape(N_BRANCH, BRANCH_WIDTH, D), w_out_bf,
                   mix_post_g[l], tm=tm)

        h, = _ffn(h, ffn2_pre_g[l], ffn2_w_in_bf, ffn2_w_out_bf, ffn2_post_g[l], tm=tm)
    return h.reshape(B, S, D)
```

```python
import functools
import math

import jax
import jax.numpy as jnp
from jax import lax
from jax.experimental import pallas as pl
from jax.experimental.pallas import tpu as pltpu

F32 = jnp.float32
BF16 = jnp.bfloat16

LANES = 128
BF16_SUBLANES = 16
MXU_WIDTH = 256
VMEM_LIMIT_BYTES = 56 * 1024 * 1024
TOKEN_TILE = 1024

NORM_EPS = 1e-6
HALF_STEP = 0.5

RWKV_HEADS = 8
RWKV_HEAD_DIM = 64
RWKV_WIDTH = RWKV_HEADS * RWKV_HEAD_DIM
DECAY_LORA = 64
AAA_LORA = 64
GATE_LORA = 128
RWKV_COLS = 3 * RWKV_WIDTH + DECAY_LORA + AAA_LORA + GATE_LORA
RWKV_GN_EPS = 64e-5
RWKV_CHUNK = 64
RWKV_HEAD_GROUP = MXU_WIDTH // RWKV_HEAD_DIM
assert RWKV_CHUNK == RWKV_HEAD_DIM and RWKV_HEADS % RWKV_HEAD_GROUP == 0
RWKV_CHUNK_STAGGER = 2
RWKV_BLOCK_ROWS = 256

SWA_Q_HEADS = 8
SWA_KV_HEADS = 2
SWA_GROUP = SWA_Q_HEADS // SWA_KV_HEADS
SWA_HEAD_DIM = 64
SWA_Q_WIDTH = SWA_Q_HEADS * SWA_HEAD_DIM
SWA_KV_WIDTH = SWA_KV_HEADS * SWA_HEAD_DIM
SWA_COLS = SWA_Q_WIDTH + 2 * SWA_KV_WIDTH
WINDOW = 128
ROPE_THETA = 500000.0
ROPE_DIM = SWA_HEAD_DIM // 4
NEG_INF = -1e30

MEM_HEADS = 4
MEM_HEAD_DIM = 128
MEM_WIDTH = MEM_HEADS * MEM_HEAD_DIM

N_BRANCH = 3
BRANCH_WIDTH = 512


def _dot(a, b):
    return jnp.dot(a.astype(BF16), b.astype(BF16), preferred_element_type=F32)


def _dot_nt(a, b):
    return lax.dot_general(a.astype(BF16), b.astype(BF16), (((1,), (1,)), ((), ())),
                           preferred_element_type=F32)


def _dot_tn(a, b):
    return lax.dot_general(a.astype(BF16), b.astype(BF16), (((0,), (0,)), ((), ())),
                           preferred_element_type=F32)


def _split3(x):
    hi = x.astype(BF16)
    r1 = x - hi.astype(F32)
    mid = r1.astype(BF16)
    lo = (r1 - mid.astype(F32)).astype(BF16)
    return hi, mid, lo


def _rms_norm(x, g):
    return x * lax.rsqrt(jnp.mean(x * x, axis=-1, keepdims=True) + NORM_EPS) * g


def _sigmoid(x):
    return 1.0 / (1.0 + jnp.exp(-x))


def _run_interleaved(gens, starts=None):
    waiting = sorted(zip(starts or [0] * len(gens), range(len(gens))))
    active = []
    rnd = 0
    while waiting or active:
        while waiting and waiting[0][0] <= rnd:
            active.append(gens[waiting.pop(0)[1]])
        active = [gen for gen in active if next(gen, True) is None]
        rnd += 1


def _rwkv_kernel(h_ref, pre_g_ref, w_ref, mu_ref, w0_ref, wup_ref, a0_ref, aup_ref, gup_ref,
                 kk_ref, ka_ref, rk_ref, gng_ref, gnb_ref, y_ref, carry_ref, state_ref, q_scr,
                 y0_scr, m_scr, n_scr, yh_scr):
    C, H, N, GH = RWKV_CHUNK, RWKV_HEADS, RWKV_HEAD_DIM, RWKV_HEAD_GROUP
    W = RWKV_WIDTH
    ts = h_ref.shape[1]
    n_chunks = ts // C

    @pl.when(pl.program_id(1) == 0)
    def _():
        carry_ref[...] = jnp.zeros_like(carry_ref)
        state_ref[...] = jnp.zeros_like(state_ref)

    GL = GH * N
    row_blk = lax.broadcasted_iota(jnp.int32, (GL, GL), 0) // N
    col_blk = lax.broadcasted_iota(jnp.int32, (GL, GL), 1) // N
    same_head = (row_blk == col_blk).astype(BF16)
    ti = lax.broadcasted_iota(jnp.int32, (C, GL), 0)
    si = lax.broadcasted_iota(jnp.int32, (C, GL), 1) % N
    cat_incl = ti >= si
    cat_strict = ti > si
    cat_eye = ti == si
    ci = lax.broadcasted_iota(jnp.int32, (C, C), 0)
    cj = lax.broadcasted_iota(jnp.int32, (C, C), 1)
    tri_3 = jnp.concatenate([(ci >= cj).astype(BF16)] * 3, axis=1)

    heads_per_vreg = LANES // N
    lane_head = lax.broadcasted_iota(jnp.int32, (C, LANES), 1) // N
    keep_head = [(lane_head == e).astype(BF16) for e in range(heads_per_vreg)]
    zero_cols = jnp.zeros((C, LANES), BF16)

    def block_diag(x):
        xb = x.astype(BF16)
        row_blocks = []
        for h in range(GH):
            col, e = divmod(h, heads_per_vreg)
            cols = [zero_cols] * (GL // LANES)
            cols[col] = xb[:, col * LANES:(col + 1) * LANES] * keep_head[e]
            row_blocks.append(jnp.concatenate(cols, axis=1))
        return jnp.concatenate(row_blocks, axis=0)

    def head_sum(x):
        return jnp.dot(x.astype(BF16), same_head, preferred_element_type=F32)

    NG = H // GH
    group_lanes = [slice(gi * GL, (gi + 1) * GL) for gi in range(NG)]
    RB = min(RWKV_BLOCK_ROWS, ts)
    n_blocks = ts // RB
    blocks = [None] * n_blocks

    proj = [None] * n_blocks

    def prologue_gen(b):
        u = _rms_norm(h_ref[0, b * RB:(b + 1) * RB, :], pre_g_ref[...]).astype(BF16)
        p_b = jnp.dot(u, w_ref[...], preferred_element_type=F32)
        proj[b] = p_b
        yield
        prev_row = carry_ref[...] if b == 0 else proj[b - 1][RB - 1:RB]
        row = lax.broadcasted_iota(jnp.int32, p_b.shape, 0)
        p_prev = jnp.where(row == 0, prev_row, pltpu.roll(p_b, 1, axis=0))
        xs = p_b + (p_prev - p_b) * mu_ref[...]
        r = xs[:, 0:W]
        k = xs[:, W:2 * W]
        v = xs[:, 2 * W:3 * W]
        o = 3 * W
        wd = xs[:, o:o + DECAY_LORA]
        ad = xs[:, o + DECAY_LORA:o + DECAY_LORA + AAA_LORA]
        gd = xs[:, o + DECAY_LORA + AAA_LORA:o + DECAY_LORA + AAA_LORA + GATE_LORA]
        z = w0_ref[...] + _dot(jnp.tanh(wd), wup_ref[...])
        lw = -math.exp(-0.5) * _sigmoid(z)
        a = _sigmoid(a0_ref[...] + _dot(ad, aup_ref[...]))
        g = _dot(_sigmoid(gd), gup_ref[...])
        kk_raw = k * kk_ref[...]
        k2 = k * (1.0 + (a - 1.0) * ka_ref[...])
        rkr = r * k2 * rk_ref[...]
        yield
        sums = head_sum(jnp.concatenate([kk_raw[:, gl] * kk_raw[:, gl] for gl in group_lanes]
                                        + [rkr[:, gl] for gl in group_lanes], axis=0))
        ssq = jnp.concatenate([sums[gi * RB:(gi + 1) * RB] for gi in range(NG)], axis=1)
        bonus = jnp.concatenate([sums[(NG + gi) * RB:(NG + gi + 1) * RB] for gi in range(NG)],
                                axis=1)
        kk = kk_raw * lax.rsqrt(jnp.maximum(ssq, 1e-24))
        blocks[b] = dict(r=r, k2=k2, v=v, lw=lw, kk=kk, bvec=kk * a, g=g, bonus=bonus)

    def chunk_terms(c, gi, r_t, a_t, b_t, k_t, v_c, p_end):
        bd_b = block_diag(b_t)
        bd_k = block_diag(k_t)
        bd_v = block_diag(v_c)
        diag_p = jnp.where(cat_eye, p_end, 0.0)
        gram = _dot_nt(jnp.concatenate([a_t, r_t, diag_p], axis=0),
                       jnp.concatenate([bd_b, bd_k], axis=0))
        yield
        a_ab = jnp.where(cat_strict, gram[:C, :GL], 0.0)
        a_ak = jnp.where(cat_strict, gram[:C, GL:], 0.0)
        a_rb = jnp.where(cat_incl, gram[C:2 * C, :GL], 0.0)
        a_rk = jnp.where(cat_incl, gram[C:2 * C, GL:], 0.0)
        bt_end = gram[2 * C:, :GL]
        kt_end = gram[2 * C:, GL:]
        x = jnp.where(cat_eye, 1.0, a_ab)
        pw = _dot(a_ab, block_diag(a_ab))
        xv = _dot(jnp.concatenate([a_ak, a_rk, kt_end], axis=0), bd_v)
        akv, y0, n0 = xv[:C], xv[C:2 * C], xv[2 * C:]
        yield
        for _ in range(int(math.log2(C)) - 2):
            sq = _dot(jnp.concatenate([pw, x], axis=0), block_diag(pw))
            pw = sq[:C]
            x = x + sq[C:]
            yield
        x = x + _dot(x, block_diag(pw))
        yield
        lt = _dot(jnp.concatenate([a_rb, bt_end], axis=0), block_diag(x))
        yield
        qymn = _dot(lt, jnp.concatenate([block_diag(a_t), block_diag(akv)], axis=1))
        q_scr[c, gi] = (r_t + qymn[:C, :GL]).astype(BF16)
        y0_scr[c, gi] = qymn[:C, GL:] + y0
        m_scr[c, gi] = (diag_p + qymn[C:, :GL]).astype(BF16)
        n_scr[c, gi] = qymn[C:, GL:] + n0

    def chunk_gen(c):
        b, off = divmod(c * C, RB)
        while blocks[b] is None:
            yield
        blk = blocks[b]
        rows = slice(off, off + C)
        r, k2, v, lw, kk, bvec = (blk[name] for name in ("r", "k2", "v", "lw", "kk", "bvec"))
        lw_c = lw[rows]
        cum = jnp.dot(tri_3, jnp.concatenate(_split3(lw_c), axis=0),
                      preferred_element_type=F32)
        e_pos = jnp.exp(cum)
        e_neg = jnp.exp(-cum)
        e_prev = jnp.exp(cum - lw_c)
        p_end = jnp.exp(cum[C - 1:C, :])
        r_t = r[rows] * e_pos
        a_t = -kk[rows] * e_prev
        b_t = bvec[rows] * e_neg
        k_t = k2[rows] * e_neg
        v_c = v[rows]
        yield
        subs = [chunk_terms(c, gi, r_t[:, gl], a_t[:, gl], b_t[:, gl], k_t[:, gl], v_c[:, gl],
                            p_end[:, gl]) for gi, gl in enumerate(group_lanes)]
        while subs:
            subs = [sub for sub in subs if next(sub, True) is None]
            yield
        chunk_done[c] = True

    chunk_done = [False] * n_chunks
    scanned = [False] * n_chunks
    states = [state_ref[gi] for gi in range(NG)]

    def scan_gen():
        for c in range(n_chunks):
            while not chunk_done[c]:
                yield
            rows = slice(c * C, (c + 1) * C)
            for gi, gl in enumerate(group_lanes):
                res = _dot(jnp.concatenate([q_scr[c, gi], m_scr[c, gi]], axis=0),
                           block_diag(states[gi]))
                yh_scr[rows, gl] = res[:C] + y0_scr[c, gi]
                states[gi] = res[C:] + n_scr[c, gi]
            scanned[c] = True
            yield

    def epilogue_gen(b):
        while not scanned[(b + 1) * RB // C - 1]:
            yield
        rows = slice(b * RB, (b + 1) * RB)
        blk = blocks[b]
        ys = [yh_scr[rows, gl] for gl in group_lanes]
        mean = head_sum(jnp.concatenate(ys, axis=0)) * (1.0 / N)
        ycs = [ys[gi] - mean[gi * RB:(gi + 1) * RB] for gi in range(NG)]
        yield
        var = head_sum(jnp.concatenate([yc * yc for yc in ycs], axis=0)) * (1.0 / N)
        yn = jnp.concatenate([ycs[gi] * lax.rsqrt(var[gi * RB:(gi + 1) * RB] + RWKV_GN_EPS)
                              for gi in range(NG)], axis=1)
        out = yn * gng_ref[...] + gnb_ref[...] + blk["bonus"] * blk["v"]
        y_ref[0, rows, :] = (out * blk["g"]).astype(y_ref.dtype)

    blocks_per_round = RB // C * max(RWKV_CHUNK_STAGGER, 1)
    _run_interleaved(
        [scan_gen()] + [prologue_gen(b) for b in range(n_blocks)]
        + [chunk_gen(c) for c in range(n_chunks)] + [epilogue_gen(b) for b in range(n_blocks)],
        starts=[0] + [b * blocks_per_round for b in range(n_blocks)]
        + [c * RWKV_CHUNK_STAGGER for c in range(n_chunks)] + [0] * n_blocks)
    for gi in range(NG):
        state_ref[gi] = states[gi]
    carry_ref[...] = proj[n_blocks - 1][RB - 1:RB]


def _rwkv_mix(h, pre_g, w_proj, mu, w0, w_up, a0, a_up, g_up, k_k, k_a, r_k, gn_g, gn_b, *,
              ts):
    B, S, D = h.shape
    C, H, N, W = RWKV_CHUNK, RWKV_HEADS, RWKV_HEAD_DIM, RWKV_WIDTH
    assert S % ts == 0 and ts % min(RWKV_BLOCK_ROWS, ts) == 0 and RWKV_BLOCK_ROWS % C == 0
    n_chunks = ts // C
    NG, GL = H // RWKV_HEAD_GROUP, RWKV_HEAD_GROUP * N
    row = lambda t: t.reshape(1, -1).astype(F32)
    full = lambda shape: pl.BlockSpec(shape, lambda b, i: (0,) * len(shape),
                                      pipeline_mode=pl.Buffered(1))
    return pl.pallas_call(
        _rwkv_kernel,
        out_shape=jax.ShapeDtypeStruct((B, S, W), BF16),
        grid=(B, S // ts),
        in_specs=[
            pl.BlockSpec((1, ts, D), lambda b, i: (b, i, 0)),
            full((1, D)), full((D, RWKV_COLS)),
            full((1, RWKV_COLS)), full((1, W)), full((DECAY_LORA, W)), full((1, W)),
            full((AAA_LORA, W)), full((GATE_LORA, W)), full((1, W)), full((1, W)),
            full((1, W)), full((1, W)), full((1, W)),
        ],
        out_specs=pl.BlockSpec((1, ts, W), lambda b, i: (b, i, 0)),
        scratch_shapes=[
            pltpu.VMEM((1, RWKV_COLS), F32),
            pltpu.VMEM((NG, N, GL), F32),
            pltpu.VMEM((n_chunks, NG, C, GL), BF16),
            pltpu.VMEM((n_chunks, NG, C, GL), F32),
            pltpu.VMEM((n_chunks, NG, N, GL), BF16),
            pltpu.VMEM((n_chunks, NG, N, GL), F32),
            pltpu.VMEM((ts, W), F32),
        ],
        compiler_params=pltpu.CompilerParams(
            dimension_semantics=("arbitrary", "arbitrary"),
            vmem_limit_bytes=VMEM_LIMIT_BYTES),
        name="rwkv7_mix",
    )(h, row(pre_g), w_proj, row(mu), row(w0), w_up.astype(BF16), row(a0), a_up.astype(BF16),
      g_up.astype(BF16), row(k_k), row(k_a), row(r_k), row(gn_g), row(gn_b))


D_MODEL = 1024
D_FF = 2816
FFN_CHUNK = 256
FFN_SUBTILES = 2
FFN_STAGGER = 9


def _const_spec(shape):
    return pl.BlockSpec(shape, lambda *_: (0,) * len(shape), pipeline_mode=pl.Buffered(1))


def _swiglu_half_step(h_ref, pre_g, w_in_ref, w_out_ref, post_g, o_ref, rows):
    h = h_ref[rows, :]
    xn = _rms_norm(h, pre_g).astype(BF16)
    acc = jnp.zeros(h.shape, F32)
    yield
    for c in range(D_FF // FFN_CHUNK):
        lo = c * FFN_CHUNK
        gate = jnp.dot(xn, w_in_ref[:, lo:lo + FFN_CHUNK], preferred_element_type=F32)
        up = jnp.dot(xn, w_in_ref[:, D_FF + lo:D_FF + lo + FFN_CHUNK],
                     preferred_element_type=F32)
        act = (gate * _sigmoid(gate) * up).astype(BF16)
        acc = acc + jnp.dot(act, w_out_ref[lo:lo + FFN_CHUNK, :], preferred_element_type=F32)
        yield
    o_ref[rows, :] = h + HALF_STEP * _rms_norm(acc, post_g)


def _swiglu_tile(h_ref, pre_g, w_in_ref, w_out_ref, post_g, o_ref):
    sub = h_ref.shape[0] // FFN_SUBTILES
    _run_interleaved(
        [_swiglu_half_step(h_ref, pre_g, w_in_ref, w_out_ref, post_g, o_ref,
                           slice(s * sub, (s + 1) * sub)) for s in range(FFN_SUBTILES)],
        starts=[s * FFN_STAGGER for s in range(FFN_SUBTILES)])


def _ffn_kernel(col_splits, h_ref, pre_g_ref, w_in_ref, w_out_ref, post_g_ref, *refs):
    n_jobs = len(col_splits)
    cast_in, o_ref, cast_out = refs[:n_jobs], refs[n_jobs], refs[n_jobs + 1:]
    k = 0
    for src_ref, splits in zip(cast_in, col_splits):
        for lo, hi in splits:
            cast_out[k][...] = src_ref[:, lo:hi].astype(BF16)
            k += 1
    _swiglu_tile(h_ref, pre_g_ref[...], w_in_ref, w_out_ref, post_g_ref[...], o_ref)


def _slab_rows(n_rows, n_steps):
    rows = BF16_SUBLANES
    while n_rows % rows or rows * n_steps < n_rows:
        rows += BF16_SUBLANES
    return rows


def _ffn(h, pre_g, w_in, w_out, post_g, *, tm, cast_jobs=()):
    T, D = h.shape
    assert T % tm == 0 and tm % (FFN_SUBTILES * BF16_SUBLANES) == 0
    n_steps = T // tm
    tile = pl.BlockSpec((tm, D), lambda i: (i, 0))
    cast_in_specs, cast_out_specs, cast_out_shapes = [], [], []
    for w, splits in cast_jobs:
        rows = _slab_rows(w.shape[0], n_steps)
        last = w.shape[0] // rows - 1
        index = lambda i, last=last: (jnp.minimum(i, last), 0)
        cast_in_specs.append(pl.BlockSpec((rows, w.shape[1]), index))
        for lo, hi in splits:
            cast_out_specs.append(pl.BlockSpec((rows, hi - lo), index))
            cast_out_shapes.append(jax.ShapeDtypeStruct((w.shape[0], hi - lo), BF16))
    outs = pl.pallas_call(
        functools.partial(_ffn_kernel, tuple(tuple(s) for _, s in cast_jobs)),
        out_shape=[jax.ShapeDtypeStruct((T, D), F32)] + cast_out_shapes,
        grid=(n_steps,),
        in_specs=[tile, _const_spec((1, D)), _const_spec((D, 2 * D_FF)),
                  _const_spec((D_FF, D)), _const_spec((1, D))] + cast_in_specs,
        out_specs=[tile] + cast_out_specs,
        compiler_params=pltpu.CompilerParams(
            dimension_semantics=("arbitrary",), vmem_limit_bytes=VMEM_LIMIT_BYTES),
        name="swiglu_half_step",
    )(h, pre_g.reshape(1, D), w_in.astype(BF16), w_out.astype(BF16), post_g.reshape(1, D),
      *[w for w, _ in cast_jobs])
    return outs


ATTN_COLS = SWA_COLS + MEM_WIDTH
BRANCH_COLS = RWKV_COLS + ATTN_COLS

def _rope_tables(pos_row):
    half = ROPE_DIM // 2
    f = lax.broadcasted_iota(jnp.int32, (half, 1), 0).astype(F32)
    inv_freq = jnp.exp(f * (-2.0 / ROPE_DIM * math.log(ROPE_THETA)))
    ang = inv_freq * pos_row
    cs = jnp.concatenate([jnp.cos(ang), jnp.sin(ang)], axis=0)
    src = lax.broadcasted_iota(jnp.int32, (2 * half, 2 * LANES), 0)
    col = lax.broadcasted_iota(jnp.int32, (2 * half, 2 * LANES), 1)
    d = col % SWA_HEAD_DIM
    is_sin = col >= LANES
    want = jnp.where(is_sin, half, 0) + d % half
    sign = jnp.where(is_sin & (d < half), -1.0, 1.0)
    sel = jnp.where((src == want) & (d < ROPE_DIM), sign, 0.0).astype(BF16)
    hi = cs.astype(BF16)
    lo = (cs - hi.astype(F32)).astype(BF16)
    tab = _dot_tn(hi, sel) + _dot_tn(lo, sel)
    lane_d = lax.broadcasted_iota(jnp.int32, (1, LANES), 1) % SWA_HEAD_DIM
    c = tab[:, :LANES] + jnp.where(lane_d < ROPE_DIM, 0.0, 1.0)
    return c, tab[:, LANES:]


def _apply_rope(x, c, s):
    half = ROPE_DIM // 2
    low = (lax.broadcasted_iota(jnp.int32, (1, LANES), 1) % SWA_HEAD_DIM) < half
    cols = []
    for j in range(x.shape[1] // LANES):
        xj = x[:, j * LANES:(j + 1) * LANES]
        partner = jnp.where(low, pltpu.roll(xj, LANES - half, axis=1),
                            pltpu.roll(xj, half, axis=1))
        cols.append(xj * c + partner * s)
    return cols[0] if len(cols) == 1 else jnp.concatenate(cols, axis=1)


def _kv_lane_variants(x):
    hd = SWA_HEAD_DIM
    low = lax.broadcasted_iota(jnp.int32, (1, LANES), 1) < hd
    xs = pltpu.roll(x, hd, axis=1)
    return [[jnp.where(low, x, 0.0).astype(BF16), jnp.where(low, 0.0, xs).astype(BF16)],
            [jnp.where(low, xs, 0.0).astype(BF16), jnp.where(low, 0.0, x).astype(BF16)]]


def _swa_attention(q, k, v, kc_ref, vc_ref, sinks_ref, first_tile):
    tq = q.shape[0]
    blk = WINDOW
    nb = tq // blk
    kv = _kv_lane_variants(k)
    vv = _kv_lane_variants(v)
    qi = lax.broadcasted_iota(jnp.int32, (blk, blk), 0)
    kj = lax.broadcasted_iota(jnp.int32, (blk, blk), 1)
    from_prev = kj > qi
    prev_bias = jnp.where(first_tile, NEG_INF, 0.0)

    def window(var, ref, g, e, j):
        prev = ref[2 * g + e] if j == 0 else var[g][e][(j - 1) * blk:j * blk]
        return [prev, var[g][e][j * blk:(j + 1) * blk]]

    pair_outs = [[None] * (2 * SWA_KV_HEADS) for _ in range(nb)]

    def block_group(j, g):
        rs = slice(j * blk, (j + 1) * blk)
        k_rhs = jnp.concatenate(window(kv, kc_ref, g, 0, j) + window(kv, kc_ref, g, 1, j),
                                axis=0)
        v_rhs = jnp.concatenate(window(vv, vc_ref, g, 0, j) + window(vv, vc_ref, g, 1, j),
                                axis=0)
        q_rows = jnp.concatenate([q[rs, (2 * g + r) * LANES:(2 * g + r + 1) * LANES]
                                  for r in range(2)], axis=0)
        s_all = _dot_nt(q_rows, k_rhs)
        yield
        probs = []
        for r in range(2):
            row_p = []
            for e in range(2):
                sink = sinks_ref[SWA_GROUP * g + 2 * r + e]
                s_prev = s_all[r * blk:(r + 1) * blk, 2 * e * blk:(2 * e + 1) * blk]
                s_cur = s_all[r * blk:(r + 1) * blk, (2 * e + 1) * blk:(2 * e + 2) * blk]
                if j == 0:
                    s_prev = s_prev + prev_bias
                s = jnp.where(from_prev, s_prev, s_cur)
                m = jnp.maximum(jnp.max(s, axis=-1, keepdims=True), sink)
                ex = jnp.exp(s - m)
                denom = jnp.sum(ex, axis=-1, keepdims=True) + jnp.exp(sink - m)
                pn = ex * (1.0 / denom)
                row_p += [jnp.where(from_prev, pn, 0.0).astype(BF16),
                          jnp.where(from_prev, 0.0, pn).astype(BF16)]
            probs.append(jnp.concatenate(row_p, axis=1))
            yield
        o = _dot(jnp.concatenate(probs, axis=0), v_rhs)
        pair_outs[j][2 * g] = o[:blk]
        pair_outs[j][2 * g + 1] = o[blk:]

    _run_interleaved([block_group(j, g) for j in range(nb) for g in range(SWA_KV_HEADS)])
    outs = [jnp.concatenate(po, axis=1) for po in pair_outs]
    for g in range(SWA_KV_HEADS):
        for e in range(2):
            kc_ref[2 * g + e] = kv[g][e][tq - blk:tq]
            vc_ref[2 * g + e] = vv[g][e][tq - blk:tq]
    return jnp.concatenate(outs, axis=0)


def _attn_kernel(sinks_ref, h_ref, pos_ref, g_ref, w_ref, km_ref, vm_ref, yb_ref, yc_ref,
                 kc_ref, vc_ref):
    first_tile = pl.program_id(1) == 0

    @pl.when(first_tile)
    def _():
        kc_ref[...] = jnp.zeros_like(kc_ref)
        vc_ref[...] = jnp.zeros_like(vc_ref)

    u = _rms_norm(h_ref[0], g_ref[...]).astype(BF16)
    p = jnp.dot(u, w_ref[...], preferred_element_type=F32)
    c, s = _rope_tables(pos_ref[0].astype(F32))
    q = (_apply_rope(p[:, 0:SWA_Q_WIDTH], c, s) * (SWA_HEAD_DIM ** -0.5)).astype(BF16)
    k = _apply_rope(p[:, SWA_Q_WIDTH:SWA_Q_WIDTH + SWA_KV_WIDTH], c, s)
    v = p[:, SWA_Q_WIDTH + SWA_KV_WIDTH:SWA_COLS]
    yb_ref[0] = _swa_attention(q, k, v, kc_ref, vc_ref, sinks_ref,
                               first_tile).astype(yb_ref.dtype)
    yc_ref[0] = _mem_attention(p[:, SWA_COLS:ATTN_COLS], km_ref[0],
                               vm_ref[0]).astype(yc_ref.dtype)


def _attn_branches(h, positions, sinks, g, w_attn, km, vm, *, tq):
    B, S, D = h.shape
    M = km.shape[1]
    assert S % tq == 0 and tq % WINDOW == 0
    n_carry = 2 * SWA_KV_HEADS
    const = lambda shape: pl.BlockSpec(shape, lambda b, i, *_: (0,) * len(shape),
                                       pipeline_mode=pl.Buffered(1))
    return pl.pallas_call(
        _attn_kernel,
        out_shape=(jax.ShapeDtypeStruct((B, S, SWA_Q_WIDTH), BF16),
                   jax.ShapeDtypeStruct((B, S, MEM_WIDTH), BF16)),
        grid_spec=pltpu.PrefetchScalarGridSpec(
            num_scalar_prefetch=1,
            grid=(B, S // tq),
            in_specs=[pl.BlockSpec((1, tq, D), lambda b, i, *_: (b, i, 0)),
                      pl.BlockSpec((1, 1, tq), lambda b, i, *_: (b, 0, i)),
                      const((1, D)), const((D, ATTN_COLS)),
                      pl.BlockSpec((1, M, MEM_WIDTH), lambda b, i, *_: (b, 0, 0)),
                      pl.BlockSpec((1, M, MEM_WIDTH), lambda b, i, *_: (b, 0, 0))],
            out_specs=(pl.BlockSpec((1, tq, SWA_Q_WIDTH), lambda b, i, *_: (b, i, 0)),
                       pl.BlockSpec((1, tq, MEM_WIDTH), lambda b, i, *_: (b, i, 0))),
            scratch_shapes=[pltpu.VMEM((n_carry, WINDOW, LANES), BF16),
                            pltpu.VMEM((n_carry, WINDOW, LANES), BF16)]),
        compiler_params=pltpu.CompilerParams(
            dimension_semantics=("arbitrary", "arbitrary"),
            vmem_limit_bytes=VMEM_LIMIT_BYTES),
        name="attn_branches",
    )(sinks.astype(F32), h, positions.reshape(B, 1, S), g.reshape(1, D), w_attn, km, vm)


def _mem_kv_kernel(mem_ref, g_ref, w_ref, km_ref, vm_ref):
    mn = _rms_norm(mem_ref[0], g_ref[...]).astype(BF16)
    km_ref[0] = jnp.dot(mn, w_ref[:, 0:MEM_WIDTH],
                        preferred_element_type=F32).astype(km_ref.dtype)
    vm_ref[0] = jnp.dot(mn, w_ref[:, MEM_WIDTH:2 * MEM_WIDTH],
                        preferred_element_type=F32).astype(vm_ref.dtype)


def _mem_kv(mem, g, w_kv):
    B, M, D = mem.shape
    out = jax.ShapeDtypeStruct((B, M, MEM_WIDTH), BF16)
    return pl.pallas_call(
        _mem_kv_kernel,
        out_shape=(out, out),
        grid=(B,),
        in_specs=[pl.BlockSpec((1, M, D), lambda b: (b, 0, 0)), _const_spec((1, D)),
                  _const_spec((D, 2 * MEM_WIDTH))],
        out_specs=(pl.BlockSpec((1, M, MEM_WIDTH), lambda b: (b, 0, 0)),
                   pl.BlockSpec((1, M, MEM_WIDTH), lambda b: (b, 0, 0))),
        compiler_params=pltpu.CompilerParams(
            dimension_semantics=("arbitrary",), vmem_limit_bytes=VMEM_LIMIT_BYTES),
        name="mem_kv_proj",
    )(mem, g.reshape(1, D), w_kv.astype(BF16))


def _mem_attention(q, km, vm):
    hd = MEM_HEAD_DIM
    q = (q * (hd ** -0.5)).astype(BF16)
    outs = []
    for h in range(MEM_HEADS):
        hs = slice(h * hd, (h + 1) * hd)
        s = _dot_nt(q[:, hs], km[:, hs])
        e = jnp.exp(s - jnp.max(s, axis=-1, keepdims=True))
        outs.append(_dot(e, vm[:, hs]) / jnp.sum(e, axis=-1, keepdims=True))
    return jnp.concatenate(outs, axis=1)


MERGE_SUBTILES = 2
MERGE_STAGGER = 3


def _merge_kernel(h_ref, ya_ref, yb_ref, yc_ref, pre_g_ref, wg_ref, gb_ref, wb_ref, wo_ref,
                  post_g_ref, o_ref):
    D = D_MODEL

    def sub_tile(rows):
        h = h_ref[rows, :]
        u = _rms_norm(h, pre_g_ref[...]).astype(BF16)
        merged = jnp.zeros(h.shape, F32)
        yield
        for i, y_ref in enumerate((ya_ref, yb_ref, yc_ref)):
            logits = jnp.dot(u, wg_ref[:, i * D:(i + 1) * D], preferred_element_type=F32)
            gate = _sigmoid(logits + gb_ref[:, i * D:(i + 1) * D])
            merged = merged + gate * jnp.dot(y_ref[rows, :], wb_ref[i],
                                             preferred_element_type=F32)
            yield
        out = jnp.dot(merged.astype(BF16), wo_ref[...], preferred_element_type=F32)
        yield
        o_ref[rows, :] = h + _rms_norm(out, post_g_ref[...])

    sub = h_ref.shape[0] // MERGE_SUBTILES
    _run_interleaved([sub_tile(slice(s * sub, (s + 1) * sub)) for s in range(MERGE_SUBTILES)],
                     starts=[s * MERGE_STAGGER for s in range(MERGE_SUBTILES)])


def _merge(h, y_a, y_b, y_c, pre_g, w_gate, gate_b, w_branch, w_out, post_g, *, tm):
    T, D = h.shape
    assert T % tm == 0 and tm % (MERGE_SUBTILES * BF16_SUBLANES) == 0
    tile = pl.BlockSpec((tm, D), lambda i: (i, 0))
    ytile = pl.BlockSpec((tm, BRANCH_WIDTH), lambda i: (i, 0))
    return pl.pallas_call(
        _merge_kernel,
        out_shape=jax.ShapeDtypeStruct((T, D), F32),
        grid=(T // tm,),
        in_specs=[tile, ytile, ytile, ytile, _const_spec((1, D)),
                  _const_spec((D, N_BRANCH * D)), _const_spec((1, N_BRANCH * D)),
                  _const_spec((N_BRANCH, BRANCH_WIDTH, D)), _const_spec((D, D)),
                  _const_spec((1, D))],
        out_specs=tile,
        compiler_params=pltpu.CompilerParams(
            dimension_semantics=("arbitrary",), vmem_limit_bytes=VMEM_LIMIT_BYTES),
        name="gated_merge",
    )(h, y_a, y_b, y_c, pre_g.reshape(1, D), w_gate, gate_b.reshape(1, N_BRANCH * D),
      w_branch.astype(BF16), w_out.astype(BF16), post_g.reshape(1, D))


def kernel(x, mem, positions, ffn1_pre_g, ffn1_w_in, ffn1_w_out, ffn1_post_g, mix_pre_g, w_in, gate_b, rwkv_mu, rwkv_w0, rwkv_w_up, rwkv_a0, rwkv_a_up, rwkv_g_up, rwkv_k_k, rwkv_k_a, rwkv_r_k, rwkv_gn_g, rwkv_gn_b, swa_sinks, mem_norm_g, mem_w_kv, w_branch, w_out, mix_post_g, ffn2_pre_g, ffn2_w_in, ffn2_w_out, ffn2_post_g):
    B, S, D = x.shape
    T = B * S
    tm = min(TOKEN_TILE, T)
    ts = min(TOKEN_TILE, S)
    h = x.reshape(T, D)
    for l in range(ffn1_pre_g.shape[0]):
        whole = lambda w: ((0, w.shape[1]),)
        cast_jobs = (
            (w_in[l], ((0, RWKV_COLS), (RWKV_COLS, BRANCH_COLS), (BRANCH_COLS, w_in.shape[2]))),
            (mem_w_kv[l], whole(mem_w_kv[l])),
            (w_branch[l].reshape(N_BRANCH * BRANCH_WIDTH, D), ((0, D),)),
            (w_out[l], whole(w_out[l])),
            (ffn2_w_in[l], whole(ffn2_w_in[l])),
            (ffn2_w_out[l], whole(ffn2_w_out[l])),
        )
        (h, w_rwkv, w_attn, w_gate, w_kv_bf, w_branch_bf, w_out_bf, ffn2_w_in_bf,
         ffn2_w_out_bf) = _ffn(h, ffn1_pre_g[l], ffn1_w_in[l], ffn1_w_out[l], ffn1_post_g[l],
                               tm=tm, cast_jobs=cast_jobs)

        h3 = h.reshape(B, S, D)
        y_a = _rwkv_mix(h3, mix_pre_g[l], w_rwkv, rwkv_mu[l], rwkv_w0[l],
                        rwkv_w_up[l], rwkv_a0[l], rwkv_a_up[l], rwkv_g_up[l], rwkv_k_k[l],
                        rwkv_k_a[l], rwkv_r_k[l].reshape(-1), rwkv_gn_g[l], rwkv_gn_b[l], ts=ts)
        km, vm = _mem_kv(mem, mem_norm_g[l], w_kv_bf)
        y_b, y_c = _attn_branches(h3, positions, swa_sinks[l], mix_pre_g[l], w_attn, km, vm,
                                  tq=ts)
        h = _merge(h, y_a.reshape(T, -1), y_b.reshape(T, -1), y_c.reshape(T, -1), mix_pre_g[l],
                   w_gate, gate_b[l], w_branch_bf.reshape(N_BRANCH, BRANCH_WIDTH, D), w_out_bf,
                   mix_post_g[l], tm=tm)

        h, = _ffn(h, ffn2_pre_g[l], ffn2_w_in_bf, ffn2_w_out_bf, ffn2_post_g[l], tm=tm)
    return h.reshape(B, S, D)
```

```python
import functools
import math

import jax
import jax.numpy as jnp
from jax import lax
from jax.experimental import pallas as pl
from jax.experimental.pallas import tpu as pltpu

F32 = jnp.float32
BF16 = jnp.bfloat16

LANES = 128
BF16_SUBLANES = 16
MXU_WIDTH = 256
VMEM_LIMIT_BYTES = 56 * 1024 * 1024
TOKEN_TILE = 1024

NORM_EPS = 1e-6
HALF_STEP = 0.5

RWKV_HEADS = 8
RWKV_HEAD_DIM = 64
RWKV_WIDTH = RWKV_HEADS * RWKV_HEAD_DIM
DECAY_LORA = 64
AAA_LORA = 64
GATE_LORA = 128
RWKV_COLS = 3 * RWKV_WIDTH + DECAY_LORA + AAA_LORA + GATE_LORA
RWKV_GN_EPS = 64e-5
RWKV_CHUNK = 64
RWKV_HEAD_GROUP = MXU_WIDTH // RWKV_HEAD_DIM
assert RWKV_CHUNK == RWKV_HEAD_DIM and RWKV_HEADS % RWKV_HEAD_GROUP == 0
RWKV_CHUNK_STAGGER = 1
RWKV_BLOCK_ROWS = 256

SWA_Q_HEADS = 8
SWA_KV_HEADS = 2
SWA_GROUP = SWA_Q_HEADS // SWA_KV_HEADS
SWA_HEAD_DIM = 64
SWA_Q_WIDTH = SWA_Q_HEADS * SWA_HEAD_DIM
SWA_KV_WIDTH = SWA_KV_HEADS * SWA_HEAD_DIM
SWA_COLS = SWA_Q_WIDTH + 2 * SWA_KV_WIDTH
WINDOW = 128
ROPE_THETA = 500000.0
ROPE_DIM = SWA_HEAD_DIM // 4
NEG_INF = -1e30

MEM_HEADS = 4
MEM_HEAD_DIM = 128
MEM_WIDTH = MEM_HEADS * MEM_HEAD_DIM

N_BRANCH = 3
BRANCH_WIDTH = 512


def _dot(a, b):
    return jnp.dot(a.astype(BF16), b.astype(BF16), preferred_element_type=F32)


def _dot_nt(a, b):
    return lax.dot_general(a.astype(BF16), b.astype(BF16), (((1,), (1,)), ((), ())),
                           preferred_element_type=F32)


def _dot_tn(a, b):
    return lax.dot_general(a.astype(BF16), b.astype(BF16), (((0,), (0,)), ((), ())),
                           preferred_element_type=F32)


def _split3(x):
    hi = x.astype(BF16)
    r1 = x - hi.astype(F32)
    mid = r1.astype(BF16)
    lo = (r1 - mid.astype(F32)).astype(BF16)
    return hi, mid, lo


def _rms_norm(x, g):
    return x * lax.rsqrt(jnp.mean(x * x, axis=-1, keepdims=True) + NORM_EPS) * g


def _sigmoid(x):
    return 1.0 / (1.0 + jnp.exp(-x))


def _run_interleaved(gens, starts=None):
    waiting = sorted(zip(starts or [0] * len(gens), range(len(gens))))
    active = []
    rnd = 0
    while waiting or active:
        while waiting and waiting[0][0] <= rnd:
            active.append(gens[waiting.pop(0)[1]])
        active = [gen for gen in active if next(gen, True) is None]
        rnd += 1


def _rwkv_kernel(h_ref, pre_g_ref, w_ref, mu_ref, w0_ref, wup_ref, a0_ref, aup_ref, gup_ref,
                 kk_ref, ka_ref, rk_ref, gng_ref, gnb_ref, y_ref, carry_ref, state_ref, q_scr,
                 y0_scr, m_scr, n_scr, yh_scr):
    C, H, N, GH = RWKV_CHUNK, RWKV_HEADS, RWKV_HEAD_DIM, RWKV_HEAD_GROUP
    W = RWKV_WIDTH
    ts = h_ref.shape[1]
    n_chunks = ts // C

    @pl.when(pl.program_id(1) == 0)
    def _():
        carry_ref[...] = jnp.zeros_like(carry_ref)
        state_ref[...] = jnp.zeros_like(state_ref)

    GL = GH * N
    row_blk = lax.broadcasted_iota(jnp.int32, (GL, GL), 0) // N
    col_blk = lax.broadcasted_iota(jnp.int32, (GL, GL), 1) // N
    same_head = (row_blk == col_blk).astype(BF16)
    ti = lax.broadcasted_iota(jnp.int32, (C, GL), 0)
    si = lax.broadcasted_iota(jnp.int32, (C, GL), 1) % N
    cat_incl = ti >= si
    cat_strict = ti > si
    cat_eye = ti == si
    ci = lax.broadcasted_iota(jnp.int32, (C, C), 0)
    cj = lax.broadcasted_iota(jnp.int32, (C, C), 1)
    tri_3 = jnp.concatenate([(ci >= cj).astype(BF16)] * 3, axis=1)

    heads_per_vreg = LANES // N
    lane_head = lax.broadcasted_iota(jnp.int32, (C, LANES), 1) // N
    keep_head = [(lane_head == e).astype(BF16) for e in range(heads_per_vreg)]
    zero_cols = jnp.zeros((C, LANES), BF16)

    def block_diag(x):
        xb = x.astype(BF16)
        row_blocks = []
        for h in range(GH):
            col, e = divmod(h, heads_per_vreg)
            cols = [zero_cols] * (GL // LANES)
            cols[col] = xb[:, col * LANES:(col + 1) * LANES] * keep_head[e]
            row_blocks.append(jnp.concatenate(cols, axis=1))
        return jnp.concatenate(row_blocks, axis=0)

    def head_sum(x):
        return jnp.dot(x.astype(BF16), same_head, preferred_element_type=F32)

    NG = H // GH
    group_lanes = [slice(gi * GL, (gi + 1) * GL) for gi in range(NG)]
    RB = min(RWKV_BLOCK_ROWS, ts)
    n_blocks = ts // RB
    blocks = [None] * n_blocks

    proj = [None] * n_blocks

    def prologue_gen(b):
        u = _rms_norm(h_ref[0, b * RB:(b + 1) * RB, :], pre_g_ref[...]).astype(BF16)
        p_b = jnp.dot(u, w_ref[...], preferred_element_type=F32)
        proj[b] = p_b
        yield
        prev_row = carry_ref[...] if b == 0 else proj[b - 1][RB - 1:RB]
        row = lax.broadcasted_iota(jnp.int32, p_b.shape, 0)
        p_prev = jnp.where(row == 0, prev_row, pltpu.roll(p_b, 1, axis=0))
        xs = p_b + (p_prev - p_b) * mu_ref[...]
        r = xs[:, 0:W]
        k = xs[:, W:2 * W]
        v = xs[:, 2 * W:3 * W]
        o = 3 * W
        wd = xs[:, o:o + DECAY_LORA]
        ad = xs[:, o + DECAY_LORA:o + DECAY_LORA + AAA_LORA]
        gd = xs[:, o + DECAY_LORA + AAA_LORA:o + DECAY_LORA + AAA_LORA + GATE_LORA]
        z = w0_ref[...] + _dot(jnp.tanh(wd), wup_ref[...])
        lw = -math.exp(-0.5) * _sigmoid(z)
        a = _sigmoid(a0_ref[...] + _dot(ad, aup_ref[...]))
        g = _dot(_sigmoid(gd), gup_ref[...])
        kk_raw = k * kk_ref[...]
        k2 = k * (1.0 + (a - 1.0) * ka_ref[...])
        rkr = r * k2 * rk_ref[...]
        yield
        sums = head_sum(jnp.concatenate([kk_raw[:, gl] * kk_raw[:, gl] for gl in group_lanes]
                                        + [rkr[:, gl] for gl in group_lanes], axis=0))
        ssq = jnp.concatenate([sums[gi * RB:(gi + 1) * RB] for gi in range(NG)], axis=1)
        bonus = jnp.concatenate([sums[(NG + gi) * RB:(NG + gi + 1) * RB] for gi in range(NG)],
                                axis=1)
        kk = kk_raw * lax.rsqrt(jnp.maximum(ssq, 1e-24))
        blocks[b] = dict(r=r, k2=k2, v=v, lw=lw, kk=kk, bvec=kk * a, g=g, bonus=bonus)

    def chunk_terms(c, gi, r_t, a_t, b_t, k_t, v_c, p_end):
        bd_b = block_diag(b_t)
        bd_k = block_diag(k_t)
        bd_v = block_diag(v_c)
        diag_p = jnp.where(cat_eye, p_end, 0.0)
        gram = _dot_nt(jnp.concatenate([a_t, r_t, diag_p], axis=0),
                       jnp.concatenate([bd_b, bd_k], axis=0))
        yield
        a_ab = jnp.where(cat_strict, gram[:C, :GL], 0.0)
        a_ak = jnp.where(cat_strict, gram[:C, GL:], 0.0)
        a_rb = jnp.where(cat_incl, gram[C:2 * C, :GL], 0.0)
        a_rk = jnp.where(cat_incl, gram[C:2 * C, GL:], 0.0)
        bt_end = gram[2 * C:, :GL]
        kt_end = gram[2 * C:, GL:]
        x = jnp.where(cat_eye, 1.0, a_ab)
        pw = _dot(a_ab, block_diag(a_ab))
        xv = _dot(jnp.concatenate([a_ak, a_rk, kt_end], axis=0), bd_v)
        akv, y0, n0 = xv[:C], xv[C:2 * C], xv[2 * C:]
        yield
        for _ in range(int(math.log2(C)) - 2):
            sq = _dot(jnp.concatenate([pw, x], axis=0), block_diag(pw))
            pw = sq[:C]
            x = x + sq[C:]
            yield
        x = x + _dot(x, block_diag(pw))
        yield
        lt = _dot(jnp.concatenate([a_rb, bt_end], axis=0), block_diag(x))
        yield
        qymn = _dot(lt, jnp.concatenate([block_diag(a_t), block_diag(akv)], axis=1))
        q_scr[c, gi] = (r_t + qymn[:C, :GL]).astype(BF16)
        y0_scr[c, gi] = qymn[:C, GL:] + y0
        m_scr[c, gi] = (diag_p + qymn[C:, :GL]).astype(BF16)
        n_scr[c, gi] = qymn[C:, GL:] + n0

    def chunk_gen(c):
        b, off = divmod(c * C, RB)
        while blocks[b] is None:
            yield
        blk = blocks[b]
        rows = slice(off, off + C)
        r, k2, v, lw, kk, bvec = (blk[name] for name in ("r", "k2", "v", "lw", "kk", "bvec"))
        lw_c = lw[rows]
        cum = jnp.dot(tri_3, jnp.concatenate(_split3(lw_c), axis=0),
                      preferred_element_type=F32)
        e_pos = jnp.exp(cum)
        e_neg = jnp.exp(-cum)
        e_prev = jnp.exp(cum - lw_c)
        p_end = jnp.exp(cum[C - 1:C, :])
        r_t = r[rows] * e_pos
        a_t = -kk[rows] * e_prev
        b_t = bvec[rows] * e_neg
        k_t = k2[rows] * e_neg
        v_c = v[rows]
        yield
        subs = [chunk_terms(c, gi, r_t[:, gl], a_t[:, gl], b_t[:, gl], k_t[:, gl], v_c[:, gl],
                            p_end[:, gl]) for gi, gl in enumerate(group_lanes)]
        while subs:
            subs = [sub for sub in subs if next(sub, True) is None]
            yield
        chunk_done[c] = True

    chunk_done = [False] * n_chunks
    scanned = [False] * n_chunks
    states = [state_ref[gi] for gi in range(NG)]

    def scan_gen():
        for c in range(n_chunks):
            while not chunk_done[c]:
                yield
            rows = slice(c * C, (c + 1) * C)
            for gi, gl in enumerate(group_lanes):
                res = _dot(jnp.concatenate([q_scr[c, gi], m_scr[c, gi]], axis=0),
                           block_diag(states[gi]))
                yh_scr[rows, gl] = res[:C] + y0_scr[c, gi]
                states[gi] = res[C:] + n_scr[c, gi]
            scanned[c] = True
            yield

    def epilogue_gen(b):
        while not scanned[(b + 1) * RB // C - 1]:
            yield
        rows = slice(b * RB, (b + 1) * RB)
        blk = blocks[b]
        ys = [yh_scr[rows, gl] for gl in group_lanes]
        mean = head_sum(jnp.concatenate(ys, axis=0)) * (1.0 / N)
        ycs = [ys[gi] - mean[gi * RB:(gi + 1) * RB] for gi in range(NG)]
        yield
        var = head_sum(jnp.concatenate([yc * yc for yc in ycs], axis=0)) * (1.0 / N)
        yn = jnp.concatenate([ycs[gi] * lax.rsqrt(var[gi * RB:(gi + 1) * RB] + RWKV_GN_EPS)
                              for gi in range(NG)], axis=1)
        out = yn * gng_ref[...] + gnb_ref[...] + blk["bonus"] * blk["v"]
        y_ref[0, rows, :] = (out * blk["g"]).astype(y_ref.dtype)

    blocks_per_round = RB // C * max(RWKV_CHUNK_STAGGER, 1)
    _run_interleaved(
        [scan_gen()] + [prologue_gen(b) for b in range(n_blocks)]
        + [chunk_gen(c) for c in range(n_chunks)] + [epilogue_gen(b) for b in range(n_blocks)],
        starts=[0] + [b * blocks_per_round for b in range(n_blocks)]
        + [c * RWKV_CHUNK_STAGGER for c in range(n_chunks)] + [0] * n_blocks)
    for gi in range(NG):
        state_ref[gi] = states[gi]
    carry_ref[...] = proj[n_blocks - 1][RB - 1:RB]


def _rwkv_mix(h, pre_g, w_proj, mu, w0, w_up, a0, a_up, g_up, k_k, k_a, r_k, gn_g, gn_b, *,
              ts):
    B, S, D = h.shape
    C, H, N, W = RWKV_CHUNK, RWKV_HEADS, RWKV_HEAD_DIM, RWKV_WIDTH
    assert S % ts == 0 and ts % min(RWKV_BLOCK_ROWS, ts) == 0 and RWKV_BLOCK_ROWS % C == 0
    n_chunks = ts // C
    NG, GL = H // RWKV_HEAD_GROUP, RWKV_HEAD_GROUP * N
    row = lambda t: t.reshape(1, -1).astype(F32)
    full = lambda shape: pl.BlockSpec(shape, lambda b, i: (0,) * len(shape),
                                      pipeline_mode=pl.Buffered(1))
    return pl.pallas_call(
        _rwkv_kernel,
        out_shape=jax.ShapeDtypeStruct((B, S, W), BF16),
        grid=(B, S // ts),
        in_specs=[
            pl.BlockSpec((1, ts, D), lambda b, i: (b, i, 0)),
            full((1, D)), full((D, RWKV_COLS)),
            full((1, RWKV_COLS)), full((1, W)), full((DECAY_LORA, W)), full((1, W)),
            full((AAA_LORA, W)), full((GATE_LORA, W)), full((1, W)), full((1, W)),
            full((1, W)), full((1, W)), full((1, W)),
        ],
        out_specs=pl.BlockSpec((1, ts, W), lambda b, i: (b, i, 0)),
        scratch_shapes=[
            pltpu.VMEM((1, RWKV_COLS), F32),
            pltpu.VMEM((NG, N, GL), F32),
            pltpu.VMEM((n_chunks, NG, C, GL), BF16),
            pltpu.VMEM((n_chunks, NG, C, GL), F32),
            pltpu.VMEM((n_chunks, NG, N, GL), BF16),
            pltpu.VMEM((n_chunks, NG, N, GL), F32),
            pltpu.VMEM((ts, W), F32),
        ],
        compiler_params=pltpu.CompilerParams(
            dimension_semantics=("arbitrary", "arbitrary"),
            vmem_limit_bytes=VMEM_LIMIT_BYTES),
        name="rwkv7_mix",
    )(h, row(pre_g), w_proj, row(mu), row(w0), w_up.astype(BF16), row(a0), a_up.astype(BF16),
      g_up.astype(BF16), row(k_k), row(k_a), row(r_k), row(gn_g), row(gn_b))


D_MODEL = 1024
D_FF = 2816
FFN_CHUNK = 256
FFN_SUBTILES = 2
FFN_STAGGER = 9


def _const_spec(shape):
    return pl.BlockSpec(shape, lambda *_: (0,) * len(shape), pipeline_mode=pl.Buffered(1))


def _swiglu_half_step(h_ref, pre_g, w_in_ref, w_out_ref, post_g, o_ref, rows):
    h = h_ref[rows, :]
    xn = _rms_norm(h, pre_g).astype(BF16)
    acc = jnp.zeros(h.shape, F32)
    yield
    for c in range(D_FF // FFN_CHUNK):
        lo = c * FFN_CHUNK
        gate = jnp.dot(xn, w_in_ref[:, lo:lo + FFN_CHUNK], preferred_element_type=F32)
        up = jnp.dot(xn, w_in_ref[:, D_FF + lo:D_FF + lo + FFN_CHUNK],
                     preferred_element_type=F32)
        act = (gate * _sigmoid(gate) * up).astype(BF16)
        acc = acc + jnp.dot(act, w_out_ref[lo:lo + FFN_CHUNK, :], preferred_element_type=F32)
        yield
    o_ref[rows, :] = h + HALF_STEP * _rms_norm(acc, post_g)


def _swiglu_tile(h_ref, pre_g, w_in_ref, w_out_ref, post_g, o_ref):
    sub = h_ref.shape[0] // FFN_SUBTILES
    _run_interleaved(
        [_swiglu_half_step(h_ref, pre_g, w_in_ref, w_out_ref, post_g, o_ref,
                           slice(s * sub, (s + 1) * sub)) for s in range(FFN_SUBTILES)],
        starts=[s * FFN_STAGGER for s in range(FFN_SUBTILES)])


def _ffn_kernel(col_splits, h_ref, pre_g_ref, w_in_ref, w_out_ref, post_g_ref, *refs):
    n_jobs = len(col_splits)
    cast_in, o_ref, cast_out = refs[:n_jobs], refs[n_jobs], refs[n_jobs + 1:]
    k = 0
    for src_ref, splits in zip(cast_in, col_splits):
        for lo, hi in splits:
            cast_out[k][...] = src_ref[:, lo:hi].astype(BF16)
            k += 1
    _swiglu_tile(h_ref, pre_g_ref[...], w_in_ref, w_out_ref, post_g_ref[...], o_ref)


def _slab_rows(n_rows, n_steps):
    rows = BF16_SUBLANES
    while n_rows % rows or rows * n_steps < n_rows:
        rows += BF16_SUBLANES
    return rows


def _ffn(h, pre_g, w_in, w_out, post_g, *, tm, cast_jobs=()):
    T, D = h.shape
    assert T % tm == 0 and tm % (FFN_SUBTILES * BF16_SUBLANES) == 0
    n_steps = T // tm
    tile = pl.BlockSpec((tm, D), lambda i: (i, 0))
    cast_in_specs, cast_out_specs, cast_out_shapes = [], [], []
    for w, splits in cast_jobs:
        rows = _slab_rows(w.shape[0], n_steps)
        last = w.shape[0] // rows - 1
        index = lambda i, last=last: (jnp.minimum(i, last), 0)
        cast_in_specs.append(pl.BlockSpec((rows, w.shape[1]), index))
        for lo, hi in splits:
            cast_out_specs.append(pl.BlockSpec((rows, hi - lo), index))
            cast_out_shapes.append(jax.ShapeDtypeStruct((w.shape[0], hi - lo), BF16))
    outs = pl.pallas_call(
        functools.partial(_ffn_kernel, tuple(tuple(s) for _, s in cast_jobs)),
        out_shape=[jax.ShapeDtypeStruct((T, D), F32)] + cast_out_shapes,
        grid=(n_steps,),
        in_specs=[tile, _const_spec((1, D)), _const_spec((D, 2 * D_FF)),
                  _const_spec((D_FF, D)), _const_spec((1, D))] + cast_in_specs,
        out_specs=[tile] + cast_out_specs,
        compiler_params=pltpu.CompilerParams(
            dimension_semantics=("arbitrary",), vmem_limit_bytes=VMEM_LIMIT_BYTES),
        name="swiglu_half_step",
    )(h, pre_g.reshape(1, D), w_in.astype(BF16), w_out.astype(BF16), post_g.reshape(1, D),
      *[w for w, _ in cast_jobs])
    return outs


ATTN_COLS = SWA_COLS + MEM_WIDTH
BRANCH_COLS = RWKV_COLS + ATTN_COLS

def _rope_tables(pos_row):
    half = ROPE_DIM // 2
    f = lax.broadcasted_iota(jnp.int32, (half, 1), 0).astype(F32)
    inv_freq = jnp.exp(f * (-2.0 / ROPE_DIM * math.log(ROPE_THETA)))
    ang = inv_freq * pos_row
    cs = jnp.concatenate([jnp.cos(ang), jnp.sin(ang)], axis=0)
    src = lax.broadcasted_iota(jnp.int32, (2 * half, 2 * LANES), 0)
    col = lax.broadcasted_iota(jnp.int32, (2 * half, 2 * LANES), 1)
    d = col % SWA_HEAD_DIM
    is_sin = col >= LANES
    want = jnp.where(is_sin, half, 0) + d % half
    sign = jnp.where(is_sin & (d < half), -1.0, 1.0)
    sel = jnp.where((src == want) & (d < ROPE_DIM), sign, 0.0).astype(BF16)
    hi = cs.astype(BF16)
    lo = (cs - hi.astype(F32)).astype(BF16)
    tab = _dot_tn(hi, sel) + _dot_tn(lo, sel)
    lane_d = lax.broadcasted_iota(jnp.int32, (1, LANES), 1) % SWA_HEAD_DIM
    c = tab[:, :LANES] + jnp.where(lane_d < ROPE_DIM, 0.0, 1.0)
    return c, tab[:, LANES:]


def _apply_rope(x, c, s):
    half = ROPE_DIM // 2
    low = (lax.broadcasted_iota(jnp.int32, (1, LANES), 1) % SWA_HEAD_DIM) < half
    cols = []
    for j in range(x.shape[1] // LANES):
        xj = x[:, j * LANES:(j + 1) * LANES]
        partner = jnp.where(low, pltpu.roll(xj, LANES - half, axis=1),
                            pltpu.roll(xj, half, axis=1))
        cols.append(xj * c + partner * s)
    return cols[0] if len(cols) == 1 else jnp.concatenate(cols, axis=1)


def _kv_lane_variants(x):
    hd = SWA_HEAD_DIM
    low = lax.broadcasted_iota(jnp.int32, (1, LANES), 1) < hd
    xs = pltpu.roll(x, hd, axis=1)
    return [[jnp.where(low, x, 0.0).astype(BF16), jnp.where(low, 0.0, xs).astype(BF16)],
            [jnp.where(low, xs, 0.0).astype(BF16), jnp.where(low, 0.0, x).astype(BF16)]]


def _swa_attention(q, k, v, kc_ref, vc_ref, sinks_ref, first_tile):
    tq = q.shape[0]
    blk = WINDOW
    nb = tq // blk
    kv = _kv_lane_variants(k)
    vv = _kv_lane_variants(v)
    qi = lax.broadcasted_iota(jnp.int32, (blk, blk), 0)
    kj = lax.broadcasted_iota(jnp.int32, (blk, blk), 1)
    from_prev = kj > qi
    prev_bias = jnp.where(first_tile, NEG_INF, 0.0)

    def window(var, ref, g, e, j):
        prev = ref[2 * g + e] if j == 0 else var[g][e][(j - 1) * blk:j * blk]
        return [prev, var[g][e][j * blk:(j + 1) * blk]]

    pair_outs = [[None] * (2 * SWA_KV_HEADS) for _ in range(nb)]

    def block_group(j, g):
        rs = slice(j * blk, (j + 1) * blk)
        k_rhs = jnp.concatenate(window(kv, kc_ref, g, 0, j) + window(kv, kc_ref, g, 1, j),
                                axis=0)
        v_rhs = jnp.concatenate(window(vv, vc_ref, g, 0, j) + window(vv, vc_ref, g, 1, j),
                                axis=0)
        q_rows = jnp.concatenate([q[rs, (2 * g + r) * LANES:(2 * g + r + 1) * LANES]
                                  for r in range(2)], axis=0)
        s_all = _dot_nt(q_rows, k_rhs)
        yield
        probs = []
        for r in range(2):
            row_p = []
            for e in range(2):
                sink = sinks_ref[SWA_GROUP * g + 2 * r + e]
                s_prev = s_all[r * blk:(r + 1) * blk, 2 * e * blk:(2 * e + 1) * blk]
                s_cur = s_all[r * blk:(r + 1) * blk, (2 * e + 1) * blk:(2 * e + 2) * blk]
                if j == 0:
                    s_prev = s_prev + prev_bias
                s = jnp.where(from_prev, s_prev, s_cur)
                m = jnp.maximum(jnp.max(s, axis=-1, keepdims=True), sink)
                ex = jnp.exp(s - m)
                denom = jnp.sum(ex, axis=-1, keepdims=True) + jnp.exp(sink - m)
                pn = ex * (1.0 / denom)
                row_p += [jnp.where(from_prev, pn, 0.0).astype(BF16),
                          jnp.where(from_prev, 0.0, pn).astype(BF16)]
            probs.append(jnp.concatenate(row_p, axis=1))
            yield
        o = _dot(jnp.concatenate(probs, axis=0), v_rhs)
        pair_outs[j][2 * g] = o[:blk]
        pair_outs[j][2 * g + 1] = o[blk:]

    _run_interleaved([block_group(j, g) for j in range(nb) for g in range(SWA_KV_HEADS)])
    outs = [jnp.concatenate(po, axis=1) for po in pair_outs]
    for g in range(SWA_KV_HEADS):
        for e in range(2):
            kc_ref[2 * g + e] = kv[g][e][tq - blk:tq]
            vc_ref[2 * g + e] = vv[g][e][tq - blk:tq]
    return jnp.concatenate(outs, axis=0)


def _attn_kernel(sinks_ref, h_ref, pos_ref, g_ref, w_ref, mem_ref, mem_g_ref, wkv_ref, yb_ref,
                 yc_ref, kc_ref, vc_ref, km_ref, vm_ref):
    first_tile = pl.program_id(1) == 0

    @pl.when(first_tile)
    def _():
        kc_ref[...] = jnp.zeros_like(kc_ref)
        vc_ref[...] = jnp.zeros_like(vc_ref)
        mn = _rms_norm(mem_ref[0], mem_g_ref[...]).astype(BF16)
        km_ref[0] = jnp.dot(mn, wkv_ref[:, 0:MEM_WIDTH],
                            preferred_element_type=F32).astype(BF16)
        vm_ref[0] = jnp.dot(mn, wkv_ref[:, MEM_WIDTH:2 * MEM_WIDTH],
                            preferred_element_type=F32).astype(BF16)

    u = _rms_norm(h_ref[0], g_ref[...]).astype(BF16)
    p = jnp.dot(u, w_ref[...], preferred_element_type=F32)
    c, s = _rope_tables(pos_ref[0].astype(F32))
    q = (_apply_rope(p[:, 0:SWA_Q_WIDTH], c, s) * (SWA_HEAD_DIM ** -0.5)).astype(BF16)
    k = _apply_rope(p[:, SWA_Q_WIDTH:SWA_Q_WIDTH + SWA_KV_WIDTH], c, s)
    v = p[:, SWA_Q_WIDTH + SWA_KV_WIDTH:SWA_COLS]
    yb_ref[0] = _swa_attention(q, k, v, kc_ref, vc_ref, sinks_ref,
                               first_tile).astype(yb_ref.dtype)
    yc_ref[0] = _mem_attention(p[:, SWA_COLS:ATTN_COLS], km_ref[0],
                               vm_ref[0]).astype(yc_ref.dtype)


def _attn_branches(h, positions, sinks, g, w_attn, mem, mem_g, w_kv, *, tq):
    B, S, D = h.shape
    M = mem.shape[1]
    assert S % tq == 0 and tq % WINDOW == 0
    n_carry = 2 * SWA_KV_HEADS
    const = lambda shape: pl.BlockSpec(shape, lambda b, i, *_: (0,) * len(shape),
                                       pipeline_mode=pl.Buffered(1))
    return pl.pallas_call(
        _attn_kernel,
        out_shape=(jax.ShapeDtypeStruct((B, S, SWA_Q_WIDTH), BF16),
                   jax.ShapeDtypeStruct((B, S, MEM_WIDTH), BF16)),
        grid_spec=pltpu.PrefetchScalarGridSpec(
            num_scalar_prefetch=1,
            grid=(B, S // tq),
            in_specs=[pl.BlockSpec((1, tq, D), lambda b, i, *_: (b, i, 0)),
                      pl.BlockSpec((1, 1, tq), lambda b, i, *_: (b, 0, i)),
                      const((1, D)), const((D, ATTN_COLS)),
                      pl.BlockSpec((1, M, D), lambda b, i, *_: (b, 0, 0)),
                      const((1, D)), const((D, 2 * MEM_WIDTH))],
            out_specs=(pl.BlockSpec((1, tq, SWA_Q_WIDTH), lambda b, i, *_: (b, i, 0)),
                       pl.BlockSpec((1, tq, MEM_WIDTH), lambda b, i, *_: (b, i, 0))),
            scratch_shapes=[pltpu.VMEM((n_carry, WINDOW, LANES), BF16),
                            pltpu.VMEM((n_carry, WINDOW, LANES), BF16),
                            pltpu.VMEM((1, M, MEM_WIDTH), BF16),
                            pltpu.VMEM((1, M, MEM_WIDTH), BF16)]),
        compiler_params=pltpu.CompilerParams(
            dimension_semantics=("arbitrary", "arbitrary"),
            vmem_limit_bytes=VMEM_LIMIT_BYTES),
        name="attn_branches",
    )(sinks.astype(F32), h, positions.reshape(B, 1, S), g.reshape(1, D), w_attn, mem,
      mem_g.reshape(1, D), w_kv)


def _mem_attention(q, km, vm):
    hd = MEM_HEAD_DIM
    q = (q * (hd ** -0.5)).astype(BF16)
    outs = []
    for h in range(MEM_HEADS):
        hs = slice(h * hd, (h + 1) * hd)
        s = _dot_nt(q[:, hs], km[:, hs])
        e = jnp.exp(s - jnp.max(s, axis=-1, keepdims=True))
        outs.append(_dot(e, vm[:, hs]) / jnp.sum(e, axis=-1, keepdims=True))
    return jnp.concatenate(outs, axis=1)


MERGE_SUBTILES = 2
MERGE_STAGGER = 3


def _merge_kernel(h_ref, ya_ref, yb_ref, yc_ref, pre_g_ref, wg_ref, gb_ref, wb_ref, wo_ref,
                  post_g_ref, o_ref):
    D = D_MODEL

    def sub_tile(rows):
        h = h_ref[rows, :]
        u = _rms_norm(h, pre_g_ref[...]).astype(BF16)
        merged = jnp.zeros(h.shape, F32)
        yield
        for i, y_ref in enumerate((ya_ref, yb_ref, yc_ref)):
            logits = jnp.dot(u, wg_ref[:, i * D:(i + 1) * D], preferred_element_type=F32)
            gate = _sigmoid(logits + gb_ref[:, i * D:(i + 1) * D])
            merged = merged + gate * jnp.dot(y_ref[rows, :], wb_ref[i],
                                             preferred_element_type=F32)
            yield
        out = jnp.dot(merged.astype(BF16), wo_ref[...], preferred_element_type=F32)
        yield
        o_ref[rows, :] = h + _rms_norm(out, post_g_ref[...])

    sub = h_ref.shape[0] // MERGE_SUBTILES
    _run_interleaved([sub_tile(slice(s * sub, (s + 1) * sub)) for s in range(MERGE_SUBTILES)],
                     starts=[s * MERGE_STAGGER for s in range(MERGE_SUBTILES)])


def _merge(h, y_a, y_b, y_c, pre_g, w_gate, gate_b, w_branch, w_out, post_g, *, tm):
    T, D = h.shape
    assert T % tm == 0 and tm % (MERGE_SUBTILES * BF16_SUBLANES) == 0
    tile = pl.BlockSpec((tm, D), lambda i: (i, 0))
    ytile = pl.BlockSpec((tm, BRANCH_WIDTH), lambda i: (i, 0))
    return pl.pallas_call(
        _merge_kernel,
        out_shape=jax.ShapeDtypeStruct((T, D), F32),
        grid=(T // tm,),
        in_specs=[tile, ytile, ytile, ytile, _const_spec((1, D)),
                  _const_spec((D, N_BRANCH * D)), _const_spec((1, N_BRANCH * D)),
                  _const_spec((N_BRANCH, BRANCH_WIDTH, D)), _const_spec((D, D)),
                  _const_spec((1, D))],
        out_specs=tile,
        compiler_params=pltpu.CompilerParams(
            dimension_semantics=("arbitrary",), vmem_limit_bytes=VMEM_LIMIT_BYTES),
        name="gated_merge",
    )(h, y_a, y_b, y_c, pre_g.reshape(1, D), w_gate, gate_b.reshape(1, N_BRANCH * D),
      w_branch.astype(BF16), w_out.astype(BF16), post_g.reshape(1, D))


def kernel(x, mem, positions, ffn1_pre_g, ffn1_w_in, ffn1_w_out, ffn1_post_g, mix_pre_g, w_in, gate_b, rwkv_mu, rwkv_w0, rwkv_w_up, rwkv_a0, rwkv_a_up, rwkv_g_up, rwkv_k_k, rwkv_k_a, rwkv_r_k, rwkv_gn_g, rwkv_gn_b, swa_sinks, mem_norm_g, mem_w_kv, w_branch, w_out, mix_post_g, ffn2_pre_g, ffn2_w_in, ffn2_w_out, ffn2_post_g):
    B, S, D = x.shape
    T = B * S
    tm = min(TOKEN_TILE, T)
    ts = min(TOKEN_TILE, S)
    h = x.reshape(T, D)
    for l in range(ffn1_pre_g.shape[0]):
        whole = lambda w: ((0, w.shape[1]),)
        cast_jobs = (
            (w_in[l], ((0, RWKV_COLS), (RWKV_COLS, BRANCH_COLS), (BRANCH_COLS, w_in.shape[2]))),
            (mem_w_kv[l], whole(mem_w_kv[l])),
            (w_branch[l].reshape(N_BRANCH * BRANCH_WIDTH, D), ((0, D),)),
            (w_out[l], whole(w_out[l])),
            (ffn2_w_in[l], whole(ffn2_w_in[l])),
            (ffn2_w_out[l], whole(ffn2_w_out[l])),
        )
        (h, w_rwkv, w_attn, w_gate, w_kv_bf, w_branch_bf, w_out_bf, ffn2_w_in_bf,
         ffn2_w_out_bf) = _ffn(h, ffn1_pre_g[l], ffn1_w_in[l], ffn1_w_out[l], ffn1_post_g[l],
                               tm=tm, cast_jobs=cast_jobs)

        h3 = h.reshape(B, S, D)
        y_a = _rwkv_mix(h3, mix_pre_g[l], w_rwkv, rwkv_mu[l], rwkv_w0[l],
                        rwkv_w_up[l], rwkv_a0[l], rwkv_a_up[l], rwkv_g_up[l], rwkv_k_k[l],
                        rwkv_k_a[l], rwkv_r_k[l].reshape(-1), rwkv_gn_g[l], rwkv_gn_b[l], ts=ts)
        y_b, y_c = _attn_branches(h3, positions, swa_sinks[l], mix_pre_g[l], w_attn, mem,
                                  mem_norm_g[l], w_kv_bf, tq=ts)
        h = _merge(h, y_a.reshape(T, -1), y_b.reshape(T, -1), y_c.reshape(T, -1), mix_pre_g[l],
                   w_gate, gate_b[l], w_branch_bf.reshape(N_BRANCH, BRANCH_WIDTH, D), w_out_bf,
                   mix_post_g[l], tm=tm)

        h, = _ffn(h, ffn2_pre_g[l], ffn2_w_in_bf, ffn2_w_out_bf, ffn2_post_g[l], tm=tm)
    return h.reshape(B, S, D)
```
